```python
import math
import jax, jax.numpy as jnp
from jax import lax
import numpy as np

D_MODEL = 1024
BATCH = 2
SEQ = 16384
DEPTH = 2

N_META = 16
N_HEADS = 16
N_KV_HEADS = 4
HEAD_DIM = 64
GROUP = N_HEADS // N_KV_HEADS
WINDOW = 128
BLOCK = 128
ATTN_SCALE = 1.0 / math.sqrt(HEAD_DIM)
Q_DIM = N_HEADS * HEAD_DIM
KV_DIM = N_KV_HEADS * HEAD_DIM
QKV_DIM = Q_DIM + 2 * KV_DIM
NEG_INF = -1e30
NUM_BUCKETS = 32
MAX_EXACT = NUM_BUCKETS // 2
MAX_DISTANCE = 128
RWKV_HEAD = 64
RWKV_HEADS = D_MODEL // RWKV_HEAD
D_DECAY_LORA = 64
D_AAA_LORA = 64
D_GATE_LORA = 160
GN_EPS = 64e-5
D_FF = int(round(8 * D_MODEL / 3 / 256)) * 256
N_EXPERTS = 8
TOP_K = 2
D_FF_EXPERT = 7 * D_MODEL // 2
ALPHA = (2 * DEPTH) ** 0.25
BETA = (8 * DEPTH) ** -0.25
LN_EPS = 1e-5
N_ATTN = (DEPTH + 1) // 2
N_RWKV = DEPTH // 2

kernel_name = "hybrid_swa_sink_rwkv7_moe_deepnorm"


def layer_norm(x, g, b):
    xf = x.astype(jnp.float32)
    mu = jnp.mean(xf, axis=-1, keepdims=True)
    var = jnp.mean(jnp.square(xf - mu), axis=-1, keepdims=True)
    y = (xf - mu) * lax.rsqrt(var + LN_EPS) * g.astype(jnp.float32) + b.astype(jnp.float32)
    return y.astype(x.dtype)


def t5_bucket(dist):
    n = jnp.maximum(dist, 0)
    is_small = n < MAX_EXACT
    nf = jnp.maximum(n, 1).astype(jnp.float32)
    large = MAX_EXACT + (jnp.log(nf / MAX_EXACT) / math.log(MAX_DISTANCE / MAX_EXACT)
                         * (NUM_BUCKETS - MAX_EXACT)).astype(jnp.int32)
    large = jnp.minimum(large, NUM_BUCKETS - 1)
    return jnp.where(is_small, n, large)


def sliding_window_attention(x, w_qkv, b_qkv, sinks, rel_bias, w_o):
    B, L, _ = x.shape
    pad = BLOCK - N_META
    qkv = jnp.einsum('bld,de->ble', x, w_qkv) + b_qkv
    qkv = jnp.pad(qkv, ((0, 0), (pad, 0), (0, 0)))
    Lp = L + pad
    nb = Lp // BLOCK
    q = qkv[..., :Q_DIM].reshape(B, nb, BLOCK, N_KV_HEADS, GROUP, HEAD_DIM) * ATTN_SCALE
    k = qkv[..., Q_DIM:Q_DIM + KV_DIM].reshape(B, nb, BLOCK, N_KV_HEADS, HEAD_DIM)
    v = qkv[..., Q_DIM + KV_DIM:].reshape(B, nb, BLOCK, N_KV_HEADS, HEAD_DIM)

    def band(t):
        prev = jnp.concatenate([jnp.zeros_like(t[:, :1]), t[:, :-1]], axis=1)
        return jnp.concatenate([prev, t], axis=2)

    k_band, v_band = band(k), band(v)
    k_meta = k[:, 0, pad:]
    v_meta = v[:, 0, pad:]

    s_band = jnp.einsum('bnqhgd,bnkhd->bnhgqk', q, k_band).astype(jnp.float32)
    s_meta = jnp.einsum('bnqhgd,bmhd->bnhgqm', q, k_meta).astype(jnp.float32)

    blk = jnp.arange(nb)[:, None]
    q_pos = blk * BLOCK + jnp.arange(BLOCK)[None, :] - pad
    kb_pos = (blk - 1) * BLOCK + jnp.arange(2 * BLOCK)[None, :] - pad
    d_band = BLOCK + jnp.arange(BLOCK)[:, None] - jnp.arange(2 * BLOCK)[None, :]
    ok_band = ((d_band >= 0) & (d_band < WINDOW))[None] & (kb_pos[:, None, :] >= 0)
    d_meta = q_pos[:, :, None] - jnp.arange(N_META)[None, None, :]
    ok_meta = d_meta >= WINDOW

    rb = rel_bias.astype(jnp.float32)
    bias_band = rb[t5_bucket(d_band)].transpose(2, 0, 1).reshape(N_KV_HEADS, GROUP, BLOCK, 2 * BLOCK)
    bias_meta = rb[t5_bucket(d_meta)].transpose(0, 3, 1, 2).reshape(nb, N_KV_HEADS, GROUP, BLOCK, N_META)

    s_band = jnp.where(ok_band[None, :, None, None], s_band + bias_band, NEG_INF)
    s_meta = jnp.where(ok_meta[None, :, None, None], s_meta + bias_meta[None], NEG_INF)

    sink = sinks.astype(jnp.float32).reshape(N_KV_HEADS, GROUP)[None, None, :, :, None]
    m = jnp.maximum(jnp.maximum(jnp.max(s_band, -1), jnp.max(s_meta, -1)), sink)
    p_band = jnp.exp(s_band - m[..., None])
    p_meta = jnp.exp(s_meta - m[..., None])
    denom = p_band.sum(-1) + p_meta.sum(-1) + jnp.exp(sink - m)
    o = (jnp.einsum('bnhgqk,bnkhd->bnqhgd', p_band, v_band.astype(jnp.float32))
         + jnp.einsum('bnhgqm,bmhd->bnqhgd', p_meta, v_meta.astype(jnp.float32)))
    o = o / denom.transpose(0, 1, 4, 2, 3)[..., None]
    o = o.reshape(B, Lp, Q_DIM)[:, pad:].astype(x.dtype)
    return jnp.einsum('ble,ed->bld', o, w_o)


def wkv7_scan(r, w, k, v, a, b):
    B, L, H, N = r.shape
    xs = tuple(t.astype(jnp.float32).transpose(1, 0, 2, 3) for t in (r, w, k, v, a, b))

    def step(S, inp):
        r_t, w_t, k_t, v_t, a_t, b_t = inp
        sa = jnp.einsum('bhij,bhj->bhi', S, a_t)
        S = S * w_t[:, :, None, :] + sa[..., None] * b_t[:, :, None, :] + v_t[..., None] * k_t[:, :, None, :]
        y = jnp.einsum('bhij,bhj->bhi', S, r_t)
        return S, y

    S0 = jnp.zeros((B, H, N, N), jnp.float32)
    _, ys = lax.scan(step, S0, xs)
    return ys.transpose(1, 0, 2, 3)


def rwkv7_time_mix(x, mu, w0, w1, w2, a0, a1, a2, g1, g2, k_k, k_a, r_k, w_rkv, lnx_g, lnx_b, w_o):
    B, L, D = x.shape
    H, N = RWKV_HEADS, RWKV_HEAD
    xx = jnp.pad(x, ((0, 0), (1, 0), (0, 0)))[:, :-1] - x
    xr = x + xx * mu[0]
    xw = x + xx * mu[1]
    xk = x + xx * mu[2]
    xv = x + xx * mu[3]
    xa = x + xx * mu[4]
    xg = x + xx * mu[5]
    r = xr @ w_rkv[0]
    k = xk @ w_rkv[1]
    v = xv @ w_rkv[2]
    w = -jax.nn.softplus(-(w0 + jnp.tanh(xw @ w1) @ w2).astype(jnp.float32)) - 0.5
    a = jax.nn.sigmoid((a0 + (xa @ a1) @ a2).astype(jnp.float32))
    g = jax.nn.sigmoid(xg @ g1) @ g2

    r = r.astype(jnp.float32).reshape(B, L, H, N)
    k = k.astype(jnp.float32)
    v = v.astype(jnp.float32).reshape(B, L, H, N)
    kk = (k * k_k.astype(jnp.float32)).reshape(B, L, H, N)
    kk = kk / jnp.maximum(jnp.sqrt(jnp.sum(kk * kk, -1, keepdims=True)), 1e-12)
    k = (k * (1.0 + (a - 1.0) * k_a.astype(jnp.float32))).reshape(B, L, H, N)
    a = a.reshape(B, L, H, N)
    decay = jnp.exp(-jnp.exp(w)).reshape(B, L, H, N)

    y = wkv7_scan(r, decay, k, v, -kk, kk * a)
    mu_y = jnp.mean(y, -1, keepdims=True)
    var_y = jnp.mean(jnp.square(y - mu_y), -1, keepdims=True)
    y = ((y - mu_y) * lax.rsqrt(var_y + GN_EPS)).reshape(B, L, D)
    y = y * lnx_g.astype(jnp.float32) + lnx_b.astype(jnp.float32)
    bonus = jnp.sum(r * k * r_k.astype(jnp.float32), -1, keepdims=True) * v
    y = (y + bonus.reshape(B, L, D)).astype(x.dtype)
    return (y * g) @ w_o


def swiglu(t, w_gu, w_down):
    gu = t @ w_gu
    gate, up = jnp.split(gu, 2, axis=-1)
    return (jax.nn.silu(gate) * up) @ w_down


def moe_swiglu(x, w_router, w_gu, w_down):
    B, L, D = x.shape
    t = x.reshape(B * L, D)
    logits = (t @ w_router).astype(jnp.float32)
    top_v, top_i = lax.top_k(logits, TOP_K)
    top_w = jax.nn.softmax(top_v, axis=-1)
    gates = jnp.sum(jax.nn.one_hot(top_i, N_EXPERTS, dtype=jnp.float32) * top_w[..., None], axis=1)
    out = jnp.zeros_like(t)
    for e in range(N_EXPERTS):
        out = out + gates[:, e:e + 1].astype(t.dtype) * swiglu(t, w_gu[e], w_down[e])
    return out.reshape(B, L, D)


def setup_inputs(seed: int = 0) -> dict:
    key = jax.random.key(seed)
    ks = iter(jax.random.split(key, 48))
    f32 = jnp.float32

    def nrm(shape, scale):
        return jax.random.normal(next(ks), shape, f32) * scale

    D = D_MODEL
    return {
        "x": nrm((BATCH, SEQ, D), 1.0),
        "meta_tokens": nrm((N_META, D), 1.0),
        "rel_bias": nrm((NUM_BUCKETS, N_HEADS), 0.5),
        "ln_mix_g": 1.0 + nrm((DEPTH, D), 0.02),
        "ln_mix_b": nrm((DEPTH, D), 0.02),
        "ln_ffn_g": 1.0 + nrm((DEPTH, D), 0.02),
        "ln_ffn_b": nrm((DEPTH, D), 0.02),
        "attn_w_qkv": nrm((N_ATTN, D, QKV_DIM), D ** -0.5),
        "attn_b_qkv": nrm((N_ATTN, QKV_DIM), 0.02),
        "attn_sinks": nrm((N_ATTN, N_HEADS), 0.5),
        "attn_w_o": nrm((N_ATTN, Q_DIM, D), Q_DIM ** -0.5 * BETA),
        "rwkv_mu": jax.random.uniform(next(ks), (N_RWKV, 6, D), f32),
        "rwkv_w0": jax.random.uniform(next(ks), (N_RWKV, D), f32, minval=-6.5, maxval=-1.5),
        "rwkv_w1": nrm((N_RWKV, D, D_DECAY_LORA), D ** -0.5),
        "rwkv_w2": nrm((N_RWKV, D_DECAY_LORA, D), 0.1 * D_DECAY_LORA ** -0.5),
        "rwkv_a0": nrm((N_RWKV, D), 0.1),
        "rwkv_a1": nrm((N_RWKV, D, D_AAA_LORA), D ** -0.5),
        "rwkv_a2": nrm((N_RWKV, D_AAA_LORA, D), 0.1 * D_AAA_LORA ** -0.5),
        "rwkv_g1": nrm((N_RWKV, D, D_GATE_LORA), D ** -0.5),
        "rwkv_g2": nrm((N_RWKV, D_GATE_LORA, D), D_GATE_LORA ** -0.5),
        "rwkv_k_k": 0.85 + nrm((N_RWKV, D), 0.02),
        "rwkv_k_a": 1.0 + nrm((N_RWKV, D), 0.02),
        "rwkv_r_k": -0.04 + nrm((N_RWKV, RWKV_HEADS, RWKV_HEAD), 0.1),
        "rwkv_w_rkv": nrm((N_RWKV, 3, D, D), D ** -0.5),
        "rwkv_lnx_g": 1.0 + nrm((N_RWKV, D), 0.02),
        "rwkv_lnx_b": nrm((N_RWKV, D), 0.02),
        "rwkv_w_o": nrm((N_RWKV, D, D), D ** -0.5 * BETA),
        "ffn_w_gu": nrm((N_ATTN, D, 2 * D_FF), D ** -0.5),
        "ffn_w_down": nrm((N_ATTN, D_FF, D), D_FF ** -0.5 * BETA),
        "moe_router": nrm((N_RWKV, D, N_EXPERTS), D ** -0.5),
        "moe_w_gu": nrm((N_RWKV, N_EXPERTS, D, 2 * D_FF_EXPERT), D ** -0.5),
        "moe_w_down": nrm((N_RWKV, N_EXPERTS, D_FF_EXPERT, D), D_FF_EXPERT ** -0.5 * BETA),
    }


def reference(x, meta_tokens, rel_bias, ln_mix_g, ln_mix_b, ln_ffn_g, ln_ffn_b,
              attn_w_qkv, attn_b_qkv, attn_sinks, attn_w_o,
              rwkv_mu, rwkv_w0, rwkv_w1, rwkv_w2, rwkv_a0, rwkv_a1, rwkv_a2,
              rwkv_g1, rwkv_g2, rwkv_k_k, rwkv_k_a, rwkv_r_k, rwkv_w_rkv,
              rwkv_lnx_g, rwkv_lnx_b, rwkv_w_o,
              ffn_w_gu, ffn_w_down, moe_router, moe_w_gu, moe_w_down):
    B = x.shape[0]
    meta = jnp.broadcast_to(meta_tokens.astype(x.dtype)[None], (B, N_META, D_MODEL))
    h = jnp.concatenate([meta, x], axis=1)
    for i in range(DEPTH):
        j = i // 2
        if i % 2 == 0:
            y = sliding_window_attention(h, attn_w_qkv[j], attn_b_qkv[j], attn_sinks[j], rel_bias, attn_w_o[j])
        else:
            y = rwkv7_time_mix(h, rwkv_mu[j], rwkv_w0[j], rwkv_w1[j], rwkv_w2[j], rwkv_a0[j], rwkv_a1[j],
                               rwkv_a2[j], rwkv_g1[j], rwkv_g2[j], rwkv_k_k[j], rwkv_k_a[j], rwkv_r_k[j],
                               rwkv_w_rkv[j], rwkv_lnx_g[j], rwkv_lnx_b[j], rwkv_w_o[j])
        h = layer_norm(ALPHA * h + y, ln_mix_g[i], ln_mix_b[i])
        if i % 2 == 0:
            y = swiglu(h, ffn_w_gu[j], ffn_w_down[j])
        else:
            y = moe_swiglu(h, moe_router[j], moe_w_gu[j], moe_w_down[j])
        h = layer_norm(ALPHA * h + y, ln_ffn_g[i], ln_ffn_b[i])
    return h[:, N_META:]
```

```python
import functools
import math

import jax
import jax.numpy as jnp
from jax import lax
from jax.experimental import pallas as pl
from jax.experimental.pallas import tpu as pltpu

F32 = jnp.float32
BF16 = jnp.bfloat16

N_META = 16
N_HEADS = 16
N_KV_HEADS = 4
HEAD_DIM = 64
GROUP = N_HEADS // N_KV_HEADS
WINDOW = 128
BLOCK = 128
PAD = BLOCK - N_META
ATTN_SCALE = 1.0 / math.sqrt(HEAD_DIM)
NEG_INF = -1e30
NUM_BUCKETS = 32
MAX_EXACT = NUM_BUCKETS // 2
MAX_DISTANCE = 128
RWKV_HEAD = 64
GN_EPS = 64e-5
N_EXPERTS = 8
DEPTH = 2
ALPHA = (2 * DEPTH) ** 0.25
LN_EPS = 1e-5

WKV_CHUNK = 64
LANES = 128
VMEM_LIMIT_BYTES = 56 * 1024 * 1024


def _params(*sem):
    return pltpu.CompilerParams(dimension_semantics=sem, vmem_limit_bytes=VMEM_LIMIT_BYTES)


def _pick(n, candidates):
    for c in candidates:
        if n % c == 0:
            return c
    raise ValueError(f"no tile in {candidates} divides {n}")


def _pad_row_mask(first_row, rows, lp, nbatch):
    r = first_row + lax.broadcasted_iota(jnp.int32, (rows, 1), 0)
    mask = r < PAD
    for bi in range(1, nbatch):
        mask = jnp.logical_or(mask, jnp.logical_and(r >= bi * lp, r < bi * lp + PAD))
    return mask


def _layer_norm(z, g, b):
    mu = jnp.mean(z, axis=-1, keepdims=True)
    zc = z - mu
    var = jnp.mean(zc * zc, axis=-1, keepdims=True)
    return zc * lax.rsqrt(var + LN_EPS) * g + b


def _dot(a, b):
    return jnp.dot(a.astype(BF16), b.astype(BF16), preferred_element_type=F32)


def _dot_nt(a, b):
    return lax.dot_general(a.astype(BF16), b.astype(BF16), (((1,), (1,)), ((), ())),
                           preferred_element_type=F32)


def _qkv_kernel(x_ref, w_ref, b_ref, s_ref, o_ref):
    acc = _dot(x_ref[...], w_ref[...])
    o_ref[...] = ((acc + b_ref[...]) * s_ref[...]).astype(o_ref.dtype)


def _qkv_proj(x, w, b, s):
    m, d = x.shape
    n = w.shape[1]
    tm = _pick(m, (768, 384, 256, 128))
    return pl.pallas_call(
        _qkv_kernel,
        grid=(m // tm,),
        in_specs=[pl.BlockSpec((tm, d), lambda i: (i, 0)),
                  pl.BlockSpec((d, n), lambda i: (0, 0)),
                  pl.BlockSpec((1, n), lambda i: (0, 0)),
                  pl.BlockSpec((1, n), lambda i: (0, 0))],
        out_specs=pl.BlockSpec((tm, n), lambda i: (i, 0)),
        out_shape=jax.ShapeDtypeStruct((m, n), BF16),
        compiler_params=_params("parallel"),
        name="qkv_proj",
    )(x, w, b, s)


def _attn_kernel(sink_ref, mbias_ref, cur_ref, prev_ref, meta_ref, bias_ref, o_ref):
    n = pl.program_id(1)
    q_dim = N_HEADS * HEAD_DIM
    kv_dim = N_KV_HEADS * HEAD_DIM
    cur = cur_ref[0]
    prev = prev_ref[0]
    meta = meta_ref[0][PAD:, :]

    col = lax.broadcasted_iota(jnp.int32, (1, 2 * BLOCK), 1)
    key_ok = (col + (n - 1) * BLOCK) >= PAD
    qrow = jnp.bitwise_and(lax.broadcasted_iota(jnp.int32, (GROUP * BLOCK, N_META), 0), BLOCK - 1)
    midx = lax.broadcasted_iota(jnp.int32, (GROUP * BLOCK, N_META), 1)
    meta_ok = (n * BLOCK + qrow - PAD - midx) >= WINDOW
    grp = jnp.right_shift(lax.broadcasted_iota(jnp.int32, (GROUP * BLOCK, 1), 0), int(math.log2(BLOCK)))

    for h in range(N_KV_HEADS):
        ks = slice(h * HEAD_DIM, (h + 1) * HEAD_DIM)
        vs = slice(kv_dim + h * HEAD_DIM, kv_dim + (h + 1) * HEAD_DIM)
        kb = jnp.concatenate([prev[:, ks], cur[:, q_dim + h * HEAD_DIM:q_dim + (h + 1) * HEAD_DIM]], axis=0)
        vb = jnp.concatenate([prev[:, vs], cur[:, q_dim + kv_dim + h * HEAD_DIM:
                                               q_dim + kv_dim + (h + 1) * HEAD_DIM]], axis=0)
        km = meta[:, ks]
        vm = meta[:, vs]
        q4 = jnp.concatenate(
            [cur[:, (h * GROUP + g) * HEAD_DIM:(h * GROUP + g + 1) * HEAD_DIM] for g in range(GROUP)], axis=0)

        sink_col = jnp.zeros((GROUP * BLOCK, 1), F32)
        mb_col = jnp.zeros((GROUP * BLOCK, 1), F32)
        for g in range(GROUP):
            sink_col = jnp.where(grp == g, sink_ref[h * GROUP + g], sink_col)
            mb_col = jnp.where(grp == g, mbias_ref[h * GROUP + g], mb_col)

        bias4 = bias_ref[h * GROUP:(h + 1) * GROUP].reshape(GROUP * BLOCK, 2 * BLOCK)
        s = jnp.where(key_ok, _dot_nt(q4, kb) + bias4, NEG_INF)
        sm = jnp.where(meta_ok, _dot_nt(q4, km) + mb_col, NEG_INF)
        mx = jnp.maximum(jnp.maximum(jnp.max(s, axis=-1, keepdims=True),
                                     jnp.max(sm, axis=-1, keepdims=True)), sink_col)
        p = jnp.exp(s - mx)
        pm = jnp.exp(sm - mx)
        denom = (jnp.sum(p, axis=-1, keepdims=True) + jnp.sum(pm, axis=-1, keepdims=True)
                 + jnp.exp(sink_col - mx))
        o4 = (_dot(p, vb) + _dot(pm, vm)) / denom
        for g in range(GROUP):
            hd = h * GROUP + g
            o_ref[0, :, hd * HEAD_DIM:(hd + 1) * HEAD_DIM] = o4[g * BLOCK:(g + 1) * BLOCK].astype(o_ref.dtype)


def _attention(qkv, bias_tbl, sinks, meta_bias):
    b, lp, e = qkv.shape
    nb = lp // BLOCK
    q_dim = N_HEADS * HEAD_DIM
    kv2 = 2 * N_KV_HEADS * HEAD_DIM
    kv_blk = q_dim // kv2
    smem = pl.BlockSpec(memory_space=pltpu.SMEM)
    return pl.pallas_call(
        _attn_kernel,
        grid=(b, nb),
        in_specs=[smem, smem,
                  pl.BlockSpec((1, BLOCK, e), lambda i, j: (i, j, 0)),
                  pl.BlockSpec((1, BLOCK, kv2), lambda i, j: (i, jnp.maximum(j - 1, 0), kv_blk)),
                  pl.BlockSpec((1, BLOCK, kv2), lambda i, j: (i, 0, kv_blk)),
                  pl.BlockSpec((N_HEADS, BLOCK, 2 * BLOCK), lambda i, j: (0, 0, 0))],
        out_specs=pl.BlockSpec((1, BLOCK, q_dim), lambda i, j: (i, j, 0)),
        out_shape=jax.ShapeDtypeStruct((b, lp, q_dim), BF16),
        compiler_params=_params("parallel", "parallel"),
        name="swa_attention",
    )(sinks, meta_bias, qkv, qkv, qkv, bias_tbl)


def _proj_ln_kernel(*refs, has_mult):
    if has_mult:
        a_ref, m_ref, w_ref, res_ref, g_ref, b_ref, o_ref = refs
        a = a_ref[...].astype(F32) * m_ref[...].astype(F32)
    else:
        a_ref, w_ref, res_ref, g_ref, b_ref, o_ref = refs
        a = a_ref[...]
    y = _dot(a, w_ref[...])
    o_ref[...] = _layer_norm(ALPHA * res_ref[...] + y, g_ref[...], b_ref[...])


def _proj_res_ln(a, w, res, g, b, mult=None):
    m, k = a.shape
    d = w.shape[1]
    tm = _pick(m, (768, 384, 256, 128))
    row = lambda i: (i, 0)
    fixed = lambda i: (0, 0)
    ins = [a] + ([mult] if mult is not None else []) + [w, res, g, b]
    specs = ([pl.BlockSpec((tm, k), row)] + ([pl.BlockSpec((tm, k), row)] if mult is not None else [])
             + [pl.BlockSpec((k, d), fixed), pl.BlockSpec((tm, d), row),
                pl.BlockSpec((1, d), fixed), pl.BlockSpec((1, d), fixed)])
    return pl.pallas_call(
        functools.partial(_proj_ln_kernel, has_mult=mult is not None),
        grid=(m // tm,),
        in_specs=specs,
        out_specs=pl.BlockSpec((tm, d), row),
        out_shape=jax.ShapeDtypeStruct((m, d), F32),
        compiler_params=_params("parallel"),
        name="proj_res_ln",
    )(*ins)


def _ffn_kernel(*refs, has_gates, zero_pad_rows, lp, nbatch, tm):
    if has_gates:
        x_ref, gate_ref, wg_ref, wu_ref, wd_ref, g_ref, b_ref, o_ref, acc_ref, xb_ref = refs
    else:
        x_ref, wg_ref, wu_ref, wd_ref, g_ref, b_ref, o_ref, acc_ref, xb_ref = refs
    i = pl.program_id(0)
    e = pl.program_id(1)
    f = pl.program_id(2)
    first = jnp.logical_and(e == 0, f == 0)
    last = jnp.logical_and(e == pl.num_programs(1) - 1, f == pl.num_programs(2) - 1)

    @pl.when(first)
    def _():
        acc_ref[...] = jnp.zeros_like(acc_ref)
        xb_ref[...] = x_ref[...].astype(BF16)

    xb = xb_ref[...]
    gate = jnp.dot(xb, wg_ref[0], preferred_element_type=F32)
    up = jnp.dot(xb, wu_ref[0], preferred_element_type=F32)
    act = gate * (1.0 / (1.0 + jnp.exp(-gate))) * up
    if has_gates:
        lane = lax.broadcasted_iota(jnp.int32, gate_ref.shape, 1)
        ge = jnp.sum(jnp.where(lane == e, gate_ref[...], 0.0), axis=-1, keepdims=True)
        act = act * ge
    acc_ref[...] += jnp.dot(act.astype(BF16), wd_ref[0], preferred_element_type=F32)

    @pl.when(last)
    def _():
        y = _layer_norm(ALPHA * x_ref[...] + acc_ref[...], g_ref[...], b_ref[...])
        if zero_pad_rows:
            y = jnp.where(_pad_row_mask(i * tm, tm, lp, nbatch), 0.0, y)
        o_ref[...] = y


def _ffn(x, w_gu, w_down, g, b, gates=None, zero_pad_rows=False, lp=0, nbatch=1):
    m, d = x.shape
    ne, ff = w_down.shape[0], w_down.shape[1]
    tm = _pick(m, (384, 256, 128))
    tf = _pick(ff, (1408, 896, 512, 256, 128))
    nf = ff // tf
    row = lambda i, e, f: (i, 0)
    fixed = lambda i, e, f: (0, 0)
    ins = [x] + ([gates] if gates is not None else []) + [w_gu, w_gu, w_down, g, b]
    specs = ([pl.BlockSpec((tm, d), row)]
             + ([pl.BlockSpec((tm, LANES), row)] if gates is not None else [])
             + [pl.BlockSpec((1, d, tf), lambda i, e, f: (e, 0, f)),
                pl.BlockSpec((1, d, tf), lambda i, e, f: (e, 0, f + nf)),
                pl.BlockSpec((1, tf, d), lambda i, e, f: (e, f, 0)),
                pl.BlockSpec((1, d), fixed), pl.BlockSpec((1, d), fixed)])
    return pl.pallas_call(
        functools.partial(_ffn_kernel, has_gates=gates is not None, zero_pad_rows=zero_pad_rows, lp=lp,
                          nbatch=nbatch, tm=tm),
        grid=(m // tm, ne, nf),
        in_specs=specs,
        out_specs=pl.BlockSpec((tm, d), row),
        out_shape=jax.ShapeDtypeStruct((m, d), F32),
        scratch_shapes=[pltpu.VMEM((tm, d), F32), pltpu.VMEM((tm, d), BF16)],
        compiler_params=_params("parallel", "arbitrary", "arbitrary"),
        name="swiglu_res_ln",
    )(*ins)


def _rwkv_proj_kernel(x_ref, xp_ref, mu_ref, wr_ref, wk_ref, wv_ref, w0_ref, w1_ref, w2_ref,
                      a0_ref, a1_ref, a2_ref, g1_ref, g2_ref,
                      r_ref, k_ref, v_ref, a_ref, ld_ref, g_ref, *, lp, nbatch):
    x = x_ref[...]
    tm = x.shape[0]
    rolled = pltpu.roll(x, 1, axis=0)
    prev_row = xp_ref[7:8, :]
    first = lax.broadcasted_iota(jnp.int32, (tm, 1), 0) == 0
    xx = jnp.where(first, prev_row, rolled) - x
    xx = jnp.where(_pad_row_mask(pl.program_id(0) * tm, tm, lp, nbatch), 0.0, xx)

    def mix(i):
        return x + xx * mu_ref[i:i + 1, :]

    r_ref[...] = _dot(mix(0), wr_ref[...])
    k_ref[...] = _dot(mix(2), wk_ref[...])
    v_ref[...] = _dot(mix(3), wv_ref[...])
    wl = w0_ref[...] + _dot(jnp.tanh(_dot(mix(1), w1_ref[...])), w2_ref[...])
    z = -wl
    softplus = jnp.maximum(z, 0.0) + jnp.log(1.0 + jnp.exp(-jnp.abs(z)))
    ld_ref[...] = -jnp.exp(-softplus - 0.5)
    al = a0_ref[...] + _dot(_dot(mix(4), a1_ref[...]), a2_ref[...])
    a_ref[...] = 1.0 / (1.0 + jnp.exp(-al))
    gl = _dot(mix(5), g1_ref[...])
    g_ref[...] = _dot(1.0 / (1.0 + jnp.exp(-gl)), g2_ref[...])


def _rwkv_proj(x, mu, w_rkv, w0, w1, w2, a0, a1, a2, g1, g2, lp, nbatch):
    m, d = x.shape
    tm = _pick(m, (384, 256, 128))
    row = lambda i: (i, 0)
    fixed = lambda i: (0, 0)
    full = lambda arr: pl.BlockSpec(arr.shape, fixed)
    out = jax.ShapeDtypeStruct((m, d), F32)
    return pl.pallas_call(
        functools.partial(_rwkv_proj_kernel, lp=lp, nbatch=nbatch),
        grid=(m // tm,),
        in_specs=[pl.BlockSpec((tm, d), row),
                  pl.BlockSpec((8, d), lambda i: (jnp.maximum(i * (tm // 8) - 1, 0), 0)),
                  full(mu), full(w_rkv[0]), full(w_rkv[1]), full(w_rkv[2]),
                  full(w0), full(w1), full(w2), full(a0), full(a1), full(a2), full(g1), full(g2)],
        out_specs=[pl.BlockSpec((tm, d), row)] * 6,
        out_shape=[out] * 6,
        compiler_params=_params("parallel"),
        name="rwkv_proj",
    )(x, x, mu, w_rkv[0], w_rkv[1], w_rkv[2], w0, w1, w2, a0, a1, a2, g1, g2)


def _bmm(a, b):
    return jnp.einsum("cij,cjk->cik", a.astype(BF16), b.astype(BF16), preferred_element_type=F32)


def _bmm_nt(a, b):
    return jnp.einsum("cik,cjk->cij", a.astype(BF16), b.astype(BF16), preferred_element_type=F32)


def _bmm_tn(a, b):
    return jnp.einsum("cki,ckj->cij", a.astype(BF16), b.astype(BF16), preferred_element_type=F32)


def _wkv_kernel(r_ref, k_ref, v_ref, a_ref, ld_ref, kk_ref, ka_ref, rk_ref, gg_ref, gb_ref,
                y_ref, z_ref):
    c = WKV_CHUNK
    n = RWKV_HEAD
    tb = r_ref.shape[1]
    nc = tb // c

    @pl.when(pl.program_id(2) == 0)
    def _():
        z_ref[...] = jnp.zeros_like(z_ref)

    ti = lax.broadcasted_iota(jnp.int32, (1, c, c), 1)
    si = lax.broadcasted_iota(jnp.int32, (1, c, c), 2)
    strict = ti > si
    incl = ti >= si
    eye = ti == si
    tri = jnp.broadcast_to(jnp.where(incl, 1.0, 0.0).astype(BF16), (nc, c, c))

    for hh in range(LANES // n):
        sl = slice(hh * n, (hh + 1) * n)
        r = r_ref[0, :, sl].reshape(nc, c, n)
        k = k_ref[0, :, sl].reshape(nc, c, n)
        v = v_ref[0, :, sl].reshape(nc, c, n)
        a = a_ref[0, :, sl].reshape(nc, c, n)
        ld = ld_ref[0, :, sl].reshape(nc, c, n)
        k_k = kk_ref[:, sl].reshape(1, 1, n)
        k_a = ka_ref[:, sl].reshape(1, 1, n)
        r_k = rk_ref[:, sl].reshape(1, 1, n)

        kkr = k * k_k
        kk = kkr / jnp.maximum(jnp.sqrt(jnp.sum(kkr * kkr, axis=-1, keepdims=True)), 1e-12)
        km = k * (1.0 + (a - 1.0) * k_a)
        bv = kk * a

        ld_hi = ld.astype(BF16)
        ld_r1 = ld - ld_hi.astype(F32)
        ld_mid = ld_r1.astype(BF16)
        ld_lo = (ld_r1 - ld_mid.astype(F32)).astype(BF16)
        cs = (jnp.einsum("cij,cjk->cik", tri, ld_hi, preferred_element_type=F32)
              + jnp.einsum("cij,cjk->cik", tri, ld_mid, preferred_element_type=F32)
              + jnp.einsum("cij,cjk->cik", tri, ld_lo, preferred_element_type=F32))
        cs_last = cs[:, c - 1:c, :]
        e_neg = jnp.exp(-cs)
        e_tail = jnp.exp(cs_last - cs)
        at = -kk * jnp.exp(cs - ld)
        rt = r * jnp.exp(cs)
        bt = bv * e_neg
        kt = km * e_neg
        bh = bv * e_tail
        kh = km * e_tail
        dec = jnp.exp(cs_last)

        x2 = jnp.concatenate([at, rt], axis=1)
        gb_ = _bmm_nt(x2, bt)
        gk_ = _bmm_nt(x2, kt)
        lab = jnp.where(strict, gb_[:, :c], 0.0)
        mrb = jnp.where(incl, gb_[:, c:], 0.0)
        lak = jnp.where(strict, gk_[:, :c], 0.0)
        mrk = jnp.where(incl, gk_[:, c:], 0.0)

        t_inv = jnp.where(eye, 1.0, lab)
        lp = lab
        steps = int(math.log2(c)) - 1
        for _ in range(steps):
            lp = _bmm(lp, lp)
            t_inv = t_inv + _bmm(t_inv, lp)

        w = _bmm(t_inv, at)
        u0 = _bmm(t_inv, _bmm(lak, v))
        q = rt + _bmm(mrb, w)
        yp = _bmm(mrb, u0) + _bmm(mrk, v)
        a_t = jnp.where(eye, dec, 0.0) + _bmm_tn(bh, w)
        g_t = _bmm_tn(bh, u0) + _bmm_tn(kh, v)

        z = z_ref[hh]
        ys = []
        for ci in range(nc):
            ys.append(yp[ci] + _dot(q[ci], z))
            z = _dot(a_t[ci], z) + g_t[ci]
        z_ref[hh] = z
        y = jnp.concatenate(ys, axis=0)

        mu = jnp.mean(y, axis=-1, keepdims=True)
        yc = y - mu
        var = jnp.mean(yc * yc, axis=-1, keepdims=True)
        yn = yc * lax.rsqrt(var + GN_EPS) * gg_ref[:, sl] + gb_ref[:, sl]
        rf = r.reshape(tb, n)
        kf = km.reshape(tb, n)
        vf = v.reshape(tb, n)
        bonus = jnp.sum(rf * kf * rk_ref[:, sl], axis=-1, keepdims=True) * vf
        y_ref[0, :, sl] = yn + bonus


def _wkv(r, k, v, a, ld, k_k, k_a, r_k, gn_g, gn_b):
    b, lp, d = r.shape
    tb = _pick(lp, (384, 256, 128))
    tok = lambda i, j, t: (i, t, j)
    par = lambda i, j, t: (0, j)
    tok_spec = pl.BlockSpec((1, tb, LANES), tok)
    par_spec = pl.BlockSpec((1, LANES), par)
    return pl.pallas_call(
        _wkv_kernel,
        grid=(b, d // LANES, lp // tb),
        in_specs=[tok_spec] * 5 + [par_spec] * 5,
        out_specs=tok_spec,
        out_shape=jax.ShapeDtypeStruct((b, lp, d), F32),
        scratch_shapes=[pltpu.VMEM((LANES // RWKV_HEAD, RWKV_HEAD, RWKV_HEAD), F32)],
        compiler_params=_params("parallel", "parallel", "arbitrary"),
        name="wkv7_chunked",
    )(r, k, v, a, ld, k_k, k_a, r_k, gn_g, gn_b)


def _router_kernel(x_ref, w_ref, o_ref):
    logits = jnp.dot(x_ref[...], w_ref[...], preferred_element_type=F32, precision=lax.Precision.HIGHEST)
    lane = lax.broadcasted_iota(jnp.int32, logits.shape, 1).astype(F32)
    neg = -jnp.inf
    lg = jnp.where(lane < N_EXPERTS, logits, neg)
    m1 = jnp.max(lg, axis=-1, keepdims=True)
    i1 = jnp.min(jnp.where(lg == m1, lane, float(LANES)), axis=-1, keepdims=True)
    lg2 = jnp.where(lane == i1, neg, lg)
    m2 = jnp.max(lg2, axis=-1, keepdims=True)
    i2 = jnp.min(jnp.where(lg2 == m2, lane, float(LANES)), axis=-1, keepdims=True)
    e2 = jnp.exp(m2 - m1)
    den = 1.0 + e2
    o_ref[...] = jnp.where(lane == i1, 1.0 / den, 0.0) + jnp.where(lane == i2, e2 / den, 0.0)


def _router(x, w_pad):
    m, d = x.shape
    tm = _pick(m, (768, 384, 256, 128))
    return pl.pallas_call(
        _router_kernel,
        grid=(m // tm,),
        in_specs=[pl.BlockSpec((tm, d), lambda i: (i, 0)), pl.BlockSpec((d, LANES), lambda i: (0, 0))],
        out_specs=pl.BlockSpec((tm, LANES), lambda i: (i, 0)),
        out_shape=jax.ShapeDtypeStruct((m, LANES), F32),
        compiler_params=_params("parallel"),
        name="moe_router",
    )(x, w_pad)


def _t5_bucket(dist):
    n = jnp.maximum(dist, 0)
    is_small = n < MAX_EXACT
    nf = jnp.maximum(n, 1).astype(F32)
    large = MAX_EXACT + (jnp.log(nf / MAX_EXACT) / math.log(MAX_DISTANCE / MAX_EXACT)
                         * (NUM_BUCKETS - MAX_EXACT)).astype(jnp.int32)
    large = jnp.minimum(large, NUM_BUCKETS - 1)
    return jnp.where(is_small, n, large)


def _bias_tables(rel_bias):
    rb = rel_bias.astype(F32)
    d_band = BLOCK + jnp.arange(BLOCK)[:, None] - jnp.arange(2 * BLOCK)[None, :]
    ok = (d_band >= 0) & (d_band < WINDOW)
    band = jnp.where(ok[..., None], rb[_t5_bucket(d_band)], NEG_INF).transpose(2, 0, 1)
    meta = rb[_t5_bucket(jnp.array(WINDOW))]
    return band, meta


def kernel(x, meta_tokens, rel_bias, ln_mix_g, ln_mix_b, ln_ffn_g, ln_ffn_b, attn_w_qkv, attn_b_qkv, attn_sinks, attn_w_o, rwkv_mu, rwkv_w0, rwkv_w1, rwkv_w2, rwkv_a0, rwkv_a1, rwkv_a2, rwkv_g1, rwkv_g2, rwkv_k_k, rwkv_k_a, rwkv_r_k, rwkv_w_rkv, rwkv_lnx_g, rwkv_lnx_b, rwkv_w_o, ffn_w_gu, ffn_w_down, moe_router, moe_w_gu, moe_w_down):
    b, seq, d = x.shape
    lp = seq + BLOCK
    m = b * lp
    row = lambda t: t.reshape(1, -1).astype(F32)

    meta = jnp.broadcast_to(meta_tokens.astype(x.dtype)[None], (b, N_META, d))
    h = jnp.concatenate([jnp.zeros((b, PAD, d), x.dtype), meta, x], axis=1).reshape(m, d)

    q_dim = N_HEADS * HEAD_DIM
    e_dim = attn_w_qkv.shape[2]
    col_scale = jnp.where(jnp.arange(e_dim) < q_dim, ATTN_SCALE, 1.0).astype(F32)
    qkv = _qkv_proj(h, attn_w_qkv[0].astype(BF16), row(attn_b_qkv[0]), row(col_scale))
    bias_band, bias_meta = _bias_tables(rel_bias)
    o = _attention(qkv.reshape(b, lp, e_dim), bias_band, attn_sinks[0].astype(F32), bias_meta)
    h = _proj_res_ln(o.reshape(m, q_dim), attn_w_o[0].astype(BF16), h, row(ln_mix_g[0]), row(ln_mix_b[0]))
    h = _ffn(h, ffn_w_gu[0:1].astype(BF16), ffn_w_down[0:1].astype(BF16), row(ln_ffn_g[0]), row(ln_ffn_b[0]),
             zero_pad_rows=True, lp=lp, nbatch=b)

    r, k, v, a, ld, g = _rwkv_proj(
        h, rwkv_mu[0], rwkv_w_rkv[0].astype(BF16), row(rwkv_w0[0]), rwkv_w1[0].astype(BF16),
        rwkv_w2[0].astype(BF16), row(rwkv_a0[0]), rwkv_a1[0].astype(BF16), rwkv_a2[0].astype(BF16),
        rwkv_g1[0].astype(BF16), rwkv_g2[0].astype(BF16), lp, b)
    t3 = lambda t: t.reshape(b, lp, d)
    y = _wkv(t3(r), t3(k), t3(v), t3(a), t3(ld), row(rwkv_k_k[0]), row(rwkv_k_a[0]), row(rwkv_r_k[0]),
             row(rwkv_lnx_g[0]), row(rwkv_lnx_b[0]))
    h = _proj_res_ln(y.reshape(m, d), rwkv_w_o[0].astype(BF16), h, row(ln_mix_g[1]), row(ln_mix_b[1]), mult=g)
    w_router = jnp.pad(moe_router[0].astype(F32), ((0, 0), (0, LANES - N_EXPERTS)))
    gates = _router(h, w_router)
    h = _ffn(h, moe_w_gu[0].astype(BF16), moe_w_down[0].astype(BF16), row(ln_ffn_g[1]), row(ln_ffn_b[1]),
             gates=gates)
    return h.reshape(b, lp, d)[:, BLOCK:]
```

```python
import functools
import math

import jax
import jax.numpy as jnp
from jax import lax
from jax.experimental import pallas as pl
from jax.experimental.pallas import tpu as pltpu

F32 = jnp.float32
BF16 = jnp.bfloat16

N_META = 16
N_HEADS = 16
N_KV_HEADS = 4
HEAD_DIM = 64
GROUP = N_HEADS // N_KV_HEADS
WINDOW = 128
BLOCK = 128
PAD = BLOCK - N_META
ATTN_SCALE = 1.0 / math.sqrt(HEAD_DIM)
NEG_INF = -1e30
NUM_BUCKETS = 32
MAX_EXACT = NUM_BUCKETS // 2
MAX_DISTANCE = 128
RWKV_HEAD = 64
GN_EPS = 64e-5
N_EXPERTS = 8
DEPTH = 2
ALPHA = (2 * DEPTH) ** 0.25
LN_EPS = 1e-5

WKV_CHUNK = 64
LANES = 128
VMEM_LIMIT_BYTES = 56 * 1024 * 1024


def _params(*sem):
    return pltpu.CompilerParams(dimension_semantics=sem, vmem_limit_bytes=VMEM_LIMIT_BYTES)


def _pick(n, candidates):
    for c in candidates:
        if n % c == 0:
            return c
    raise ValueError(f"no tile in {candidates} divides {n}")


def _pad_row_mask(first_row, rows, lp, nbatch):
    r = first_row + lax.broadcasted_iota(jnp.int32, (rows, 1), 0)
    mask = r < PAD
    for bi in range(1, nbatch):
        mask = jnp.logical_or(mask, jnp.logical_and(r >= bi * lp, r < bi * lp + PAD))
    return mask


def _layer_norm(z, g, b):
    mu = jnp.mean(z, axis=-1, keepdims=True)
    zc = z - mu
    var = jnp.mean(zc * zc, axis=-1, keepdims=True)
    return zc * lax.rsqrt(var + LN_EPS) * g + b


def _dot(a, b):
    return jnp.dot(a.astype(BF16), b.astype(BF16), preferred_element_type=F32)


def _dot_nt(a, b):
    return lax.dot_general(a.astype(BF16), b.astype(BF16), (((1,), (1,)), ((), ())),
                           preferred_element_type=F32)


def _qkv_kernel(x_ref, w_ref, b_ref, s_ref, o_ref):
    acc = _dot(x_ref[...], w_ref[...])
    o_ref[...] = ((acc + b_ref[...]) * s_ref[...]).astype(o_ref.dtype)


def _qkv_proj(x, w, b, s):
    m, d = x.shape
    n = w.shape[1]
    tm = _pick(m, (768, 384, 256, 128))
    return pl.pallas_call(
        _qkv_kernel,
        grid=(m // tm,),
        in_specs=[pl.BlockSpec((tm, d), lambda i: (i, 0)),
                  pl.BlockSpec((d, n), lambda i: (0, 0)),
                  pl.BlockSpec((1, n), lambda i: (0, 0)),
                  pl.BlockSpec((1, n), lambda i: (0, 0))],
        out_specs=pl.BlockSpec((tm, n), lambda i: (i, 0)),
        out_shape=jax.ShapeDtypeStruct((m, n), BF16),
        compiler_params=_params("parallel"),
        name="qkv_proj",
    )(x, w, b, s)


def _attn_kernel(sink_ref, mbias_ref, cur_ref, prev_ref, meta_ref, bias_ref, o_ref):
    n = pl.program_id(1)
    q_dim = N_HEADS * HEAD_DIM
    kv_dim = N_KV_HEADS * HEAD_DIM
    cur = cur_ref[0]
    prev = prev_ref[0]
    meta = meta_ref[0][PAD:, :]

    col = lax.broadcasted_iota(jnp.int32, (1, 2 * BLOCK), 1)
    key_ok = (col + (n - 1) * BLOCK) >= PAD
    qrow = jnp.bitwise_and(lax.broadcasted_iota(jnp.int32, (GROUP * BLOCK, N_META), 0), BLOCK - 1)
    midx = lax.broadcasted_iota(jnp.int32, (GROUP * BLOCK, N_META), 1)
    meta_ok = (n * BLOCK + qrow - PAD - midx) >= WINDOW
    grp = jnp.right_shift(lax.broadcasted_iota(jnp.int32, (GROUP * BLOCK, 1), 0), int(math.log2(BLOCK)))

    for h in range(N_KV_HEADS):
        ks = slice(h * HEAD_DIM, (h + 1) * HEAD_DIM)
        vs = slice(kv_dim + h * HEAD_DIM, kv_dim + (h + 1) * HEAD_DIM)
        kb = jnp.concatenate([prev[:, ks], cur[:, q_dim + h * HEAD_DIM:q_dim + (h + 1) * HEAD_DIM]], axis=0)
        vb = jnp.concatenate([prev[:, vs], cur[:, q_dim + kv_dim + h * HEAD_DIM:
                                               q_dim + kv_dim + (h + 1) * HEAD_DIM]], axis=0)
        km = meta[:, ks]
        vm = meta[:, vs]
        q4 = jnp.concatenate(
            [cur[:, (h * GROUP + g) * HEAD_DIM:(h * GROUP + g + 1) * HEAD_DIM] for g in range(GROUP)], axis=0)

        sink_col = jnp.zeros((GROUP * BLOCK, 1), F32)
        mb_col = jnp.zeros((GROUP * BLOCK, 1), F32)
        for g in range(GROUP):
            sink_col = jnp.where(grp == g, sink_ref[h * GROUP + g], sink_col)
            mb_col = jnp.where(grp == g, mbias_ref[h * GROUP + g], mb_col)

        bias4 = bias_ref[h * GROUP:(h + 1) * GROUP].reshape(GROUP * BLOCK, 2 * BLOCK)
        s = jnp.where(key_ok, _dot_nt(q4, kb) + bias4, NEG_INF)
        sm = jnp.where(meta_ok, _dot_nt(q4, km) + mb_col, NEG_INF)
        mx = jnp.maximum(jnp.maximum(jnp.max(s, axis=-1, keepdims=True),
                                     jnp.max(sm, axis=-1, keepdims=True)), sink_col)
        p = jnp.exp(s - mx)
        pm = jnp.exp(sm - mx)
        denom = (jnp.sum(p, axis=-1, keepdims=True) + jnp.sum(pm, axis=-1, keepdims=True)
                 + jnp.exp(sink_col - mx))
        o4 = (_dot(p, vb) + _dot(pm, vm)) / denom
        for g in range(GROUP):
            hd = h * GROUP + g
            o_ref[0, :, hd * HEAD_DIM:(hd + 1) * HEAD_DIM] = o4[g * BLOCK:(g + 1) * BLOCK].astype(o_ref.dtype)


def _attention(qkv, bias_tbl, sinks, meta_bias):
    b, lp, e = qkv.shape
    nb = lp // BLOCK
    q_dim = N_HEADS * HEAD_DIM
    kv2 = 2 * N_KV_HEADS * HEAD_DIM
    kv_blk = q_dim // kv2
    smem = pl.BlockSpec(memory_space=pltpu.SMEM)
    return pl.pallas_call(
        _attn_kernel,
        grid=(b, nb),
        in_specs=[smem, smem,
                  pl.BlockSpec((1, BLOCK, e), lambda i, j: (i, j, 0)),
                  pl.BlockSpec((1, BLOCK, kv2), lambda i, j: (i, jnp.maximum(j - 1, 0), kv_blk)),
                  pl.BlockSpec((1, BLOCK, kv2), lambda i, j: (i, 0, kv_blk)),
                  pl.BlockSpec((N_HEADS, BLOCK, 2 * BLOCK), lambda i, j: (0, 0, 0))],
        out_specs=pl.BlockSpec((1, BLOCK, q_dim), lambda i, j: (i, j, 0)),
        out_shape=jax.ShapeDtypeStruct((b, lp, q_dim), BF16),
        compiler_params=_params("parallel", "parallel"),
        name="swa_attention",
    )(sinks, meta_bias, qkv, qkv, qkv, bias_tbl)


def _proj_ln_kernel(*refs, has_mult):
    if has_mult:
        a_ref, m_ref, w_ref, res_ref, g_ref, b_ref, o_ref = refs
        a = a_ref[...].astype(F32) * m_ref[...].astype(F32)
    else:
        a_ref, w_ref, res_ref, g_ref, b_ref, o_ref = refs
        a = a_ref[...]
    y = _dot(a, w_ref[...])
    o_ref[...] = _layer_norm(ALPHA * res_ref[...] + y, g_ref[...], b_ref[...])


def _proj_res_ln(a, w, res, g, b, mult=None):
    m, k = a.shape
    d = w.shape[1]
    tm = _pick(m, (768, 384, 256, 128))
    row = lambda i: (i, 0)
    fixed = lambda i: (0, 0)
    ins = [a] + ([mult] if mult is not None else []) + [w, res, g, b]
    specs = ([pl.BlockSpec((tm, k), row)] + ([pl.BlockSpec((tm, k), row)] if mult is not None else [])
             + [pl.BlockSpec((k, d), fixed), pl.BlockSpec((tm, d), row),
                pl.BlockSpec((1, d), fixed), pl.BlockSpec((1, d), fixed)])
    return pl.pallas_call(
        functools.partial(_proj_ln_kernel, has_mult=mult is not None),
        grid=(m // tm,),
        in_specs=specs,
        out_specs=pl.BlockSpec((tm, d), row),
        out_shape=jax.ShapeDtypeStruct((m, d), F32),
        compiler_params=_params("parallel"),
        name="proj_res_ln",
    )(*ins)


def _silu_mul(gate, up):
    return gate * (1.0 / (1.0 + jnp.exp(-gate))) * up


def _ffn_kernel(x_ref, wg_ref, wu_ref, wd_ref, g_ref, b_ref, o_ref, acc_ref, xb_ref, *, lp, nbatch, tm):
    i = pl.program_id(0)
    f = pl.program_id(1)

    @pl.when(f == 0)
    def _():
        acc_ref[...] = jnp.zeros_like(acc_ref)
        xb_ref[...] = x_ref[...].astype(BF16)

    xb = xb_ref[...]
    gate = jnp.dot(xb, wg_ref[...], preferred_element_type=F32)
    up = jnp.dot(xb, wu_ref[...], preferred_element_type=F32)
    acc_ref[...] += jnp.dot(_silu_mul(gate, up).astype(BF16), wd_ref[...], preferred_element_type=F32)

    @pl.when(f == pl.num_programs(1) - 1)
    def _():
        y = _layer_norm(ALPHA * x_ref[...] + acc_ref[...], g_ref[...], b_ref[...])
        o_ref[...] = jnp.where(_pad_row_mask(i * tm, tm, lp, nbatch), 0.0, y)


def _ffn(x, w_gu, w_down, g, b, lp, nbatch):
    m, d = x.shape
    ff = w_down.shape[0]
    tm = _pick(m, (384, 256, 128))
    tf = _pick(ff, (1408, 896, 512, 256, 128))
    nf = ff // tf
    row = lambda i, f: (i, 0)
    fixed = lambda i, f: (0, 0)
    return pl.pallas_call(
        functools.partial(_ffn_kernel, lp=lp, nbatch=nbatch, tm=tm),
        grid=(m // tm, nf),
        in_specs=[pl.BlockSpec((tm, d), row),
                  pl.BlockSpec((d, tf), lambda i, f: (0, f)),
                  pl.BlockSpec((d, tf), lambda i, f: (0, f + nf)),
                  pl.BlockSpec((tf, d), lambda i, f: (f, 0)),
                  pl.BlockSpec((1, d), fixed), pl.BlockSpec((1, d), fixed)],
        out_specs=pl.BlockSpec((tm, d), row),
        out_shape=jax.ShapeDtypeStruct((m, d), F32),
        scratch_shapes=[pltpu.VMEM((tm, d), F32), pltpu.VMEM((tm, d), BF16)],
        compiler_params=_params("parallel", "arbitrary"),
        name="swiglu_res_ln",
    )(x, w_gu, w_gu, w_down, g, b)


def _moe_kernel(tile_e_ref, nact_ref, src_ref, src_next_ref, dst_ref, x_hbm, wg_ref, wu_ref, wd_ref,
                y_hbm, xbuf, xb_ref, acc_ref, ybuf, gsem, ssem, *, tm):
    j = pl.program_id(0)
    f = pl.program_id(1)
    nf = pl.num_programs(1)
    nact = nact_ref[0]
    active = j < nact
    slot = lax.rem(j, 2)

    def gather_start(idx_ref, s):
        def body(i, carry):
            t = idx_ref[0, 0, i]
            pltpu.make_async_copy(x_hbm.at[pl.ds(t, 1)], xbuf.at[s, pl.ds(i, 1)], gsem.at[s]).start()
            return carry
        lax.fori_loop(0, tm, body, 0)

    def gather_wait(s):
        pltpu.make_async_copy(x_hbm.at[pl.ds(0, tm)], xbuf.at[s], gsem.at[s]).wait()

    def scatter_start(s):
        def body(i, carry):
            t = dst_ref[0, 0, i]
            pltpu.make_async_copy(ybuf.at[s, pl.ds(i, 1)], y_hbm.at[pl.ds(t, 1)], ssem.at[s]).start()
            return carry
        lax.fori_loop(0, tm, body, 0)

    def scatter_wait(s):
        pltpu.make_async_copy(ybuf.at[s], y_hbm.at[pl.ds(0, tm)], ssem.at[s]).wait()

    @pl.when(jnp.logical_and(active, f == 0))
    def _():
        @pl.when(j == 0)
        def _():
            gather_start(src_ref, 0)

        gather_wait(slot)

        @pl.when(j + 1 < nact)
        def _():
            gather_start(src_next_ref, 1 - slot)

        xb_ref[...] = xbuf[slot].astype(BF16)
        acc_ref[...] = jnp.zeros_like(acc_ref)

    @pl.when(active)
    def _():
        xb = xb_ref[...]
        gate = jnp.dot(xb, wg_ref[0], preferred_element_type=F32)
        up = jnp.dot(xb, wu_ref[0], preferred_element_type=F32)
        acc_ref[...] += jnp.dot(_silu_mul(gate, up).astype(BF16), wd_ref[0], preferred_element_type=F32)

    @pl.when(f == nf - 1)
    def _():
        @pl.when(active)
        def _():
            ybuf[slot] = acc_ref[...]

        @pl.when(jnp.logical_not(active))
        def _():
            ybuf[slot] = jnp.zeros(ybuf.shape[1:], ybuf.dtype)

        scatter_start(slot)

        @pl.when(j >= 1)
        def _():
            scatter_wait(1 - slot)

        @pl.when(j == pl.num_programs(0) - 1)
        def _():
            scatter_wait(slot)


def _moe_experts(x, expert_idx, w_gu, w_down):
    m, d = x.shape
    ne, ff = w_down.shape[0], w_down.shape[1]
    nslots = 2 * m
    tm = _pick(nslots, (512, 256, 128))
    tf = _pick(ff, (896, 512, 256, 128))
    nf = ff // tf
    n_tiles = nslots // tm + ne
    p_rows = n_tiles * tm

    e_flat = expert_idx.reshape(nslots)
    onehot = (e_flat[:, None] == jnp.arange(ne, dtype=jnp.int32)[None, :]).astype(jnp.int32)
    csum = jnp.cumsum(onehot, axis=0)
    rank = jnp.take_along_axis(csum, e_flat[:, None], axis=1)[:, 0] - 1
    counts = csum[-1]
    gsize = ((counts + tm - 1) // tm) * tm
    gend = jnp.cumsum(gsize)
    dest = (gend - gsize)[e_flat] + rank
    slot_ids = jnp.arange(nslots, dtype=jnp.int32)
    dst_sorted = jnp.full((p_rows,), -1, jnp.int32).at[dest].set(slot_ids)
    is_fill = dst_sorted < 0
    fill_rank = jnp.cumsum(is_fill.astype(jnp.int32)) - 1
    src_sorted = jnp.where(is_fill, 0, dst_sorted // 2)
    dst_sorted = jnp.where(is_fill, nslots + fill_rank, dst_sorted)
    tile_e = jnp.minimum(jnp.searchsorted(gend, jnp.arange(n_tiles, dtype=jnp.int32) * tm, side="right"),
                         ne - 1).astype(jnp.int32)
    nact = (gend[-1:] // tm).astype(jnp.int32)
    src3 = src_sorted.reshape(n_tiles, 1, tm)
    dst3 = dst_sorted.reshape(n_tiles, 1, tm)

    def f_eff(j, f, nact_ref):
        return jnp.where(j < nact_ref[0], f, nf - 1)

    smem_blk = lambda imap: pl.BlockSpec((1, 1, tm), imap, memory_space=pltpu.SMEM)
    grid_spec = pltpu.PrefetchScalarGridSpec(
        num_scalar_prefetch=2,
        grid=(n_tiles, nf),
        in_specs=[smem_blk(lambda j, f, te, na: (j, 0, 0)),
                  smem_blk(lambda j, f, te, na: (jnp.minimum(j + 1, n_tiles - 1), 0, 0)),
                  smem_blk(lambda j, f, te, na: (j, 0, 0)),
                  pl.BlockSpec(memory_space=pl.ANY),
                  pl.BlockSpec((1, d, tf), lambda j, f, te, na: (te[j], 0, f_eff(j, f, na))),
                  pl.BlockSpec((1, d, tf), lambda j, f, te, na: (te[j], 0, f_eff(j, f, na) + nf)),
                  pl.BlockSpec((1, tf, d), lambda j, f, te, na: (te[j], f_eff(j, f, na), 0))],
        out_specs=pl.BlockSpec(memory_space=pl.ANY),
        scratch_shapes=[pltpu.VMEM((2, tm, d), F32), pltpu.VMEM((tm, d), BF16), pltpu.VMEM((tm, d), F32),
                        pltpu.VMEM((2, tm, d), F32), pltpu.SemaphoreType.DMA((2,)),
                        pltpu.SemaphoreType.DMA((2,))],
    )
    return pl.pallas_call(
        functools.partial(_moe_kernel, tm=tm),
        grid_spec=grid_spec,
        out_shape=jax.ShapeDtypeStruct((p_rows, d), F32),
        compiler_params=_params("arbitrary", "arbitrary"),
        name="moe_grouped_swiglu",
    )(tile_e, nact, src3, src3, dst3, x, w_gu, w_gu, w_down)


def _combine_kernel(x_ref, y_ref, ro_ref, g_ref, b_ref, o_ref):
    d = x_ref.shape[1]
    ro = ro_ref[...]
    lane = lax.broadcasted_iota(jnp.int32, ro.shape, 1)
    w1 = jnp.sum(jnp.where(lane == 2, ro, 0.0), axis=-1, keepdims=True)
    w2 = jnp.sum(jnp.where(lane == 3, ro, 0.0), axis=-1, keepdims=True)
    z = ALPHA * x_ref[...] + w1 * y_ref[:, :d] + w2 * y_ref[:, d:]
    o_ref[...] = _layer_norm(z, g_ref[...], b_ref[...])


def _moe_combine(x, y_slots, router_out, g, b):
    m, d = x.shape
    y2 = y_slots.reshape(y_slots.shape[0] // 2, 2 * d)
    tm = _pick(m, (768, 384, 256, 128))
    row = lambda i: (i, 0)
    fixed = lambda i: (0, 0)
    return pl.pallas_call(
        _combine_kernel,
        grid=(m // tm,),
        in_specs=[pl.BlockSpec((tm, d), row), pl.BlockSpec((tm, 2 * d), row), pl.BlockSpec((tm, LANES), row),
                  pl.BlockSpec((1, d), fixed), pl.BlockSpec((1, d), fixed)],
        out_specs=pl.BlockSpec((tm, d), row),
        out_shape=jax.ShapeDtypeStruct((m, d), F32),
        compiler_params=_params("parallel"),
        name="moe_combine_ln",
    )(x, y2, router_out, g, b)


def _rwkv_proj_kernel(x_ref, xp_ref, mu_ref, wr_ref, wk_ref, wv_ref, w0_ref, w1_ref, w2_ref,
                      a0_ref, a1_ref, a2_ref, g1_ref, g2_ref,
                      r_ref, k_ref, v_ref, a_ref, ld_ref, g_ref, *, lp, nbatch):
    x = x_ref[...]
    tm = x.shape[0]
    rolled = pltpu.roll(x, 1, axis=0)
    prev_row = xp_ref[7:8, :]
    first = lax.broadcasted_iota(jnp.int32, (tm, 1), 0) == 0
    xx = jnp.where(first, prev_row, rolled) - x
    xx = jnp.where(_pad_row_mask(pl.program_id(0) * tm, tm, lp, nbatch), 0.0, xx)

    def mix(i):
        return x + xx * mu_ref[i:i + 1, :]

    r_ref[...] = _dot(mix(0), wr_ref[...])
    k_ref[...] = _dot(mix(2), wk_ref[...])
    v_ref[...] = _dot(mix(3), wv_ref[...])
    wl = w0_ref[...] + _dot(jnp.tanh(_dot(mix(1), w1_ref[...])), w2_ref[...])
    z = -wl
    softplus = jnp.maximum(z, 0.0) + jnp.log(1.0 + jnp.exp(-jnp.abs(z)))
    ld_ref[...] = -jnp.exp(-softplus - 0.5)
    al = a0_ref[...] + _dot(_dot(mix(4), a1_ref[...]), a2_ref[...])
    a_ref[...] = 1.0 / (1.0 + jnp.exp(-al))
    gl = _dot(mix(5), g1_ref[...])
    g_ref[...] = _dot(1.0 / (1.0 + jnp.exp(-gl)), g2_ref[...])


def _rwkv_proj(x, mu, w_rkv, w0, w1, w2, a0, a1, a2, g1, g2, lp, nbatch):
    m, d = x.shape
    tm = _pick(m, (384, 256, 128))
    row = lambda i: (i, 0)
    fixed = lambda i: (0, 0)
    full = lambda arr: pl.BlockSpec(arr.shape, fixed)
    out = jax.ShapeDtypeStruct((m, d), F32)
    return pl.pallas_call(
        functools.partial(_rwkv_proj_kernel, lp=lp, nbatch=nbatch),
        grid=(m // tm,),
        in_specs=[pl.BlockSpec((tm, d), row),
                  pl.BlockSpec((8, d), lambda i: (jnp.maximum(i * (tm // 8) - 1, 0), 0)),
                  full(mu), full(w_rkv[0]), full(w_rkv[1]), full(w_rkv[2]),
                  full(w0), full(w1), full(w2), full(a0), full(a1), full(a2), full(g1), full(g2)],
        out_specs=[pl.BlockSpec((tm, d), row)] * 6,
        out_shape=[out] * 6,
        compiler_params=_params("parallel"),
        name="rwkv_proj",
    )(x, x, mu, w_rkv[0], w_rkv[1], w_rkv[2], w0, w1, w2, a0, a1, a2, g1, g2)


def _bmm(a, b):
    return jnp.einsum("cij,cjk->cik", a.astype(BF16), b.astype(BF16), preferred_element_type=F32)


def _bmm_nt(a, b):
    return jnp.einsum("cik,cjk->cij", a.astype(BF16), b.astype(BF16), preferred_element_type=F32)


def _bmm_tn(a, b):
    return jnp.einsum("cki,ckj->cij", a.astype(BF16), b.astype(BF16), preferred_element_type=F32)


def _wkv_kernel(r_ref, k_ref, v_ref, a_ref, ld_ref, kk_ref, ka_ref, rk_ref, gg_ref, gb_ref,
                y_ref, z_ref):
    c = WKV_CHUNK
    n = RWKV_HEAD
    tb = r_ref.shape[1]
    nc = tb // c

    @pl.when(pl.program_id(2) == 0)
    def _():
        z_ref[...] = jnp.zeros_like(z_ref)

    ti = lax.broadcasted_iota(jnp.int32, (1, c, c), 1)
    si = lax.broadcasted_iota(jnp.int32, (1, c, c), 2)
    strict = ti > si
    incl = ti >= si
    eye = ti == si
    tri = jnp.broadcast_to(jnp.where(incl, 1.0, 0.0).astype(BF16), (nc, c, c))

    for hh in range(LANES // n):
        sl = slice(hh * n, (hh + 1) * n)
        r = r_ref[0, :, sl].reshape(nc, c, n)
        k = k_ref[0, :, sl].reshape(nc, c, n)
        v = v_ref[0, :, sl].reshape(nc, c, n)
        a = a_ref[0, :, sl].reshape(nc, c, n)
        ld = ld_ref[0, :, sl].reshape(nc, c, n)
        k_k = kk_ref[:, sl].reshape(1, 1, n)
        k_a = ka_ref[:, sl].reshape(1, 1, n)
        r_k = rk_ref[:, sl].reshape(1, 1, n)

        kkr = k * k_k
        kk = kkr / jnp.maximum(jnp.sqrt(jnp.sum(kkr * kkr, axis=-1, keepdims=True)), 1e-12)
        km = k * (1.0 + (a - 1.0) * k_a)
        bv = kk * a

        ld_hi = ld.astype(BF16)
        ld_r1 = ld - ld_hi.astype(F32)
        ld_mid = ld_r1.astype(BF16)
        ld_lo = (ld_r1 - ld_mid.astype(F32)).astype(BF16)
        cs = (jnp.einsum("cij,cjk->cik", tri, ld_hi, preferred_element_type=F32)
              + jnp.einsum("cij,cjk->cik", tri, ld_mid, preferred_element_type=F32)
              + jnp.einsum("cij,cjk->cik", tri, ld_lo, preferred_element_type=F32))
        cs_last = cs[:, c - 1:c, :]
        e_neg = jnp.exp(-cs)
        e_tail = jnp.exp(cs_last - cs)
        at = -kk * jnp.exp(cs - ld)
        rt = r * jnp.exp(cs)
        bt = bv * e_neg
        kt = km * e_neg
        bh = bv * e_tail
        kh = km * e_tail
        dec = jnp.exp(cs_last)

        x2 = jnp.concatenate([at, rt], axis=1)
        gb_ = _bmm_nt(x2, bt)
        gk_ = _bmm_nt(x2, kt)
        lab = jnp.where(strict, gb_[:, :c], 0.0)
        mrb = jnp.where(incl, gb_[:, c:], 0.0)
        lak = jnp.where(strict, gk_[:, :c], 0.0)
        mrk = jnp.where(incl, gk_[:, c:], 0.0)

        t_inv = jnp.where(eye, 1.0, lab)
        lp = lab
        steps = int(math.log2(c)) - 1
        for _ in range(steps):
            lp = _bmm(lp, lp)
            t_inv = t_inv + _bmm(t_inv, lp)

        w = _bmm(t_inv, at)
        u0 = _bmm(t_inv, _bmm(lak, v))
        q = rt + _bmm(mrb, w)
        yp = _bmm(mrb, u0) + _bmm(mrk, v)
        a_t = jnp.where(eye, dec, 0.0) + _bmm_tn(bh, w)
        g_t = _bmm_tn(bh, u0) + _bmm_tn(kh, v)

        z = z_ref[hh]
        ys = []
        for ci in range(nc):
            ys.append(yp[ci] + _dot(q[ci], z))
            z = _dot(a_t[ci], z) + g_t[ci]
        z_ref[hh] = z
        y = jnp.concatenate(ys, axis=0)

        mu = jnp.mean(y, axis=-1, keepdims=True)
        yc = y - mu
        var = jnp.mean(yc * yc, axis=-1, keepdims=True)
        yn = yc * lax.rsqrt(var + GN_EPS) * gg_ref[:, sl] + gb_ref[:, sl]
        rf = r.reshape(tb, n)
        kf = km.reshape(tb, n)
        vf = v.reshape(tb, n)
        bonus = jnp.sum(rf * kf * rk_ref[:, sl], axis=-1, keepdims=True) * vf
        y_ref[0, :, sl] = yn + bonus


def _wkv(r, k, v, a, ld, k_k, k_a, r_k, gn_g, gn_b):
    b, lp, d = r.shape
    tb = _pick(lp, (384, 256, 128))
    tok = lambda i, j, t: (i, t, j)
    par = lambda i, j, t: (0, j)
    tok_spec = pl.BlockSpec((1, tb, LANES), tok)
    par_spec = pl.BlockSpec((1, LANES), par)
    return pl.pallas_call(
        _wkv_kernel,
        grid=(b, d // LANES, lp // tb),
        in_specs=[tok_spec] * 5 + [par_spec] * 5,
        out_specs=tok_spec,
        out_shape=jax.ShapeDtypeStruct((b, lp, d), F32),
        scratch_shapes=[pltpu.VMEM((LANES // RWKV_HEAD, RWKV_HEAD, RWKV_HEAD), F32)],
        compiler_params=_params("parallel", "parallel", "arbitrary"),
        name="wkv7_chunked",
    )(r, k, v, a, ld, k_k, k_a, r_k, gn_g, gn_b)


def _router_kernel(x_ref, w_ref, o_ref):
    logits = jnp.dot(x_ref[...], w_ref[...], preferred_element_type=F32, precision=lax.Precision.HIGHEST)
    lane = lax.broadcasted_iota(jnp.int32, logits.shape, 1).astype(F32)
    neg = -jnp.inf
    lg = jnp.where(lane < N_EXPERTS, logits, neg)
    m1 = jnp.max(lg, axis=-1, keepdims=True)
    i1 = jnp.min(jnp.where(lg == m1, lane, float(LANES)), axis=-1, keepdims=True)
    lg2 = jnp.where(lane == i1, neg, lg)
    m2 = jnp.max(lg2, axis=-1, keepdims=True)
    i2 = jnp.min(jnp.where(lg2 == m2, lane, float(LANES)), axis=-1, keepdims=True)
    e2 = jnp.exp(m2 - m1)
    den = 1.0 + e2
    o_ref[...] = jnp.where(lane == 0.0, i1, jnp.where(lane == 1.0, i2, jnp.where(
        lane == 2.0, 1.0 / den, jnp.where(lane == 3.0, e2 / den, 0.0))))


def _router(x, w_pad):
    m, d = x.shape
    tm = _pick(m, (768, 384, 256, 128))
    return pl.pallas_call(
        _router_kernel,
        grid=(m // tm,),
        in_specs=[pl.BlockSpec((tm, d), lambda i: (i, 0)), pl.BlockSpec((d, LANES), lambda i: (0, 0))],
        out_specs=pl.BlockSpec((tm, LANES), lambda i: (i, 0)),
        out_shape=jax.ShapeDtypeStruct((m, LANES), F32),
        compiler_params=_params("parallel"),
        name="moe_router",
    )(x, w_pad)


def _t5_bucket(dist):
    n = jnp.maximum(dist, 0)
    is_small = n < MAX_EXACT
    nf = jnp.maximum(n, 1).astype(F32)
    large = MAX_EXACT + (jnp.log(nf / MAX_EXACT) / math.log(MAX_DISTANCE / MAX_EXACT)
                         * (NUM_BUCKETS - MAX_EXACT)).astype(jnp.int32)
    large = jnp.minimum(large, NUM_BUCKETS - 1)
    return jnp.where(is_small, n, large)


def _bias_tables(rel_bias):
    rb = rel_bias.astype(F32)
    d_band = BLOCK + jnp.arange(BLOCK)[:, None] - jnp.arange(2 * BLOCK)[None, :]
    ok = (d_band >= 0) & (d_band < WINDOW)
    band = jnp.where(ok[..., None], rb[_t5_bucket(d_band)], NEG_INF).transpose(2, 0, 1)
    meta = rb[_t5_bucket(jnp.array(WINDOW))]
    return band, meta


def kernel(x, meta_tokens, rel_bias, ln_mix_g, ln_mix_b, ln_ffn_g, ln_ffn_b, attn_w_qkv, attn_b_qkv, attn_sinks, attn_w_o, rwkv_mu, rwkv_w0, rwkv_w1, rwkv_w2, rwkv_a0, rwkv_a1, rwkv_a2, rwkv_g1, rwkv_g2, rwkv_k_k, rwkv_k_a, rwkv_r_k, rwkv_w_rkv, rwkv_lnx_g, rwkv_lnx_b, rwkv_w_o, ffn_w_gu, ffn_w_down, moe_router, moe_w_gu, moe_w_down):
    b, seq, d = x.shape
    lp = seq + BLOCK
    m = b * lp
    row = lambda t: t.reshape(1, -1).astype(F32)

    meta = jnp.broadcast_to(meta_tokens.astype(x.dtype)[None], (b, N_META, d))
    h = jnp.concatenate([jnp.zeros((b, PAD, d), x.dtype), meta, x], axis=1).reshape(m, d)

    q_dim = N_HEADS * HEAD_DIM
    e_dim = attn_w_qkv.shape[2]
    col_scale = jnp.where(jnp.arange(e_dim) < q_dim, ATTN_SCALE, 1.0).astype(F32)
    qkv = _qkv_proj(h, attn_w_qkv[0].astype(BF16), row(attn_b_qkv[0]), row(col_scale))
    bias_band, bias_meta = _bias_tables(rel_bias)
    o = _attention(qkv.reshape(b, lp, e_dim), bias_band, attn_sinks[0].astype(F32), bias_meta)
    h = _proj_res_ln(o.reshape(m, q_dim), attn_w_o[0].astype(BF16), h, row(ln_mix_g[0]), row(ln_mix_b[0]))
    h = _ffn(h, ffn_w_gu[0].astype(BF16), ffn_w_down[0].astype(BF16), row(ln_ffn_g[0]), row(ln_ffn_b[0]), lp, b)

    r, k, v, a, ld, g = _rwkv_proj(
        h, rwkv_mu[0], rwkv_w_rkv[0].astype(BF16), row(rwkv_w0[0]), rwkv_w1[0].astype(BF16),
        rwkv_w2[0].astype(BF16), row(rwkv_a0[0]), rwkv_a1[0].astype(BF16), rwkv_a2[0].astype(BF16),
        rwkv_g1[0].astype(BF16), rwkv_g2[0].astype(BF16), lp, b)
    t3 = lambda t: t.reshape(b, lp, d)
    y = _wkv(t3(r), t3(k), t3(v), t3(a), t3(ld), row(rwkv_k_k[0]), row(rwkv_k_a[0]), row(rwkv_r_k[0]),
             row(rwkv_lnx_g[0]), row(rwkv_lnx_b[0]))
    h = _proj_res_ln(y.reshape(m, d), rwkv_w_o[0].astype(BF16), h, row(ln_mix_g[1]), row(ln_mix_b[1]), mult=g)
    w_router = jnp.pad(moe_router[0].astype(F32), ((0, 0), (0, LANES - N_EXPERTS)))
    routed = _router(h, w_router)
    expert_idx = routed[:, :2].astype(jnp.int32)
    y_slots = _moe_experts(h, expert_idx, moe_w_gu[0].astype(BF16), moe_w_down[0].astype(BF16))
    h = _moe_combine(h, y_slots, routed, row(ln_ffn_g[1]), row(ln_ffn_b[1]))
    return h.reshape(b, lp, d)[:, BLOCK:]
```

```python
import functools
import math

import jax
import jax.numpy as jnp
from jax import lax
from jax.experimental import pallas as pl
from jax.experimental.pallas import tpu as pltpu

F32 = jnp.float32
BF16 = jnp.bfloat16

N_META = 16
N_HEADS = 16
N_KV_HEADS = 4
HEAD_DIM = 64
GROUP = N_HEADS // N_KV_HEADS
WINDOW = 128
BLOCK = 128
PAD = BLOCK - N_META
ATTN_SCALE = 1.0 / math.sqrt(HEAD_DIM)
NEG_INF = -1e30
NUM_BUCKETS = 32
MAX_EXACT = NUM_BUCKETS // 2
MAX_DISTANCE = 128
RWKV_HEAD = 64
GN_EPS = 64e-5
N_EXPERTS = 8
DEPTH = 2
ALPHA = (2 * DEPTH) ** 0.25
LN_EPS = 1e-5

WKV_CHUNK = 64
LANES = 128
DMA_ISSUE_UNROLL = 8
VMEM_LIMIT_BYTES = 56 * 1024 * 1024


def _params(*sem):
    return pltpu.CompilerParams(dimension_semantics=sem, vmem_limit_bytes=VMEM_LIMIT_BYTES)


def _pick(n, candidates):
    for c in candidates:
        if n % c == 0:
            return c
    raise ValueError(f"no tile in {candidates} divides {n}")


def _pad_row_mask(first_row, rows, lp, nbatch):
    r = first_row + lax.broadcasted_iota(jnp.int32, (rows, 1), 0)
    mask = r < PAD
    for bi in range(1, nbatch):
        mask = jnp.logical_or(mask, jnp.logical_and(r >= bi * lp, r < bi * lp + PAD))
    return mask


def _layer_norm(z, g, b):
    mu = jnp.mean(z, axis=-1, keepdims=True)
    zc = z - mu
    var = jnp.mean(zc * zc, axis=-1, keepdims=True)
    return zc * lax.rsqrt(var + LN_EPS) * g + b


def _dot(a, b):
    return jnp.dot(a.astype(BF16), b.astype(BF16), preferred_element_type=F32)


def _dot_nt(a, b):
    return lax.dot_general(a.astype(BF16), b.astype(BF16), (((1,), (1,)), ((), ())),
                           preferred_element_type=F32)


def _qkv_kernel(x_ref, w_ref, b_ref, s_ref, o_ref):
    acc = _dot(x_ref[...], w_ref[...])
    o_ref[...] = ((acc + b_ref[...]) * s_ref[...]).astype(o_ref.dtype)


def _qkv_proj(x, w, b, s):
    m, d = x.shape
    n = w.shape[1]
    tm = _pick(m, (768, 384, 256, 128))
    return pl.pallas_call(
        _qkv_kernel,
        grid=(m // tm,),
        in_specs=[pl.BlockSpec((tm, d), lambda i: (i, 0)),
                  pl.BlockSpec((d, n), lambda i: (0, 0)),
                  pl.BlockSpec((1, n), lambda i: (0, 0)),
                  pl.BlockSpec((1, n), lambda i: (0, 0))],
        out_specs=pl.BlockSpec((tm, n), lambda i: (i, 0)),
        out_shape=jax.ShapeDtypeStruct((m, n), BF16),
        compiler_params=_params("parallel"),
        name="qkv_proj",
    )(x, w, b, s)


def _attn_kernel(sink_ref, mbias_ref, cur_ref, prev_ref, meta_ref, bias_ref, o_ref):
    n = pl.program_id(1)
    q_dim = N_HEADS * HEAD_DIM
    kv_dim = N_KV_HEADS * HEAD_DIM
    cur = cur_ref[0]
    prev = prev_ref[0]
    meta = meta_ref[0][PAD:, :]

    col = lax.broadcasted_iota(jnp.int32, (1, 2 * BLOCK), 1)
    key_ok = (col + (n - 1) * BLOCK) >= PAD
    qrow = jnp.bitwise_and(lax.broadcasted_iota(jnp.int32, (GROUP * BLOCK, N_META), 0), BLOCK - 1)
    midx = lax.broadcasted_iota(jnp.int32, (GROUP * BLOCK, N_META), 1)
    meta_ok = (n * BLOCK + qrow - PAD - midx) >= WINDOW
    grp = jnp.right_shift(lax.broadcasted_iota(jnp.int32, (GROUP * BLOCK, 1), 0), int(math.log2(BLOCK)))

    for h in range(N_KV_HEADS):
        ks = slice(h * HEAD_DIM, (h + 1) * HEAD_DIM)
        vs = slice(kv_dim + h * HEAD_DIM, kv_dim + (h + 1) * HEAD_DIM)
        kb = jnp.concatenate([prev[:, ks], cur[:, q_dim + h * HEAD_DIM:q_dim + (h + 1) * HEAD_DIM]], axis=0)
        vb = jnp.concatenate([prev[:, vs], cur[:, q_dim + kv_dim + h * HEAD_DIM:
                                               q_dim + kv_dim + (h + 1) * HEAD_DIM]], axis=0)
        km = meta[:, ks]
        vm = meta[:, vs]
        q4 = jnp.concatenate(
            [cur[:, (h * GROUP + g) * HEAD_DIM:(h * GROUP + g + 1) * HEAD_DIM] for g in range(GROUP)], axis=0)

        sink_col = jnp.zeros((GROUP * BLOCK, 1), F32)
        mb_col = jnp.zeros((GROUP * BLOCK, 1), F32)
        for g in range(GROUP):
            sink_col = jnp.where(grp == g, sink_ref[h * GROUP + g], sink_col)
            mb_col = jnp.where(grp == g, mbias_ref[h * GROUP + g], mb_col)

        bias4 = bias_ref[h * GROUP:(h + 1) * GROUP].reshape(GROUP * BLOCK, 2 * BLOCK)
        s = jnp.where(key_ok, _dot_nt(q4, kb) + bias4, NEG_INF)
        sm = jnp.where(meta_ok, _dot_nt(q4, km) + mb_col, NEG_INF)
        mx = jnp.maximum(jnp.maximum(jnp.max(s, axis=-1, keepdims=True),
                                     jnp.max(sm, axis=-1, keepdims=True)), sink_col)
        p = jnp.exp(s - mx)
        pm = jnp.exp(sm - mx)
        denom = (jnp.sum(p, axis=-1, keepdims=True) + jnp.sum(pm, axis=-1, keepdims=True)
                 + jnp.exp(sink_col - mx))
        o4 = (_dot(p, vb) + _dot(pm, vm)) / denom
        for g in range(GROUP):
            hd = h * GROUP + g
            o_ref[0, :, hd * HEAD_DIM:(hd + 1) * HEAD_DIM] = o4[g * BLOCK:(g + 1) * BLOCK].astype(o_ref.dtype)


def _attention(qkv, bias_tbl, sinks, meta_bias):
    b, lp, e = qkv.shape
    nb = lp // BLOCK
    q_dim = N_HEADS * HEAD_DIM
    kv2 = 2 * N_KV_HEADS * HEAD_DIM
    kv_blk = q_dim // kv2
    smem = pl.BlockSpec(memory_space=pltpu.SMEM)
    return pl.pallas_call(
        _attn_kernel,
        grid=(b, nb),
        in_specs=[smem, smem,
                  pl.BlockSpec((1, BLOCK, e), lambda i, j: (i, j, 0)),
                  pl.BlockSpec((1, BLOCK, kv2), lambda i, j: (i, jnp.maximum(j - 1, 0), kv_blk)),
                  pl.BlockSpec((1, BLOCK, kv2), lambda i, j: (i, 0, kv_blk)),
                  pl.BlockSpec((N_HEADS, BLOCK, 2 * BLOCK), lambda i, j: (0, 0, 0))],
        out_specs=pl.BlockSpec((1, BLOCK, q_dim), lambda i, j: (i, j, 0)),
        out_shape=jax.ShapeDtypeStruct((b, lp, q_dim), BF16),
        compiler_params=_params("parallel", "parallel"),
        name="swa_attention",
    )(sinks, meta_bias, qkv, qkv, qkv, bias_tbl)


def _proj_ln_kernel(*refs, has_mult):
    if has_mult:
        a_ref, m_ref, w_ref, res_ref, g_ref, b_ref, o_ref = refs
        a = a_ref[...].astype(F32) * m_ref[...].astype(F32)
    else:
        a_ref, w_ref, res_ref, g_ref, b_ref, o_ref = refs
        a = a_ref[...]
    y = _dot(a, w_ref[...])
    o_ref[...] = _layer_norm(ALPHA * res_ref[...] + y, g_ref[...], b_ref[...])


def _proj_res_ln(a, w, res, g, b, mult=None):
    m, k = a.shape
    d = w.shape[1]
    tm = _pick(m, (768, 384, 256, 128))
    row = lambda i: (i, 0)
    fixed = lambda i: (0, 0)
    ins = [a] + ([mult] if mult is not None else []) + [w, res, g, b]
    specs = ([pl.BlockSpec((tm, k), row)] + ([pl.BlockSpec((tm, k), row)] if mult is not None else [])
             + [pl.BlockSpec((k, d), fixed), pl.BlockSpec((tm, d), row),
                pl.BlockSpec((1, d), fixed), pl.BlockSpec((1, d), fixed)])
    return pl.pallas_call(
        functools.partial(_proj_ln_kernel, has_mult=mult is not None),
        grid=(m // tm,),
        in_specs=specs,
        out_specs=pl.BlockSpec((tm, d), row),
        out_shape=jax.ShapeDtypeStruct((m, d), F32),
        compiler_params=_params("parallel"),
        name="proj_res_ln",
    )(*ins)


def _silu_mul(gate, up):
    return gate * (1.0 / (1.0 + jnp.exp(-gate))) * up


def _ffn_kernel(x_ref, wg_ref, wu_ref, wd_ref, g_ref, b_ref, o_ref, acc_ref, xb_ref, *, lp, nbatch, tm):
    i = pl.program_id(0)
    f = pl.program_id(1)

    @pl.when(f == 0)
    def _():
        acc_ref[...] = jnp.zeros_like(acc_ref)
        xb_ref[...] = x_ref[...].astype(BF16)

    xb = xb_ref[...]
    gate = jnp.dot(xb, wg_ref[...], preferred_element_type=F32)
    up = jnp.dot(xb, wu_ref[...], preferred_element_type=F32)
    acc_ref[...] += jnp.dot(_silu_mul(gate, up).astype(BF16), wd_ref[...], preferred_element_type=F32)

    @pl.when(f == pl.num_programs(1) - 1)
    def _():
        y = _layer_norm(ALPHA * x_ref[...] + acc_ref[...], g_ref[...], b_ref[...])
        o_ref[...] = jnp.where(_pad_row_mask(i * tm, tm, lp, nbatch), 0.0, y)


def _ffn(x, w_gu, w_down, g, b, lp, nbatch):
    m, d = x.shape
    ff = w_down.shape[0]
    tm = _pick(m, (384, 256, 128))
    tf = _pick(ff, (1408, 896, 512, 256, 128))
    nf = ff // tf
    row = lambda i, f: (i, 0)
    fixed = lambda i, f: (0, 0)
    return pl.pallas_call(
        functools.partial(_ffn_kernel, lp=lp, nbatch=nbatch, tm=tm),
        grid=(m // tm, nf),
        in_specs=[pl.BlockSpec((tm, d), row),
                  pl.BlockSpec((d, tf), lambda i, f: (0, f)),
                  pl.BlockSpec((d, tf), lambda i, f: (0, f + nf)),
                  pl.BlockSpec((tf, d), lambda i, f: (f, 0)),
                  pl.BlockSpec((1, d), fixed), pl.BlockSpec((1, d), fixed)],
        out_specs=pl.BlockSpec((tm, d), row),
        out_shape=jax.ShapeDtypeStruct((m, d), F32),
        scratch_shapes=[pltpu.VMEM((tm, d), F32), pltpu.VMEM((tm, d), BF16)],
        compiler_params=_params("parallel", "arbitrary"),
        name="swiglu_res_ln",
    )(x, w_gu, w_gu, w_down, g, b)


def _moe_kernel(tile_e_ref, nact_ref, src_ref, src_next_ref, dst_ref, x_hbm, wg_ref, wu_ref, wd_ref,
                y_hbm, xbuf, xb_ref, acc_ref, ybuf, gsem, ssem, *, tm):
    j = pl.program_id(0)
    f = pl.program_id(1)
    nf = pl.num_programs(1)
    nact = nact_ref[0]
    active = j < nact
    slot = lax.rem(j, 2)

    def gather_start(idx_ref, s):
        def body(i, carry):
            t = idx_ref[0, 0, i]
            pltpu.make_async_copy(x_hbm.at[pl.ds(t, 1)], xbuf.at[s, pl.ds(i, 1)], gsem.at[s]).start()
            return carry
        lax.fori_loop(0, tm, body, 0, unroll=DMA_ISSUE_UNROLL)

    def gather_wait(s):
        pltpu.make_async_copy(x_hbm.at[pl.ds(0, tm)], xbuf.at[s], gsem.at[s]).wait()

    def scatter_start(s):
        def body(i, carry):
            t = dst_ref[0, 0, i]
            pltpu.make_async_copy(ybuf.at[s, pl.ds(i, 1)], y_hbm.at[pl.ds(t, 1)], ssem.at[s]).start()
            return carry
        lax.fori_loop(0, tm, body, 0, unroll=DMA_ISSUE_UNROLL)

    def scatter_wait(s):
        pltpu.make_async_copy(ybuf.at[s], y_hbm.at[pl.ds(0, tm)], ssem.at[s]).wait()

    @pl.when(jnp.logical_and(active, f == 0))
    def _():
        @pl.when(j == 0)
        def _():
            gather_start(src_ref, 0)

        gather_wait(slot)

        @pl.when(j + 1 < nact)
        def _():
            gather_start(src_next_ref, 1 - slot)

        xb_ref[...] = xbuf[slot].astype(BF16)
        acc_ref[...] = jnp.zeros_like(acc_ref)

    @pl.when(active)
    def _():
        xb = xb_ref[...]
        gate = jnp.dot(xb, wg_ref[0], preferred_element_type=F32)
        up = jnp.dot(xb, wu_ref[0], preferred_element_type=F32)
        acc_ref[...] += jnp.dot(_silu_mul(gate, up).astype(BF16), wd_ref[0], preferred_element_type=F32)

    @pl.when(f == nf - 1)
    def _():
        @pl.when(active)
        def _():
            ybuf[slot] = acc_ref[...]

        @pl.when(jnp.logical_not(active))
        def _():
            ybuf[slot] = jnp.zeros(ybuf.shape[1:], ybuf.dtype)

        scatter_start(slot)

        @pl.when(j >= 1)
        def _():
            scatter_wait(1 - slot)

        @pl.when(j == pl.num_programs(0) - 1)
        def _():
            scatter_wait(slot)


def _moe_experts(x, expert_idx, w_gu, w_down):
    m, d = x.shape
    ne, ff = w_down.shape[0], w_down.shape[1]
    nslots = 2 * m
    tm = _pick(nslots, (512, 256, 128))
    tf = _pick(ff, (896, 512, 256, 128))
    nf = ff // tf
    n_tiles = nslots // tm + ne
    p_rows = n_tiles * tm

    e_flat = expert_idx.reshape(nslots)
    onehot = (e_flat[:, None] == jnp.arange(ne, dtype=jnp.int32)[None, :]).astype(jnp.int32)
    csum = jnp.cumsum(onehot, axis=0)
    rank = jnp.take_along_axis(csum, e_flat[:, None], axis=1)[:, 0] - 1
    counts = csum[-1]
    gsize = ((counts + tm - 1) // tm) * tm
    gend = jnp.cumsum(gsize)
    dest = (gend - gsize)[e_flat] + rank
    slot_ids = jnp.arange(nslots, dtype=jnp.int32)
    slot_sorted = jnp.full((p_rows,), -1, jnp.int32).at[dest].set(slot_ids)
    is_fill = slot_sorted < 0
    fill_rank = jnp.cumsum(is_fill.astype(jnp.int32)) - 1
    src_sorted = jnp.where(is_fill, 0, slot_sorted // 2)
    dst_sorted = jnp.where(is_fill, nslots + fill_rank, (slot_sorted % 2) * m + slot_sorted // 2)
    tile_e = jnp.minimum(jnp.searchsorted(gend, jnp.arange(n_tiles, dtype=jnp.int32) * tm, side="right"),
                         ne - 1).astype(jnp.int32)
    nact = (gend[-1:] // tm).astype(jnp.int32)
    src3 = src_sorted.reshape(n_tiles, 1, tm)
    dst3 = dst_sorted.reshape(n_tiles, 1, tm)

    def f_eff(j, f, nact_ref):
        return jnp.where(j < nact_ref[0], f, nf - 1)

    smem_blk = lambda imap: pl.BlockSpec((1, 1, tm), imap, memory_space=pltpu.SMEM)
    grid_spec = pltpu.PrefetchScalarGridSpec(
        num_scalar_prefetch=2,
        grid=(n_tiles, nf),
        in_specs=[smem_blk(lambda j, f, te, na: (j, 0, 0)),
                  smem_blk(lambda j, f, te, na: (jnp.minimum(j + 1, n_tiles - 1), 0, 0)),
                  smem_blk(lambda j, f, te, na: (j, 0, 0)),
                  pl.BlockSpec(memory_space=pl.ANY),
                  pl.BlockSpec((1, d, tf), lambda j, f, te, na: (te[j], 0, f_eff(j, f, na))),
                  pl.BlockSpec((1, d, tf), lambda j, f, te, na: (te[j], 0, f_eff(j, f, na) + nf)),
                  pl.BlockSpec((1, tf, d), lambda j, f, te, na: (te[j], f_eff(j, f, na), 0))],
        out_specs=pl.BlockSpec(memory_space=pl.ANY),
        scratch_shapes=[pltpu.VMEM((2, tm, d), F32), pltpu.VMEM((tm, d), BF16), pltpu.VMEM((tm, d), F32),
                        pltpu.VMEM((2, tm, d), F32), pltpu.SemaphoreType.DMA((2,)),
                        pltpu.SemaphoreType.DMA((2,))],
    )
    return pl.pallas_call(
        functools.partial(_moe_kernel, tm=tm),
        grid_spec=grid_spec,
        out_shape=jax.ShapeDtypeStruct((p_rows, d), F32),
        compiler_params=_params("arbitrary", "arbitrary"),
        name="moe_grouped_swiglu",
    )(tile_e, nact, src3, src3, dst3, x, w_gu, w_gu, w_down)


def _combine_kernel(x_ref, y1_ref, y2_ref, ro_ref, g_ref, b_ref, o_ref):
    ro = ro_ref[...]
    lane = lax.broadcasted_iota(jnp.int32, ro.shape, 1)
    w1 = jnp.sum(jnp.where(lane == 2, ro, 0.0), axis=-1, keepdims=True)
    w2 = jnp.sum(jnp.where(lane == 3, ro, 0.0), axis=-1, keepdims=True)
    z = ALPHA * x_ref[...] + w1 * y1_ref[...] + w2 * y2_ref[...]
    o_ref[0] = _layer_norm(z, g_ref[...], b_ref[...])


def _moe_combine(x, y_slots, router_out, g, b, nbatch, lp):
    m, d = x.shape
    tm = BLOCK
    nb = lp // tm
    choice2 = m // tm
    row = lambda i, j: (i * nb + j + 1, 0)
    fixed = lambda i, j: (0, 0)
    return pl.pallas_call(
        _combine_kernel,
        grid=(nbatch, nb - 1),
        in_specs=[pl.BlockSpec((tm, d), row), pl.BlockSpec((tm, d), row),
                  pl.BlockSpec((tm, d), lambda i, j: (choice2 + i * nb + j + 1, 0)),
                  pl.BlockSpec((tm, LANES), row),
                  pl.BlockSpec((1, d), fixed), pl.BlockSpec((1, d), fixed)],
        out_specs=pl.BlockSpec((1, tm, d), lambda i, j: (i, j, 0)),
        out_shape=jax.ShapeDtypeStruct((nbatch, lp - tm, d), F32),
        compiler_params=_params("parallel", "parallel"),
        name="moe_combine_ln",
    )(x, y_slots, y_slots, router_out, g, b)


def _rwkv_proj_kernel(x_ref, xp_ref, mu_ref, wr_ref, wk_ref, wv_ref, w0_ref, w1_ref, w2_ref,
                      a0_ref, a1_ref, a2_ref, g1_ref, g2_ref,
                      r_ref, k_ref, v_ref, a_ref, ld_ref, g_ref, *, lp, nbatch):
    x = x_ref[...]
    tm = x.shape[0]
    rolled = pltpu.roll(x, 1, axis=0)
    prev_row = xp_ref[7:8, :]
    first = lax.broadcasted_iota(jnp.int32, (tm, 1), 0) == 0
    xx = jnp.where(first, prev_row, rolled) - x
    xx = jnp.where(_pad_row_mask(pl.program_id(0) * tm, tm, lp, nbatch), 0.0, xx)

    def mix(i):
        return x + xx * mu_ref[i:i + 1, :]

    r_ref[...] = _dot(mix(0), wr_ref[...])
    k_ref[...] = _dot(mix(2), wk_ref[...])
    v_ref[...] = _dot(mix(3), wv_ref[...])
    wl = w0_ref[...] + _dot(jnp.tanh(_dot(mix(1), w1_ref[...])), w2_ref[...])
    z = -wl
    softplus = jnp.maximum(z, 0.0) + jnp.log(1.0 + jnp.exp(-jnp.abs(z)))
    ld_ref[...] = -jnp.exp(-softplus - 0.5)
    al = a0_ref[...] + _dot(_dot(mix(4), a1_ref[...]), a2_ref[...])
    a_ref[...] = 1.0 / (1.0 + jnp.exp(-al))
    gl = _dot(mix(5), g1_ref[...])
    g_ref[...] = _dot(1.0 / (1.0 + jnp.exp(-gl)), g2_ref[...])


def _rwkv_proj(x, mu, w_rkv, w0, w1, w2, a0, a1, a2, g1, g2, lp, nbatch):
    m, d = x.shape
    tm = _pick(m, (384, 256, 128))
    row = lambda i: (i, 0)
    fixed = lambda i: (0, 0)
    full = lambda arr: pl.BlockSpec(arr.shape, fixed)
    out = jax.ShapeDtypeStruct((m, d), F32)
    return pl.pallas_call(
        functools.partial(_rwkv_proj_kernel, lp=lp, nbatch=nbatch),
        grid=(m // tm,),
        in_specs=[pl.BlockSpec((tm, d), row),
                  pl.BlockSpec((8, d), lambda i: (jnp.maximum(i * (tm // 8) - 1, 0), 0)),
                  full(mu), full(w_rkv[0]), full(w_rkv[1]), full(w_rkv[2]),
                  full(w0), full(w1), full(w2), full(a0), full(a1), full(a2), full(g1), full(g2)],
        out_specs=[pl.BlockSpec((tm, d), row)] * 6,
        out_shape=[out] * 6,
        compiler_params=_params("parallel"),
        name="rwkv_proj",
    )(x, x, mu, w_rkv[0], w_rkv[1], w_rkv[2], w0, w1, w2, a0, a1, a2, g1, g2)


def _bmm(a, b):
    return jnp.einsum("cij,cjk->cik", a.astype(BF16), b.astype(BF16), preferred_element_type=F32)


def _bmm_nt(a, b):
    return jnp.einsum("cik,cjk->cij", a.astype(BF16), b.astype(BF16), preferred_element_type=F32)


def _bmm_tn(a, b):
    return jnp.einsum("cki,ckj->cij", a.astype(BF16), b.astype(BF16), preferred_element_type=F32)


def _wkv_kernel(r_ref, k_ref, v_ref, a_ref, ld_ref, kk_ref, ka_ref, rk_ref, gg_ref, gb_ref,
                y_ref, z_ref):
    c = WKV_CHUNK
    n = RWKV_HEAD
    tb = r_ref.shape[1]
    nc = tb // c

    @pl.when(pl.program_id(2) == 0)
    def _():
        z_ref[...] = jnp.zeros_like(z_ref)

    ti = lax.broadcasted_iota(jnp.int32, (1, c, c), 1)
    si = lax.broadcasted_iota(jnp.int32, (1, c, c), 2)
    strict = ti > si
    incl = ti >= si
    eye = ti == si
    tri = jnp.broadcast_to(jnp.where(incl, 1.0, 0.0).astype(BF16), (nc, c, c))

    for hh in range(LANES // n):
        sl = slice(hh * n, (hh + 1) * n)
        r = r_ref[0, :, sl].reshape(nc, c, n)
        k = k_ref[0, :, sl].reshape(nc, c, n)
        v = v_ref[0, :, sl].reshape(nc, c, n)
        a = a_ref[0, :, sl].reshape(nc, c, n)
        ld = ld_ref[0, :, sl].reshape(nc, c, n)
        k_k = kk_ref[:, sl].reshape(1, 1, n)
        k_a = ka_ref[:, sl].reshape(1, 1, n)
        r_k = rk_ref[:, sl].reshape(1, 1, n)

        kkr = k * k_k
        kk = kkr / jnp.maximum(jnp.sqrt(jnp.sum(kkr * kkr, axis=-1, keepdims=True)), 1e-12)
        km = k * (1.0 + (a - 1.0) * k_a)
        bv = kk * a

        ld_hi = ld.astype(BF16)
        ld_r1 = ld - ld_hi.astype(F32)
        ld_mid = ld_r1.astype(BF16)
        ld_lo = (ld_r1 - ld_mid.astype(F32)).astype(BF16)
        cs = (jnp.einsum("cij,cjk->cik", tri, ld_hi, preferred_element_type=F32)
              + jnp.einsum("cij,cjk->cik", tri, ld_mid, preferred_element_type=F32)
              + jnp.einsum("cij,cjk->cik", tri, ld_lo, preferred_element_type=F32))
        cs_last = cs[:, c - 1:c, :]
        e_neg = jnp.exp(-cs)
        e_tail = jnp.exp(cs_last - cs)
        at = -kk * jnp.exp(cs - ld)
        rt = r * jnp.exp(cs)
        bt = bv * e_neg
        kt = km * e_neg
        bh = bv * e_tail
        kh = km * e_tail
        dec = jnp.exp(cs_last)

        x2 = jnp.concatenate([at, rt], axis=1)
        gb_ = _bmm_nt(x2, bt)
        gk_ = _bmm_nt(x2, kt)
        lab = jnp.where(strict, gb_[:, :c], 0.0)
        mrb = jnp.where(incl, gb_[:, c:], 0.0)
        lak = jnp.where(strict, gk_[:, :c], 0.0)
        mrk = jnp.where(incl, gk_[:, c:], 0.0)

        t_inv = jnp.where(eye, 1.0, lab)
        lp = lab
        steps = int(math.log2(c)) - 1
        for _ in range(steps):
            lp = _bmm(lp, lp)
            t_inv = t_inv + _bmm(t_inv, lp)

        w = _bmm(t_inv, at)
        u0 = _bmm(t_inv, _bmm(lak, v))
        q = rt + _bmm(mrb, w)
        yp = _bmm(mrb, u0) + _bmm(mrk, v)
        a_t = jnp.where(eye, dec, 0.0) + _bmm_tn(bh, w)
        g_t = _bmm_tn(bh, u0) + _bmm_tn(kh, v)

        z = z_ref[hh]
        ys = []
        for ci in range(nc):
            ys.append(yp[ci] + _dot(q[ci], z))
            z = _dot(a_t[ci], z) + g_t[ci]
        z_ref[hh] = z
        y = jnp.concatenate(ys, axis=0)

        mu = jnp.mean(y, axis=-1, keepdims=True)
        yc = y - mu
        var = jnp.mean(yc * yc, axis=-1, keepdims=True)
        yn = yc * lax.rsqrt(var + GN_EPS) * gg_ref[:, sl] + gb_ref[:, sl]
        rf = r.reshape(tb, n)
        kf = km.reshape(tb, n)
        vf = v.reshape(tb, n)
        bonus = jnp.sum(rf * kf * rk_ref[:, sl], axis=-1, keepdims=True) * vf
        y_ref[0, :, sl] = yn + bonus


def _wkv(r, k, v, a, ld, k_k, k_a, r_k, gn_g, gn_b):
    b, lp, d = r.shape
    tb = _pick(lp, (384, 256, 128))
    tok = lambda i, j, t: (i, t, j)
    par = lambda i, j, t: (0, j)
    tok_spec = pl.BlockSpec((1, tb, LANES), tok)
    par_spec = pl.BlockSpec((1, LANES), par)
    return pl.pallas_call(
        _wkv_kernel,
        grid=(b, d // LANES, lp // tb),
        in_specs=[tok_spec] * 5 + [par_spec] * 5,
        out_specs=tok_spec,
        out_shape=jax.ShapeDtypeStruct((b, lp, d), F32),
        scratch_shapes=[pltpu.VMEM((LANES // RWKV_HEAD, RWKV_HEAD, RWKV_HEAD), F32)],
        compiler_params=_params("parallel", "parallel", "arbitrary"),
        name="wkv7_chunked",
    )(r, k, v, a, ld, k_k, k_a, r_k, gn_g, gn_b)


def _router_kernel(x_ref, w_ref, o_ref):
    logits = jnp.dot(x_ref[...], w_ref[...], preferred_element_type=F32, precision=lax.Precision.HIGHEST)
    lane = lax.broadcasted_iota(jnp.int32, logits.shape, 1).astype(F32)
    neg = -jnp.inf
    lg = jnp.where(lane < N_EXPERTS, logits, neg)
    m1 = jnp.max(lg, axis=-1, keepdims=True)
    i1 = jnp.min(jnp.where(lg == m1, lane, float(LANES)), axis=-1, keepdims=True)
    lg2 = jnp.where(lane == i1, neg, lg)
    m2 = jnp.max(lg2, axis=-1, keepdims=True)
    i2 = jnp.min(jnp.where(lg2 == m2, lane, float(LANES)), axis=-1, keepdims=True)
    e2 = jnp.exp(m2 - m1)
    den = 1.0 + e2
    o_ref[...] = jnp.where(lane == 0.0, i1, jnp.where(lane == 1.0, i2, jnp.where(
        lane == 2.0, 1.0 / den, jnp.where(lane == 3.0, e2 / den, 0.0))))


def _router(x, w_pad):
    m, d = x.shape
    tm = _pick(m, (768, 384, 256, 128))
    return pl.pallas_call(
        _router_kernel,
        grid=(m // tm,),
        in_specs=[pl.BlockSpec((tm, d), lambda i: (i, 0)), pl.BlockSpec((d, LANES), lambda i: (0, 0))],
        out_specs=pl.BlockSpec((tm, LANES), lambda i: (i, 0)),
        out_shape=jax.ShapeDtypeStruct((m, LANES), F32),
        compiler_params=_params("parallel"),
        name="moe_router",
    )(x, w_pad)


def _t5_bucket(dist):
    n = jnp.maximum(dist, 0)
    is_small = n < MAX_EXACT
    nf = jnp.maximum(n, 1).astype(F32)
    large = MAX_EXACT + (jnp.log(nf / MAX_EXACT) / math.log(MAX_DISTANCE / MAX_EXACT)
                         * (NUM_BUCKETS - MAX_EXACT)).astype(jnp.int32)
    large = jnp.minimum(large, NUM_BUCKETS - 1)
    return jnp.where(is_small, n, large)


def _bias_tables(rel_bias):
    rb = rel_bias.astype(F32)
    d_band = BLOCK + jnp.arange(BLOCK)[:, None] - jnp.arange(2 * BLOCK)[None, :]
    ok = (d_band >= 0) & (d_band < WINDOW)
    onehot = (_t5_bucket(d_band)[..., None] == jnp.arange(NUM_BUCKETS)).astype(F32)
    looked_up = jnp.einsum("qkn,nh->hqk", onehot, rb, precision=lax.Precision.HIGHEST)
    band = jnp.where(ok[None], looked_up, NEG_INF)
    meta = rb[_t5_bucket(jnp.array(WINDOW))]
    return band, meta


def kernel(x, meta_tokens, rel_bias, ln_mix_g, ln_mix_b, ln_ffn_g, ln_ffn_b, attn_w_qkv, attn_b_qkv, attn_sinks, attn_w_o, rwkv_mu, rwkv_w0, rwkv_w1, rwkv_w2, rwkv_a0, rwkv_a1, rwkv_a2, rwkv_g1, rwkv_g2, rwkv_k_k, rwkv_k_a, rwkv_r_k, rwkv_w_rkv, rwkv_lnx_g, rwkv_lnx_b, rwkv_w_o, ffn_w_gu, ffn_w_down, moe_router, moe_w_gu, moe_w_down):
    b, seq, d = x.shape
    lp = seq + BLOCK
    m = b * lp
    row = lambda t: t.reshape(1, -1).astype(F32)

    meta = jnp.broadcast_to(meta_tokens.astype(x.dtype)[None], (b, N_META, d))
    h = jnp.concatenate([jnp.zeros((b, PAD, d), x.dtype), meta, x], axis=1).reshape(m, d)

    q_dim = N_HEADS * HEAD_DIM
    e_dim = attn_w_qkv.shape[2]
    col_scale = jnp.where(jnp.arange(e_dim) < q_dim, ATTN_SCALE, 1.0).astype(F32)
    qkv = _qkv_proj(h, attn_w_qkv[0].astype(BF16), row(attn_b_qkv[0]), row(col_scale))
    bias_band, bias_meta = _bias_tables(rel_bias)
    o = _attention(qkv.reshape(b, lp, e_dim), bias_band, attn_sinks[0].astype(F32), bias_meta)
    h = _proj_res_ln(o.reshape(m, q_dim), attn_w_o[0].astype(BF16), h, row(ln_mix_g[0]), row(ln_mix_b[0]))
    h = _ffn(h, ffn_w_gu[0].astype(BF16), ffn_w_down[0].astype(BF16), row(ln_ffn_g[0]), row(ln_ffn_b[0]), lp, b)

    r, k, v, a, ld, g = _rwkv_proj(
        h, rwkv_mu[0], rwkv_w_rkv[0].astype(BF16), row(rwkv_w0[0]), rwkv_w1[0].astype(BF16),
        rwkv_w2[0].astype(BF16), row(rwkv_a0[0]), rwkv_a1[0].astype(BF16), rwkv_a2[0].astype(BF16),
        rwkv_g1[0].astype(BF16), rwkv_g2[0].astype(BF16), lp, b)
    t3 = lambda t: t.reshape(b, lp, d)
    y = _wkv(t3(r), t3(k), t3(v), t3(a), t3(ld), row(rwkv_k_k[0]), row(rwkv_k_a[0]), row(rwkv_r_k[0]),
             row(rwkv_lnx_g[0]), row(rwkv_lnx_b[0]))
    h = _proj_res_ln(y.reshape(m, d), rwkv_w_o[0].astype(BF16), h, row(ln_mix_g[1]), row(ln_mix_b[1]), mult=g)
    w_router = jnp.pad(moe_router[0].astype(F32), ((0, 0), (0, LANES - N_EXPERTS)))
    routed = _router(h, w_router)
    expert_idx = routed[:, :2].astype(jnp.int32)
    y_slots = _moe_experts(h, expert_idx, moe_w_gu[0].astype(BF16), moe_w_down[0].astype(BF16))
    return _moe_combine(h, y_slots, routed, row(ln_ffn_g[1]), row(ln_ffn_b[1]), b, lp)
```

```python
import functools
import math

import jax
import jax.numpy as jnp
from jax import lax
from jax.experimental import pallas as pl
from jax.experimental.pallas import tpu as pltpu

F32 = jnp.float32
BF16 = jnp.bfloat16

N_META = 16
N_HEADS = 16
N_KV_HEADS = 4
HEAD_DIM = 64
GROUP = N_HEADS // N_KV_HEADS
WINDOW = 128
BLOCK = 128
PAD = BLOCK - N_META
ATTN_SCALE = 1.0 / math.sqrt(HEAD_DIM)
NEG_INF = -1e30
NUM_BUCKETS = 32
MAX_EXACT = NUM_BUCKETS // 2
MAX_DISTANCE = 128
RWKV_HEAD = 64
GN_EPS = 64e-5
N_EXPERTS = 8
DEPTH = 2
ALPHA = (2 * DEPTH) ** 0.25
LN_EPS = 1e-5

WKV_CHUNK = 64
LANES = 128
DMA_ISSUE_UNROLL = 8
VMEM_LIMIT_BYTES = 56 * 1024 * 1024


def _params(*sem):
    return pltpu.CompilerParams(dimension_semantics=sem, vmem_limit_bytes=VMEM_LIMIT_BYTES)


def _pick(n, candidates):
    for c in candidates:
        if n % c == 0:
            return c
    raise ValueError(f"no tile in {candidates} divides {n}")


def _pad_row_mask(first_row, rows, lp, nbatch):
    r = first_row + lax.broadcasted_iota(jnp.int32, (rows, 1), 0)
    mask = r < PAD
    for bi in range(1, nbatch):
        mask = jnp.logical_or(mask, jnp.logical_and(r >= bi * lp, r < bi * lp + PAD))
    return mask


def _layer_norm(z, g, b):
    mu = jnp.mean(z, axis=-1, keepdims=True)
    zc = z - mu
    var = jnp.mean(zc * zc, axis=-1, keepdims=True)
    return zc * lax.rsqrt(var + LN_EPS) * g + b


def _dot(a, b):
    return jnp.dot(a.astype(BF16), b.astype(BF16), preferred_element_type=F32)


def _dot_nt(a, b):
    return lax.dot_general(a.astype(BF16), b.astype(BF16), (((1,), (1,)), ((), ())),
                           preferred_element_type=F32)


def _qkv_kernel(x_ref, w_ref, b_ref, s_ref, o_ref):
    acc = _dot(x_ref[...], w_ref[...])
    o_ref[...] = ((acc + b_ref[...]) * s_ref[...]).astype(o_ref.dtype)


def _qkv_proj(x, w, b, s):
    m, d = x.shape
    n = w.shape[1]
    tm = _pick(m, (768, 384, 256, 128))
    return pl.pallas_call(
        _qkv_kernel,
        grid=(m // tm,),
        in_specs=[pl.BlockSpec((tm, d), lambda i: (i, 0)),
                  pl.BlockSpec((d, n), lambda i: (0, 0)),
                  pl.BlockSpec((1, n), lambda i: (0, 0)),
                  pl.BlockSpec((1, n), lambda i: (0, 0))],
        out_specs=pl.BlockSpec((tm, n), lambda i: (i, 0)),
        out_shape=jax.ShapeDtypeStruct((m, n), BF16),
        compiler_params=_params("parallel"),
        name="qkv_proj",
    )(x, w, b, s)


def _attn_kernel(sink_ref, mbias_ref, cur_ref, prev_ref, meta_ref, bias_ref, o_ref):
    n = pl.program_id(1)
    q_dim = N_HEADS * HEAD_DIM
    kv_dim = N_KV_HEADS * HEAD_DIM
    cur = cur_ref[0]
    prev = prev_ref[0]
    meta = meta_ref[0][PAD:, :]

    col = lax.broadcasted_iota(jnp.int32, (1, 2 * BLOCK), 1)
    key_ok = (col + (n - 1) * BLOCK) >= PAD
    qrow = jnp.bitwise_and(lax.broadcasted_iota(jnp.int32, (GROUP * BLOCK, N_META), 0), BLOCK - 1)
    midx = lax.broadcasted_iota(jnp.int32, (GROUP * BLOCK, N_META), 1)
    meta_ok = (n * BLOCK + qrow - PAD - midx) >= WINDOW
    grp = jnp.right_shift(lax.broadcasted_iota(jnp.int32, (GROUP * BLOCK, 1), 0), int(math.log2(BLOCK)))

    for h in range(N_KV_HEADS):
        ks = slice(h * HEAD_DIM, (h + 1) * HEAD_DIM)
        vs = slice(kv_dim + h * HEAD_DIM, kv_dim + (h + 1) * HEAD_DIM)
        kb = jnp.concatenate([prev[:, ks], cur[:, q_dim + h * HEAD_DIM:q_dim + (h + 1) * HEAD_DIM]], axis=0)
        vb = jnp.concatenate([prev[:, vs], cur[:, q_dim + kv_dim + h * HEAD_DIM:
                                               q_dim + kv_dim + (h + 1) * HEAD_DIM]], axis=0)
        km = meta[:, ks]
        vm = meta[:, vs]
        q4 = jnp.concatenate(
            [cur[:, (h * GROUP + g) * HEAD_DIM:(h * GROUP + g + 1) * HEAD_DIM] for g in range(GROUP)], axis=0)

        sink_col = jnp.zeros((GROUP * BLOCK, 1), F32)
        mb_col = jnp.zeros((GROUP * BLOCK, 1), F32)
        for g in range(GROUP):
            sink_col = jnp.where(grp == g, sink_ref[h * GROUP + g], sink_col)
            mb_col = jnp.where(grp == g, mbias_ref[h * GROUP + g], mb_col)

        bias4 = bias_ref[h * GROUP:(h + 1) * GROUP].reshape(GROUP * BLOCK, 2 * BLOCK)
        s = jnp.where(key_ok, _dot_nt(q4, kb) + bias4, NEG_INF)
        sm = jnp.where(meta_ok, _dot_nt(q4, km) + mb_col, NEG_INF)
        mx = jnp.maximum(jnp.maximum(jnp.max(s, axis=-1, keepdims=True),
                                     jnp.max(sm, axis=-1, keepdims=True)), sink_col)
        p = jnp.exp(s - mx)
        pm = jnp.exp(sm - mx)
        denom = (jnp.sum(p, axis=-1, keepdims=True) + jnp.sum(pm, axis=-1, keepdims=True)
                 + jnp.exp(sink_col - mx))
        o4 = (_dot(p, vb) + _dot(pm, vm)) / denom
        for g in range(GROUP):
            hd = h * GROUP + g
            o_ref[0, :, hd * HEAD_DIM:(hd + 1) * HEAD_DIM] = o4[g * BLOCK:(g + 1) * BLOCK].astype(o_ref.dtype)


def _attention(qkv, bias_tbl, sinks, meta_bias):
    b, lp, e = qkv.shape
    nb = lp // BLOCK
    q_dim = N_HEADS * HEAD_DIM
    kv2 = 2 * N_KV_HEADS * HEAD_DIM
    kv_blk = q_dim // kv2
    smem = pl.BlockSpec(memory_space=pltpu.SMEM)
    return pl.pallas_call(
        _attn_kernel,
        grid=(b, nb),
        in_specs=[smem, smem,
                  pl.BlockSpec((1, BLOCK, e), lambda i, j: (i, j, 0)),
                  pl.BlockSpec((1, BLOCK, kv2), lambda i, j: (i, jnp.maximum(j - 1, 0), kv_blk)),
                  pl.BlockSpec((1, BLOCK, kv2), lambda i, j: (i, 0, kv_blk)),
                  pl.BlockSpec((N_HEADS, BLOCK, 2 * BLOCK), lambda i, j: (0, 0, 0))],
        out_specs=pl.BlockSpec((1, BLOCK, q_dim), lambda i, j: (i, j, 0)),
        out_shape=jax.ShapeDtypeStruct((b, lp, q_dim), BF16),
        compiler_params=_params("parallel", "parallel"),
        name="swa_attention",
    )(sinks, meta_bias, qkv, qkv, qkv, bias_tbl)


def _proj_ln_kernel(*refs, has_mult):
    if has_mult:
        a_ref, m_ref, w_ref, res_ref, g_ref, b_ref, o_ref = refs
        a = a_ref[...].astype(F32) * m_ref[...].astype(F32)
    else:
        a_ref, w_ref, res_ref, g_ref, b_ref, o_ref = refs
        a = a_ref[...]
    y = _dot(a, w_ref[...])
    o_ref[...] = _layer_norm(ALPHA * res_ref[...] + y, g_ref[...], b_ref[...])


def _proj_res_ln(a, w, res, g, b, mult=None):
    m, k = a.shape
    d = w.shape[1]
    tm = _pick(m, (768, 384, 256, 128))
    row = lambda i: (i, 0)
    fixed = lambda i: (0, 0)
    ins = [a] + ([mult] if mult is not None else []) + [w, res, g, b]
    specs = ([pl.BlockSpec((tm, k), row)] + ([pl.BlockSpec((tm, k), row)] if mult is not None else [])
             + [pl.BlockSpec((k, d), fixed), pl.BlockSpec((tm, d), row),
                pl.BlockSpec((1, d), fixed), pl.BlockSpec((1, d), fixed)])
    return pl.pallas_call(
        functools.partial(_proj_ln_kernel, has_mult=mult is not None),
        grid=(m // tm,),
        in_specs=specs,
        out_specs=pl.BlockSpec((tm, d), row),
        out_shape=jax.ShapeDtypeStruct((m, d), F32),
        compiler_params=_params("parallel"),
        name="proj_res_ln",
    )(*ins)


def _silu_mul(gate, up):
    return gate * (1.0 / (1.0 + jnp.exp(-gate))) * up


def _ffn_kernel(x_ref, wg_ref, wu_ref, wd_ref, g_ref, b_ref, o_ref, acc_ref, xb_ref, *, lp, nbatch, tm):
    i = pl.program_id(0)
    f = pl.program_id(1)

    @pl.when(f == 0)
    def _():
        acc_ref[...] = jnp.zeros_like(acc_ref)
        xb_ref[...] = x_ref[...].astype(BF16)

    xb = xb_ref[...]
    gate = jnp.dot(xb, wg_ref[...], preferred_element_type=F32)
    up = jnp.dot(xb, wu_ref[...], preferred_element_type=F32)
    acc_ref[...] += jnp.dot(_silu_mul(gate, up).astype(BF16), wd_ref[...], preferred_element_type=F32)

    @pl.when(f == pl.num_programs(1) - 1)
    def _():
        y = _layer_norm(ALPHA * x_ref[...] + acc_ref[...], g_ref[...], b_ref[...])
        o_ref[...] = jnp.where(_pad_row_mask(i * tm, tm, lp, nbatch), 0.0, y)


def _ffn(x, w_gu, w_down, g, b, lp, nbatch):
    m, d = x.shape
    ff = w_down.shape[0]
    tm = _pick(m, (384, 256, 128))
    tf = _pick(ff, (1408, 896, 512, 256, 128))
    nf = ff // tf
    row = lambda i, f: (i, 0)
    fixed = lambda i, f: (0, 0)
    return pl.pallas_call(
        functools.partial(_ffn_kernel, lp=lp, nbatch=nbatch, tm=tm),
        grid=(m // tm, nf),
        in_specs=[pl.BlockSpec((tm, d), row),
                  pl.BlockSpec((d, tf), lambda i, f: (0, f)),
                  pl.BlockSpec((d, tf), lambda i, f: (0, f + nf)),
                  pl.BlockSpec((tf, d), lambda i, f: (f, 0)),
                  pl.BlockSpec((1, d), fixed), pl.BlockSpec((1, d), fixed)],
        out_specs=pl.BlockSpec((tm, d), row),
        out_shape=jax.ShapeDtypeStruct((m, d), F32),
        scratch_shapes=[pltpu.VMEM((tm, d), F32), pltpu.VMEM((tm, d), BF16)],
        compiler_params=_params("parallel", "arbitrary"),
        name="swiglu_res_ln",
    )(x, w_gu, w_gu, w_down, g, b)


def _moe_kernel(tile_e_ref, nact_ref, src_ref, src_next_ref, dst_ref, x_hbm, wg_ref, wu_ref, wd_ref,
                y_hbm, xbuf, xb_ref, acc_ref, ybuf, gsem, ssem, *, tm):
    j = pl.program_id(0)
    f = pl.program_id(1)
    nf = pl.num_programs(1)
    nact = nact_ref[0]
    active = j < nact
    slot = lax.rem(j, 2)

    def gather_start(idx_ref, s):
        def body(i, carry):
            t = idx_ref[0, 0, i]
            pltpu.make_async_copy(x_hbm.at[pl.ds(t, 1)], xbuf.at[s, pl.ds(i, 1)], gsem.at[s]).start()
            return carry
        lax.fori_loop(0, tm, body, 0, unroll=DMA_ISSUE_UNROLL)

    def gather_wait(s):
        pltpu.make_async_copy(x_hbm.at[pl.ds(0, tm)], xbuf.at[s], gsem.at[s]).wait()

    def scatter_start(s):
        def body(i, carry):
            t = dst_ref[0, 0, i]
            pltpu.make_async_copy(ybuf.at[s, pl.ds(i, 1)], y_hbm.at[pl.ds(t, 1)], ssem.at[s]).start()
            return carry
        lax.fori_loop(0, tm, body, 0, unroll=DMA_ISSUE_UNROLL)

    def scatter_wait(s):
        pltpu.make_async_copy(ybuf.at[s], y_hbm.at[pl.ds(0, tm)], ssem.at[s]).wait()

    @pl.when(jnp.logical_and(active, f == 0))
    def _():
        @pl.when(j == 0)
        def _():
            gather_start(src_ref, 0)

        gather_wait(slot)

        @pl.when(j + 1 < nact)
        def _():
            gather_start(src_next_ref, 1 - slot)

        xb_ref[...] = xbuf[slot].astype(BF16)
        acc_ref[...] = jnp.zeros_like(acc_ref)

    @pl.when(active)
    def _():
        xb = xb_ref[...]
        gate = jnp.dot(xb, wg_ref[0], preferred_element_type=F32)
        up = jnp.dot(xb, wu_ref[0], preferred_element_type=F32)
        acc_ref[...] += jnp.dot(_silu_mul(gate, up).astype(BF16), wd_ref[0], preferred_element_type=F32)

    @pl.when(f == nf - 1)
    def _():
        @pl.when(active)
        def _():
            ybuf[slot] = acc_ref[...]

        @pl.when(jnp.logical_not(active))
        def _():
            ybuf[slot] = jnp.zeros(ybuf.shape[1:], ybuf.dtype)

        scatter_start(slot)

        @pl.when(j >= 1)
        def _():
            scatter_wait(1 - slot)

        @pl.when(j == pl.num_programs(0) - 1)
        def _():
            scatter_wait(slot)


def _moe_experts(x, expert_idx, w_gu, w_down):
    m, d = x.shape
    ne, ff = w_down.shape[0], w_down.shape[1]
    nslots = 2 * m
    tm = _pick(nslots, (512, 256, 128))
    tf = _pick(ff, (896, 512, 256, 128))
    nf = ff // tf
    n_tiles = nslots // tm + ne
    p_rows = n_tiles * tm

    e_flat = expert_idx.reshape(nslots)
    onehot = (e_flat[:, None] == jnp.arange(ne, dtype=jnp.int32)[None, :]).astype(jnp.int32)
    csum = jnp.cumsum(onehot, axis=0)
    rank = jnp.take_along_axis(csum, e_flat[:, None], axis=1)[:, 0] - 1
    counts = csum[-1]
    gsize = ((counts + tm - 1) // tm) * tm
    gend = jnp.cumsum(gsize)
    dest = (gend - gsize)[e_flat] + rank
    slot_ids = jnp.arange(nslots, dtype=jnp.int32)
    slot_sorted = jnp.full((p_rows,), -1, jnp.int32).at[dest].set(slot_ids)
    is_fill = slot_sorted < 0
    fill_rank = jnp.cumsum(is_fill.astype(jnp.int32)) - 1
    src_sorted = jnp.where(is_fill, 0, slot_sorted // 2)
    dst_sorted = jnp.where(is_fill, nslots + fill_rank, (slot_sorted % 2) * m + slot_sorted // 2)
    tile_e = jnp.minimum(jnp.searchsorted(gend, jnp.arange(n_tiles, dtype=jnp.int32) * tm, side="right"),
                         ne - 1).astype(jnp.int32)
    nact = (gend[-1:] // tm).astype(jnp.int32)
    src3 = src_sorted.reshape(n_tiles, 1, tm)
    dst3 = dst_sorted.reshape(n_tiles, 1, tm)

    def f_eff(j, f, nact_ref):
        return jnp.where(j < nact_ref[0], f, nf - 1)

    smem_blk = lambda imap: pl.BlockSpec((1, 1, tm), imap, memory_space=pltpu.SMEM)
    grid_spec = pltpu.PrefetchScalarGridSpec(
        num_scalar_prefetch=2,
        grid=(n_tiles, nf),
        in_specs=[smem_blk(lambda j, f, te, na: (j, 0, 0)),
                  smem_blk(lambda j, f, te, na: (jnp.minimum(j + 1, n_tiles - 1), 0, 0)),
                  smem_blk(lambda j, f, te, na: (j, 0, 0)),
                  pl.BlockSpec(memory_space=pl.ANY),
                  pl.BlockSpec((1, d, tf), lambda j, f, te, na: (te[j], 0, f_eff(j, f, na))),
                  pl.BlockSpec((1, d, tf), lambda j, f, te, na: (te[j], 0, f_eff(j, f, na) + nf)),
                  pl.BlockSpec((1, tf, d), lambda j, f, te, na: (te[j], f_eff(j, f, na), 0))],
        out_specs=pl.BlockSpec(memory_space=pl.ANY),
        scratch_shapes=[pltpu.VMEM((2, tm, d), F32), pltpu.VMEM((tm, d), BF16), pltpu.VMEM((tm, d), F32),
                        pltpu.VMEM((2, tm, d), F32), pltpu.SemaphoreType.DMA((2,)),
                        pltpu.SemaphoreType.DMA((2,))],
    )
    return pl.pallas_call(
        functools.partial(_moe_kernel, tm=tm),
        grid_spec=grid_spec,
        out_shape=jax.ShapeDtypeStruct((p_rows, d), F32),
        compiler_params=_params("arbitrary", "arbitrary"),
        name="moe_grouped_swiglu",
    )(tile_e, nact, src3, src3, dst3, x, w_gu, w_gu, w_down)


def _combine_kernel(x_ref, y1_ref, y2_ref, ro_ref, g_ref, b_ref, o_ref):
    ro = ro_ref[...]
    lane = lax.broadcasted_iota(jnp.int32, ro.shape, 1)
    w1 = jnp.sum(jnp.where(lane == 2, ro, 0.0), axis=-1, keepdims=True)
    w2 = jnp.sum(jnp.where(lane == 3, ro, 0.0), axis=-1, keepdims=True)
    z = ALPHA * x_ref[...] + w1 * y1_ref[...] + w2 * y2_ref[...]
    o_ref[0] = _layer_norm(z, g_ref[...], b_ref[...])


def _moe_combine(x, y_slots, router_out, g, b, nbatch, lp):
    m, d = x.shape
    tm = BLOCK
    nb = lp // tm
    choice2 = m // tm
    row = lambda i, j: (i * nb + j + 1, 0)
    fixed = lambda i, j: (0, 0)
    return pl.pallas_call(
        _combine_kernel,
        grid=(nbatch, nb - 1),
        in_specs=[pl.BlockSpec((tm, d), row), pl.BlockSpec((tm, d), row),
                  pl.BlockSpec((tm, d), lambda i, j: (choice2 + i * nb + j + 1, 0)),
                  pl.BlockSpec((tm, LANES), row),
                  pl.BlockSpec((1, d), fixed), pl.BlockSpec((1, d), fixed)],
        out_specs=pl.BlockSpec((1, tm, d), lambda i, j: (i, j, 0)),
        out_shape=jax.ShapeDtypeStruct((nbatch, lp - tm, d), F32),
        compiler_params=_params("parallel", "parallel"),
        name="moe_combine_ln",
    )(x, y_slots, y_slots, router_out, g, b)


def _rwkv_proj_kernel(x_ref, xp_ref, mu_ref, wr_ref, wk_ref, wv_ref, w0_ref, w1_ref, w2_ref,
                      a0_ref, a1_ref, a2_ref, g1_ref, g2_ref,
                      r_ref, k_ref, v_ref, a_ref, ld_ref, g_ref, *, lp, nbatch):
    x = x_ref[...]
    tm = x.shape[0]
    rolled = pltpu.roll(x, 1, axis=0)
    prev_row = xp_ref[7:8, :]
    first = lax.broadcasted_iota(jnp.int32, (tm, 1), 0) == 0
    xx = jnp.where(first, prev_row, rolled) - x
    xx = jnp.where(_pad_row_mask(pl.program_id(0) * tm, tm, lp, nbatch), 0.0, xx)

    def mix(i):
        return x + xx * mu_ref[i:i + 1, :]

    r_ref[...] = _dot(mix(0), wr_ref[...])
    k_ref[...] = _dot(mix(2), wk_ref[...])
    v_ref[...] = _dot(mix(3), wv_ref[...])
    wl = w0_ref[...] + _dot(jnp.tanh(_dot(mix(1), w1_ref[...])), w2_ref[...])
    z = -wl
    softplus = jnp.maximum(z, 0.0) + jnp.log(1.0 + jnp.exp(-jnp.abs(z)))
    ld_ref[...] = -jnp.exp(-softplus - 0.5)
    al = a0_ref[...] + _dot(_dot(mix(4), a1_ref[...]), a2_ref[...])
    a_ref[...] = 1.0 / (1.0 + jnp.exp(-al))
    gl = _dot(mix(5), g1_ref[...])
    g_ref[...] = _dot(1.0 / (1.0 + jnp.exp(-gl)), g2_ref[...])


def _rwkv_proj(x, mu, w_rkv, w0, w1, w2, a0, a1, a2, g1, g2, lp, nbatch):
    m, d = x.shape
    tm = _pick(m, (384, 256, 128))
    row = lambda i: (i, 0)
    fixed = lambda i: (0, 0)
    full = lambda arr: pl.BlockSpec(arr.shape, fixed)
    out = jax.ShapeDtypeStruct((m, d), F32)
    return pl.pallas_call(
        functools.partial(_rwkv_proj_kernel, lp=lp, nbatch=nbatch),
        grid=(m // tm,),
        in_specs=[pl.BlockSpec((tm, d), row),
                  pl.BlockSpec((8, d), lambda i: (jnp.maximum(i * (tm // 8) - 1, 0), 0)),
                  full(mu), full(w_rkv[0]), full(w_rkv[1]), full(w_rkv[2]),
                  full(w0), full(w1), full(w2), full(a0), full(a1), full(a2), full(g1), full(g2)],
        out_specs=[pl.BlockSpec((tm, d), row)] * 6,
        out_shape=[out] * 6,
        compiler_params=_params("parallel"),
        name="rwkv_proj",
    )(x, x, mu, w_rkv[0], w_rkv[1], w_rkv[2], w0, w1, w2, a0, a1, a2, g1, g2)


def _bmm(a, b):
    return jnp.einsum("cij,cjk->cik", a.astype(BF16), b.astype(BF16), preferred_element_type=F32)


def _bmm_nt(a, b):
    return jnp.einsum("cik,cjk->cij", a.astype(BF16), b.astype(BF16), preferred_element_type=F32)


def _bmm_tn(a, b):
    return jnp.einsum("cki,ckj->cij", a.astype(BF16), b.astype(BF16), preferred_element_type=F32)


def _wkv_head_pair(r, k, v, a, ld, k_k, k_a, r_k, gn_g, gn_b, z):
    c = WKV_CHUNK
    n = RWKV_HEAD
    tb = r.shape[0]
    nc = tb // c
    head0 = lax.broadcasted_iota(jnp.int32, (1, LANES), 1) < n

    def head_sum(x):
        s0 = jnp.sum(jnp.where(head0, x, 0.0), axis=-1, keepdims=True)
        s1 = jnp.sum(jnp.where(head0, 0.0, x), axis=-1, keepdims=True)
        return jnp.where(head0, s0, s1)

    def stack(x):
        xb = x.astype(BF16)
        zero = jnp.zeros_like(xb)
        return jnp.concatenate([jnp.where(head0, xb, zero), jnp.where(head0, zero, xb)], axis=1)

    def fold(x):
        return x[:, :c] + x[:, c:]

    kkr = k * k_k
    kk = kkr / jnp.maximum(jnp.sqrt(head_sum(kkr * kkr)), 1e-12)
    km = k * (1.0 + (a - 1.0) * k_a)
    bv = kk * a

    tpos = jnp.bitwise_and(lax.broadcasted_iota(jnp.int32, (tb, 1), 0), c - 1)
    cs = ld
    shift = 1
    while shift < c:
        cs = cs + jnp.where(tpos >= shift, pltpu.roll(cs, shift, axis=0), 0.0)
        shift *= 2

    to3 = lambda x: x.reshape(nc, c, LANES)
    cs3, ld3 = to3(cs), to3(ld)
    cs_last = cs3[:, c - 1:c, :]
    e_neg = jnp.exp(-cs3)
    e_tail = jnp.exp(cs_last - cs3)
    at_s = stack(-to3(kk) * jnp.exp(cs3 - ld3))
    rt = to3(r) * jnp.exp(cs3)
    rt_s = stack(rt)
    bt_s = stack(to3(bv) * e_neg)
    kt_s = stack(to3(km) * e_neg)
    bh_s = stack(to3(bv) * e_tail)
    kh_s = stack(to3(km) * e_tail)
    v_s = stack(to3(v))
    dec = jnp.exp(cs_last)

    c2 = 2 * c
    g = _bmm_nt(jnp.concatenate([at_s, rt_s], axis=1), jnp.concatenate([bt_s, kt_s], axis=1))
    ti = jnp.bitwise_and(lax.broadcasted_iota(jnp.int32, (1, c2, c2), 1), c - 1)
    si = jnp.bitwise_and(lax.broadcasted_iota(jnp.int32, (1, c2, c2), 2), c - 1)
    strict = ti > si
    incl = ti >= si
    lab = jnp.where(strict, g[:, :c2, :c2], 0.0)
    lak = jnp.where(strict, g[:, :c2, c2:], 0.0)
    mrb = jnp.where(incl, g[:, c2:, :c2], 0.0)
    mrk = jnp.where(incl, g[:, c2:, c2:], 0.0)

    eye2 = lax.broadcasted_iota(jnp.int32, (1, c2, c2), 1) == lax.broadcasted_iota(jnp.int32, (1, c2, c2), 2)
    t_inv = jnp.where(eye2, 1.0, lab)
    lpow = lab
    for _ in range(int(math.log2(c)) - 1):
        lpow = _bmm(lpow, lpow)
        t_inv = t_inv + _bmm(t_inv, lpow)

    lakv = _bmm(lak, v_s)
    wu = _bmm(t_inv, jnp.concatenate([at_s, lakv.astype(BF16)], axis=2))
    qy = _bmm(mrb, wu)
    q = rt + fold(qy[:, :, :LANES])
    yp = fold(qy[:, :, LANES:] + _bmm(mrk, v_s))
    eye_l = (lax.broadcasted_iota(jnp.int32, (1, LANES, LANES), 1)
             == lax.broadcasted_iota(jnp.int32, (1, LANES, LANES), 2))
    a_t = jnp.where(eye_l, dec, 0.0) + _bmm_tn(bh_s, wu[:, :, :LANES])
    g_t = _bmm_tn(jnp.concatenate([bh_s, kh_s], axis=1),
                  jnp.concatenate([wu[:, :, LANES:].astype(BF16), v_s], axis=1))

    ys = []
    for ci in range(nc):
        ys.append(yp[ci] + _dot(q[ci], z))
        z = _dot(a_t[ci], z) + g_t[ci]
    y = jnp.concatenate(ys, axis=0)

    mu = head_sum(y) * (1.0 / n)
    yc = y - mu
    var = head_sum(yc * yc) * (1.0 / n)
    yn = yc * lax.rsqrt(var + GN_EPS) * gn_g + gn_b
    bonus = head_sum(r * km * r_k) * v
    return yn + bonus, z


def _wkv_kernel(r_ref, k_ref, v_ref, a_ref, ld_ref, kk_ref, ka_ref, rk_ref, gg_ref, gb_ref,
                y_ref, z_ref):
    @pl.when(pl.program_id(1) == 0)
    def _():
        z_ref[...] = jnp.zeros_like(z_ref)

    for bi in range(r_ref.shape[0]):
        y, z = _wkv_head_pair(r_ref[bi], k_ref[bi], v_ref[bi], a_ref[bi], ld_ref[bi], kk_ref[...], ka_ref[...],
                              rk_ref[...], gg_ref[...], gb_ref[...], z_ref[bi])
        y_ref[bi] = y
        z_ref[bi] = z


def _wkv(r, k, v, a, ld, k_k, k_a, r_k, gn_g, gn_b):
    b, lp, d = r.shape
    tb = _pick(lp, (384, 256, 128))
    tok_spec = pl.BlockSpec((b, tb, LANES), lambda j, t: (0, t, j))
    par_spec = pl.BlockSpec((1, LANES), lambda j, t: (0, j))
    return pl.pallas_call(
        _wkv_kernel,
        grid=(d // LANES, lp // tb),
        in_specs=[tok_spec] * 5 + [par_spec] * 5,
        out_specs=tok_spec,
        out_shape=jax.ShapeDtypeStruct((b, lp, d), F32),
        scratch_shapes=[pltpu.VMEM((b, LANES, LANES), F32)],
        compiler_params=_params("parallel", "arbitrary"),
        name="wkv7_chunked",
    )(r, k, v, a, ld, k_k, k_a, r_k, gn_g, gn_b)


def _router_kernel(x_ref, w_ref, o_ref):
    logits = jnp.dot(x_ref[...], w_ref[...], preferred_element_type=F32, precision=lax.Precision.HIGHEST)
    lane = lax.broadcasted_iota(jnp.int32, logits.shape, 1).astype(F32)
    neg = -jnp.inf
    lg = jnp.where(lane < N_EXPERTS, logits, neg)
    m1 = jnp.max(lg, axis=-1, keepdims=True)
    i1 = jnp.min(jnp.where(lg == m1, lane, float(LANES)), axis=-1, keepdims=True)
    lg2 = jnp.where(lane == i1, neg, lg)
    m2 = jnp.max(lg2, axis=-1, keepdims=True)
    i2 = jnp.min(jnp.where(lg2 == m2, lane, float(LANES)), axis=-1, keepdims=True)
    e2 = jnp.exp(m2 - m1)
    den = 1.0 + e2
    o_ref[...] = jnp.where(lane == 0.0, i1, jnp.where(lane == 1.0, i2, jnp.where(
        lane == 2.0, 1.0 / den, jnp.where(lane == 3.0, e2 / den, 0.0))))


def _router(x, w_pad):
    m, d = x.shape
    tm = _pick(m, (768, 384, 256, 128))
    return pl.pallas_call(
        _router_kernel,
        grid=(m // tm,),
        in_specs=[pl.BlockSpec((tm, d), lambda i: (i, 0)), pl.BlockSpec((d, LANES), lambda i: (0, 0))],
        out_specs=pl.BlockSpec((tm, LANES), lambda i: (i, 0)),
        out_shape=jax.ShapeDtypeStruct((m, LANES), F32),
        compiler_params=_params("parallel"),
        name="moe_router",
    )(x, w_pad)


def _t5_bucket(dist):
    n = jnp.maximum(dist, 0)
    is_small = n < MAX_EXACT
    nf = jnp.maximum(n, 1).astype(F32)
    large = MAX_EXACT + (jnp.log(nf / MAX_EXACT) / math.log(MAX_DISTANCE / MAX_EXACT)
                         * (NUM_BUCKETS - MAX_EXACT)).astype(jnp.int32)
    large = jnp.minimum(large, NUM_BUCKETS - 1)
    return jnp.where(is_small, n, large)


def _bias_tables(rel_bias):
    rb = rel_bias.astype(F32)
    d_band = BLOCK + jnp.arange(BLOCK)[:, None] - jnp.arange(2 * BLOCK)[None, :]
    ok = (d_band >= 0) & (d_band < WINDOW)
    onehot = (_t5_bucket(d_band)[..., None] == jnp.arange(NUM_BUCKETS)).astype(F32)
    looked_up = jnp.einsum("qkn,nh->hqk", onehot, rb, precision=lax.Precision.HIGHEST)
    band = jnp.where(ok[None], looked_up, NEG_INF)
    meta = rb[_t5_bucket(jnp.array(WINDOW))]
    return band, meta


def kernel(x, meta_tokens, rel_bias, ln_mix_g, ln_mix_b, ln_ffn_g, ln_ffn_b, attn_w_qkv, attn_b_qkv, attn_sinks, attn_w_o, rwkv_mu, rwkv_w0, rwkv_w1, rwkv_w2, rwkv_a0, rwkv_a1, rwkv_a2, rwkv_g1, rwkv_g2, rwkv_k_k, rwkv_k_a, rwkv_r_k, rwkv_w_rkv, rwkv_lnx_g, rwkv_lnx_b, rwkv_w_o, ffn_w_gu, ffn_w_down, moe_router, moe_w_gu, moe_w_down):
    b, seq, d = x.shape
    lp = seq + BLOCK
    m = b * lp
    row = lambda t: t.reshape(1, -1).astype(F32)

    meta = jnp.broadcast_to(meta_tokens.astype(x.dtype)[None], (b, N_META, d))
    h = jnp.concatenate([jnp.zeros((b, PAD, d), x.dtype), meta, x], axis=1).reshape(m, d)

    q_dim = N_HEADS * HEAD_DIM
    e_dim = attn_w_qkv.shape[2]
    col_scale = jnp.where(jnp.arange(e_dim) < q_dim, ATTN_SCALE, 1.0).astype(F32)
    qkv = _qkv_proj(h, attn_w_qkv[0].astype(BF16), row(attn_b_qkv[0]), row(col_scale))
    bias_band, bias_meta = _bias_tables(rel_bias)
    o = _attention(qkv.reshape(b, lp, e_dim), bias_band, attn_sinks[0].astype(F32), bias_meta)
    h = _proj_res_ln(o.reshape(m, q_dim), attn_w_o[0].astype(BF16), h, row(ln_mix_g[0]), row(ln_mix_b[0]))
    h = _ffn(h, ffn_w_gu[0].astype(BF16), ffn_w_down[0].astype(BF16), row(ln_ffn_g[0]), row(ln_ffn_b[0]), lp, b)

    r, k, v, a, ld, g = _rwkv_proj(
        h, rwkv_mu[0], rwkv_w_rkv[0].astype(BF16), row(rwkv_w0[0]), rwkv_w1[0].astype(BF16),
        rwkv_w2[0].astype(BF16), row(rwkv_a0[0]), rwkv_a1[0].astype(BF16), rwkv_a2[0].astype(BF16),
        rwkv_g1[0].astype(BF16), rwkv_g2[0].astype(BF16), lp, b)
    t3 = lambda t: t.reshape(b, lp, d)
    y = _wkv(t3(r), t3(k), t3(v), t3(a), t3(ld), row(rwkv_k_k[0]), row(rwkv_k_a[0]), row(rwkv_r_k[0]),
             row(rwkv_lnx_g[0]), row(rwkv_lnx_b[0]))
    h = _proj_res_ln(y.reshape(m, d), rwkv_w_o[0].astype(BF16), h, row(ln_mix_g[1]), row(ln_mix_b[1]), mult=g)
    w_router = jnp.pad(moe_router[0].astype(F32), ((0, 0), (0, LANES - N_EXPERTS)))
    routed = _router(h, w_router)
    expert_idx = routed[:, :2].astype(jnp.int32)
    y_slots = _moe_experts(h, expert_idx, moe_w_gu[0].astype(BF16), moe_w_down[0].astype(BF16))
    return _moe_combine(h, y_slots, routed, row(ln_ffn_g[1]), row(ln_ffn_b[1]), b, lp)
```

```python
import functools
import math

import jax
import jax.numpy as jnp
from jax import lax
from jax.experimental import pallas as pl
from jax.experimental.pallas import tpu as pltpu

F32 = jnp.float32
BF16 = jnp.bfloat16

N_META = 16
N_HEADS = 16
N_KV_HEADS = 4
HEAD_DIM = 64
GROUP = N_HEADS // N_KV_HEADS
WINDOW = 128
BLOCK = 128
PAD = BLOCK - N_META
ATTN_SCALE = 1.0 / math.sqrt(HEAD_DIM)
NEG_INF = -1e30
NUM_BUCKETS = 32
MAX_EXACT = NUM_BUCKETS // 2
MAX_DISTANCE = 128
RWKV_HEAD = 64
GN_EPS = 64e-5
N_EXPERTS = 8
DEPTH = 2
ALPHA = (2 * DEPTH) ** 0.25
LN_EPS = 1e-5

WKV_CHUNK = 64
LANES = 128
DMA_ISSUE_UNROLL = 8
VMEM_LIMIT_BYTES = 56 * 1024 * 1024


def _params(*sem):
    return pltpu.CompilerParams(dimension_semantics=sem, vmem_limit_bytes=VMEM_LIMIT_BYTES)


def _pick(n, candidates):
    for c in candidates:
        if n % c == 0:
            return c
    raise ValueError(f"no tile in {candidates} divides {n}")


def _pad_row_mask(first_row, rows, lp, nbatch):
    r = first_row + lax.broadcasted_iota(jnp.int32, (rows, 1), 0)
    mask = r < PAD
    for bi in range(1, nbatch):
        mask = jnp.logical_or(mask, jnp.logical_and(r >= bi * lp, r < bi * lp + PAD))
    return mask


def _layer_norm(z, g, b):
    mu = jnp.mean(z, axis=-1, keepdims=True)
    zc = z - mu
    var = jnp.mean(zc * zc, axis=-1, keepdims=True)
    return zc * lax.rsqrt(var + LN_EPS) * g + b


def _dot(a, b):
    return jnp.dot(a.astype(BF16), b.astype(BF16), preferred_element_type=F32)


def _dot_nt(a, b):
    return lax.dot_general(a.astype(BF16), b.astype(BF16), (((1,), (1,)), ((), ())),
                           preferred_element_type=F32)


def _qkv_kernel(x_ref, w_ref, b_ref, s_ref, o_ref):
    acc = _dot(x_ref[...], w_ref[...])
    o_ref[...] = ((acc + b_ref[...]) * s_ref[...]).astype(o_ref.dtype)


def _qkv_proj(x, w, b, s):
    m, d = x.shape
    n = w.shape[1]
    tm = _pick(m, (768, 384, 256, 128))
    return pl.pallas_call(
        _qkv_kernel,
        grid=(m // tm,),
        in_specs=[pl.BlockSpec((tm, d), lambda i: (i, 0)),
                  pl.BlockSpec((d, n), lambda i: (0, 0)),
                  pl.BlockSpec((1, n), lambda i: (0, 0)),
                  pl.BlockSpec((1, n), lambda i: (0, 0))],
        out_specs=pl.BlockSpec((tm, n), lambda i: (i, 0)),
        out_shape=jax.ShapeDtypeStruct((m, n), BF16),
        compiler_params=_params("parallel"),
        name="qkv_proj",
    )(x, w, b, s)


def _attn_kernel(sink_ref, mbias_ref, cur_ref, prev_ref, meta_ref, bias_ref, o_ref):
    n = pl.program_id(1)
    q_dim = N_HEADS * HEAD_DIM
    kv_dim = N_KV_HEADS * HEAD_DIM
    cur = cur_ref[0]
    prev = prev_ref[0]
    meta = meta_ref[0][PAD:, :]

    col = lax.broadcasted_iota(jnp.int32, (1, 2 * BLOCK), 1)
    key_ok = (col + (n - 1) * BLOCK) >= PAD
    qrow = jnp.bitwise_and(lax.broadcasted_iota(jnp.int32, (GROUP * BLOCK, N_META), 0), BLOCK - 1)
    midx = lax.broadcasted_iota(jnp.int32, (GROUP * BLOCK, N_META), 1)
    meta_ok = (n * BLOCK + qrow - PAD - midx) >= WINDOW
    grp = jnp.right_shift(lax.broadcasted_iota(jnp.int32, (GROUP * BLOCK, 1), 0), int(math.log2(BLOCK)))

    for h in range(N_KV_HEADS):
        ks = slice(h * HEAD_DIM, (h + 1) * HEAD_DIM)
        vs = slice(kv_dim + h * HEAD_DIM, kv_dim + (h + 1) * HEAD_DIM)
        kb = jnp.concatenate([prev[:, ks], cur[:, q_dim + h * HEAD_DIM:q_dim + (h + 1) * HEAD_DIM]], axis=0)
        vb = jnp.concatenate([prev[:, vs], cur[:, q_dim + kv_dim + h * HEAD_DIM:
                                               q_dim + kv_dim + (h + 1) * HEAD_DIM]], axis=0)
        km = meta[:, ks]
        vm = meta[:, vs]
        q4 = jnp.concatenate(
            [cur[:, (h * GROUP + g) * HEAD_DIM:(h * GROUP + g + 1) * HEAD_DIM] for g in range(GROUP)], axis=0)

        sink_col = jnp.zeros((GROUP * BLOCK, 1), F32)
        mb_col = jnp.zeros((GROUP * BLOCK, 1), F32)
        for g in range(GROUP):
            sink_col = jnp.where(grp == g, sink_ref[h * GROUP + g], sink_col)
            mb_col = jnp.where(grp == g, mbias_ref[h * GROUP + g], mb_col)

        bias4 = bias_ref[h * GROUP:(h + 1) * GROUP].reshape(GROUP * BLOCK, 2 * BLOCK)
        s = jnp.where(key_ok, _dot_nt(q4, kb) + bias4, NEG_INF)
        sm = jnp.where(meta_ok, _dot_nt(q4, km) + mb_col, NEG_INF)
        mx = jnp.maximum(jnp.maximum(jnp.max(s, axis=-1, keepdims=True),
                                     jnp.max(sm, axis=-1, keepdims=True)), sink_col)
        p = jnp.exp(s - mx)
        pm = jnp.exp(sm - mx)
        denom = (jnp.sum(p, axis=-1, keepdims=True) + jnp.sum(pm, axis=-1, keepdims=True)
                 + jnp.exp(sink_col - mx))
        o4 = (_dot(p, vb) + _dot(pm, vm)) / denom
        for g in range(GROUP):
            hd = h * GROUP + g
            o_ref[0, :, hd * HEAD_DIM:(hd + 1) * HEAD_DIM] = o4[g * BLOCK:(g + 1) * BLOCK].astype(o_ref.dtype)


def _attention(qkv, bias_tbl, sinks, meta_bias):
    b, lp, e = qkv.shape
    nb = lp // BLOCK
    q_dim = N_HEADS * HEAD_DIM
    kv2 = 2 * N_KV_HEADS * HEAD_DIM
    kv_blk = q_dim // kv2
    smem = pl.BlockSpec(memory_space=pltpu.SMEM)
    return pl.pallas_call(
        _attn_kernel,
        grid=(b, nb),
        in_specs=[smem, smem,
                  pl.BlockSpec((1, BLOCK, e), lambda i, j: (i, j, 0)),
                  pl.BlockSpec((1, BLOCK, kv2), lambda i, j: (i, jnp.maximum(j - 1, 0), kv_blk)),
                  pl.BlockSpec((1, BLOCK, kv2), lambda i, j: (i, 0, kv_blk)),
                  pl.BlockSpec((N_HEADS, BLOCK, 2 * BLOCK), lambda i, j: (0, 0, 0))],
        out_specs=pl.BlockSpec((1, BLOCK, q_dim), lambda i, j: (i, j, 0)),
        out_shape=jax.ShapeDtypeStruct((b, lp, q_dim), BF16),
        compiler_params=_params("parallel", "parallel"),
        name="swa_attention",
    )(sinks, meta_bias, qkv, qkv, qkv, bias_tbl)


def _proj_ln_kernel(*refs, has_mult):
    if has_mult:
        a_ref, m_ref, w_ref, res_ref, g_ref, b_ref, o_ref = refs
        a = a_ref[...].astype(F32) * m_ref[...].astype(F32)
    else:
        a_ref, w_ref, res_ref, g_ref, b_ref, o_ref = refs
        a = a_ref[...]
    y = _dot(a, w_ref[...])
    o_ref[...] = _layer_norm(ALPHA * res_ref[...] + y, g_ref[...], b_ref[...])


def _proj_res_ln(a, w, res, g, b, mult=None):
    m, k = a.shape
    d = w.shape[1]
    tm = _pick(m, (768, 384, 256, 128))
    row = lambda i: (i, 0)
    fixed = lambda i: (0, 0)
    ins = [a] + ([mult] if mult is not None else []) + [w, res, g, b]
    specs = ([pl.BlockSpec((tm, k), row)] + ([pl.BlockSpec((tm, k), row)] if mult is not None else [])
             + [pl.BlockSpec((k, d), fixed), pl.BlockSpec((tm, d), row),
                pl.BlockSpec((1, d), fixed), pl.BlockSpec((1, d), fixed)])
    return pl.pallas_call(
        functools.partial(_proj_ln_kernel, has_mult=mult is not None),
        grid=(m // tm,),
        in_specs=specs,
        out_specs=pl.BlockSpec((tm, d), row),
        out_shape=jax.ShapeDtypeStruct((m, d), F32),
        compiler_params=_params("parallel"),
        name="proj_res_ln",
    )(*ins)


def _silu_mul(gate, up):
    return gate * (1.0 / (1.0 + jnp.exp(-gate))) * up


def _ffn_kernel(x_ref, wg_ref, wu_ref, wd_ref, g_ref, b_ref, o_ref, acc_ref, xb_ref, *, lp, nbatch, tm):
    i = pl.program_id(0)
    f = pl.program_id(1)

    @pl.when(f == 0)
    def _():
        acc_ref[...] = jnp.zeros_like(acc_ref)
        xb_ref[...] = x_ref[...].astype(BF16)

    xb = xb_ref[...]
    gate = jnp.dot(xb, wg_ref[...], preferred_element_type=F32)
    up = jnp.dot(xb, wu_ref[...], preferred_element_type=F32)
    acc_ref[...] += jnp.dot(_silu_mul(gate, up).astype(BF16), wd_ref[...], preferred_element_type=F32)

    @pl.when(f == pl.num_programs(1) - 1)
    def _():
        y = _layer_norm(ALPHA * x_ref[...] + acc_ref[...], g_ref[...], b_ref[...])
        o_ref[...] = jnp.where(_pad_row_mask(i * tm, tm, lp, nbatch), 0.0, y)


def _ffn(x, w_gu, w_down, g, b, lp, nbatch):
    m, d = x.shape
    ff = w_down.shape[0]
    tm = _pick(m, (384, 256, 128))
    tf = _pick(ff, (1408, 896, 512, 256, 128))
    nf = ff // tf
    row = lambda i, f: (i, 0)
    fixed = lambda i, f: (0, 0)
    return pl.pallas_call(
        functools.partial(_ffn_kernel, lp=lp, nbatch=nbatch, tm=tm),
        grid=(m // tm, nf),
        in_specs=[pl.BlockSpec((tm, d), row),
                  pl.BlockSpec((d, tf), lambda i, f: (0, f)),
                  pl.BlockSpec((d, tf), lambda i, f: (0, f + nf)),
                  pl.BlockSpec((tf, d), lambda i, f: (f, 0)),
                  pl.BlockSpec((1, d), fixed), pl.BlockSpec((1, d), fixed)],
        out_specs=pl.BlockSpec((tm, d), row),
        out_shape=jax.ShapeDtypeStruct((m, d), F32),
        scratch_shapes=[pltpu.VMEM((tm, d), F32), pltpu.VMEM((tm, d), BF16)],
        compiler_params=_params("parallel", "arbitrary"),
        name="swiglu_res_ln",
    )(x, w_gu, w_gu, w_down, g, b)


def _moe_kernel(tile_e_ref, nact_ref, src_ref, src_next_ref, dst_ref, dst_prev_ref, x_hbm, wg_ref, wu_ref,
                wd_ref, y_hbm, xbuf, xb_ref, acc_ref, ybuf, gsem, ssem, *, tm, nf):
    j = pl.program_id(0)
    f = pl.program_id(1)
    last_tile = j == pl.num_programs(0) - 1
    active = j < nact_ref[0]
    slot = lax.rem(j, 2)
    other = 1 - slot
    rows = tm // nf

    def gather_row(idx_ref, s, i):
        t = idx_ref[0, 0, i]
        pltpu.make_async_copy(x_hbm.at[pl.ds(t, 1)], xbuf.at[s, pl.ds(i, 1)], gsem.at[s]).start()

    def scatter_row(idx_ref, s, i):
        t = idx_ref[0, 0, i]
        pltpu.make_async_copy(ybuf.at[s, pl.ds(i, 1)], y_hbm.at[pl.ds(t, 1)], ssem.at[s]).start()

    def gather_wait(s):
        pltpu.make_async_copy(x_hbm.at[pl.ds(0, tm)], xbuf.at[s], gsem.at[s]).wait()

    def scatter_wait(s):
        pltpu.make_async_copy(ybuf.at[s], y_hbm.at[pl.ds(0, tm)], ssem.at[s]).wait()

    def issue_step_dmas():
        for i in range(rows):
            gather_row(src_next_ref, other, f * rows + i)
            scatter_row(dst_prev_ref, other, f * rows + i)

    @pl.when(f == 0)
    def _():
        @pl.when(j == 0)
        def _():
            ybuf[...] = jnp.zeros_like(ybuf)

            def body(i, carry):
                gather_row(src_ref, 0, i)
                return carry
            lax.fori_loop(0, tm, body, 0, unroll=DMA_ISSUE_UNROLL)

        gather_wait(slot)
        xb_ref[...] = xbuf[slot].astype(BF16)
        acc_ref[...] = jnp.zeros_like(acc_ref)

    @pl.when(active)
    def _():
        issue_step_dmas()
        xb = xb_ref[...]
        gate = jnp.dot(xb, wg_ref[0], preferred_element_type=F32)
        up = jnp.dot(xb, wu_ref[0], preferred_element_type=F32)
        acc_ref[...] += jnp.dot(_silu_mul(gate, up).astype(BF16), wd_ref[0], preferred_element_type=F32)

    @pl.when(jnp.logical_not(active))
    def _():
        issue_step_dmas()

    @pl.when(f == nf - 1)
    def _():
        @pl.when(j >= 1)
        def _():
            scatter_wait(slot)

        ybuf[slot] = acc_ref[...]

        @pl.when(last_tile)
        def _():
            def body(i, carry):
                scatter_row(dst_ref, slot, i)
                return carry
            lax.fori_loop(0, tm, body, 0, unroll=DMA_ISSUE_UNROLL)
            scatter_wait(slot)
            scatter_wait(other)
            gather_wait(other)


def _moe_experts(x, expert_idx, w_gu, w_down):
    m, d = x.shape
    ne, ff = w_down.shape[0], w_down.shape[1]
    nslots = 2 * m
    tm = _pick(nslots, (512, 256, 128))
    tf = _pick(ff, (1792, 896, 512, 256, 128))
    nf = ff // tf
    n_tiles = nslots // tm + ne
    p_rows = n_tiles * tm

    e_flat = expert_idx.reshape(nslots)
    onehot = (e_flat[:, None] == jnp.arange(ne, dtype=jnp.int32)[None, :]).astype(jnp.int32)
    csum = jnp.cumsum(onehot, axis=0)
    rank = jnp.take_along_axis(csum, e_flat[:, None], axis=1)[:, 0] - 1
    counts = csum[-1]
    gsize = ((counts + tm - 1) // tm) * tm
    gend = jnp.cumsum(gsize)
    dest = (gend - gsize)[e_flat] + rank
    slot_ids = jnp.arange(nslots, dtype=jnp.int32)
    slot_sorted = jnp.full((p_rows,), -1, jnp.int32).at[dest].set(slot_ids)
    is_fill = slot_sorted < 0
    fill_rank = jnp.cumsum(is_fill.astype(jnp.int32)) - 1
    src_sorted = jnp.where(is_fill, 0, slot_sorted // 2)
    dst_sorted = jnp.where(is_fill, nslots + fill_rank, (slot_sorted % 2) * m + slot_sorted // 2)
    tile_e = jnp.minimum(jnp.searchsorted(gend, jnp.arange(n_tiles, dtype=jnp.int32) * tm, side="right"),
                         ne - 1).astype(jnp.int32)
    nact = (gend[-1:] // tm).astype(jnp.int32)
    src3 = src_sorted.reshape(n_tiles, 1, tm)
    dst3 = dst_sorted.reshape(n_tiles, 1, tm)

    def f_eff(j, f, nact_ref):
        return jnp.where(j < nact_ref[0], f, nf - 1)

    smem_blk = lambda imap: pl.BlockSpec((1, 1, tm), imap, memory_space=pltpu.SMEM)
    grid_spec = pltpu.PrefetchScalarGridSpec(
        num_scalar_prefetch=2,
        grid=(n_tiles, nf),
        in_specs=[smem_blk(lambda j, f, te, na: (j, 0, 0)),
                  smem_blk(lambda j, f, te, na: (jnp.minimum(j + 1, n_tiles - 1), 0, 0)),
                  smem_blk(lambda j, f, te, na: (j, 0, 0)),
                  smem_blk(lambda j, f, te, na: (jnp.where(j == 0, n_tiles - 1, j - 1), 0, 0)),
                  pl.BlockSpec(memory_space=pl.ANY),
                  pl.BlockSpec((1, d, tf), lambda j, f, te, na: (te[j], 0, f_eff(j, f, na))),
                  pl.BlockSpec((1, d, tf), lambda j, f, te, na: (te[j], 0, f_eff(j, f, na) + nf)),
                  pl.BlockSpec((1, tf, d), lambda j, f, te, na: (te[j], f_eff(j, f, na), 0))],
        out_specs=pl.BlockSpec(memory_space=pl.ANY),
        scratch_shapes=[pltpu.VMEM((2, tm, d), F32), pltpu.VMEM((tm, d), BF16), pltpu.VMEM((tm, d), F32),
                        pltpu.VMEM((2, tm, d), F32), pltpu.SemaphoreType.DMA((2,)),
                        pltpu.SemaphoreType.DMA((2,))],
    )
    return pl.pallas_call(
        functools.partial(_moe_kernel, tm=tm, nf=nf),
        grid_spec=grid_spec,
        out_shape=jax.ShapeDtypeStruct((p_rows, d), F32),
        compiler_params=_params("arbitrary", "arbitrary"),
        name="moe_grouped_swiglu",
    )(tile_e, nact, src3, src3, dst3, dst3, x, w_gu, w_gu, w_down)


def _combine_kernel(x_ref, y1_ref, y2_ref, ro_ref, g_ref, b_ref, o_ref):
    ro = ro_ref[...]
    lane = lax.broadcasted_iota(jnp.int32, ro.shape, 1)
    w1 = jnp.sum(jnp.where(lane == 2, ro, 0.0), axis=-1, keepdims=True)
    w2 = jnp.sum(jnp.where(lane == 3, ro, 0.0), axis=-1, keepdims=True)
    z = ALPHA * x_ref[...] + w1 * y1_ref[...] + w2 * y2_ref[...]
    o_ref[0] = _layer_norm(z, g_ref[...], b_ref[...])


def _moe_combine(x, y_slots, router_out, g, b, nbatch, lp):
    m, d = x.shape
    tm = BLOCK
    nb = lp // tm
    choice2 = m // tm
    row = lambda i, j: (i * nb + j + 1, 0)
    fixed = lambda i, j: (0, 0)
    return pl.pallas_call(
        _combine_kernel,
        grid=(nbatch, nb - 1),
        in_specs=[pl.BlockSpec((tm, d), row), pl.BlockSpec((tm, d), row),
                  pl.BlockSpec((tm, d), lambda i, j: (choice2 + i * nb + j + 1, 0)),
                  pl.BlockSpec((tm, LANES), row),
                  pl.BlockSpec((1, d), fixed), pl.BlockSpec((1, d), fixed)],
        out_specs=pl.BlockSpec((1, tm, d), lambda i, j: (i, j, 0)),
        out_shape=jax.ShapeDtypeStruct((nbatch, lp - tm, d), F32),
        compiler_params=_params("parallel", "parallel"),
        name="moe_combine_ln",
    )(x, y_slots, y_slots, router_out, g, b)


def _rwkv_proj_kernel(x_ref, xp_ref, mu_ref, wr_ref, wk_ref, wv_ref, w0_ref, w1_ref, w2_ref,
                      a0_ref, a1_ref, a2_ref, g1_ref, g2_ref,
                      r_ref, k_ref, v_ref, a_ref, ld_ref, g_ref, *, lp, nbatch):
    x = x_ref[...]
    tm = x.shape[0]
    rolled = pltpu.roll(x, 1, axis=0)
    prev_row = xp_ref[7:8, :]
    first = lax.broadcasted_iota(jnp.int32, (tm, 1), 0) == 0
    xx = jnp.where(first, prev_row, rolled) - x
    xx = jnp.where(_pad_row_mask(pl.program_id(0) * tm, tm, lp, nbatch), 0.0, xx)

    def mix(i):
        return x + xx * mu_ref[i:i + 1, :]

    r_ref[...] = _dot(mix(0), wr_ref[...])
    k_ref[...] = _dot(mix(2), wk_ref[...])
    v_ref[...] = _dot(mix(3), wv_ref[...])
    wl = w0_ref[...] + _dot(jnp.tanh(_dot(mix(1), w1_ref[...])), w2_ref[...])
    z = -wl
    softplus = jnp.maximum(z, 0.0) + jnp.log(1.0 + jnp.exp(-jnp.abs(z)))
    ld_ref[...] = -jnp.exp(-softplus - 0.5)
    al = a0_ref[...] + _dot(_dot(mix(4), a1_ref[...]), a2_ref[...])
    a_ref[...] = 1.0 / (1.0 + jnp.exp(-al))
    gl = _dot(mix(5), g1_ref[...])
    g_ref[...] = _dot(1.0 / (1.0 + jnp.exp(-gl)), g2_ref[...])


def _rwkv_proj(x, mu, w_rkv, w0, w1, w2, a0, a1, a2, g1, g2, lp, nbatch):
    m, d = x.shape
    tm = _pick(m, (384, 256, 128))
    row = lambda i: (i, 0)
    fixed = lambda i: (0, 0)
    full = lambda arr: pl.BlockSpec(arr.shape, fixed)
    out = jax.ShapeDtypeStruct((m, d), F32)
    return pl.pallas_call(
        functools.partial(_rwkv_proj_kernel, lp=lp, nbatch=nbatch),
        grid=(m // tm,),
        in_specs=[pl.BlockSpec((tm, d), row),
                  pl.BlockSpec((8, d), lambda i: (jnp.maximum(i * (tm // 8) - 1, 0), 0)),
                  full(mu), full(w_rkv[0]), full(w_rkv[1]), full(w_rkv[2]),
                  full(w0), full(w1), full(w2), full(a0), full(a1), full(a2), full(g1), full(g2)],
        out_specs=[pl.BlockSpec((tm, d), row)] * 6,
        out_shape=[out] * 6,
        compiler_params=_params("parallel"),
        name="rwkv_proj",
    )(x, x, mu, w_rkv[0], w_rkv[1], w_rkv[2], w0, w1, w2, a0, a1, a2, g1, g2)


def _bmm(a, b):
    return jnp.einsum("cij,cjk->cik", a.astype(BF16), b.astype(BF16), preferred_element_type=F32)


def _bmm_nt(a, b):
    return jnp.einsum("cik,cjk->cij", a.astype(BF16), b.astype(BF16), preferred_element_type=F32)


def _bmm_tn(a, b):
    return jnp.einsum("cki,ckj->cij", a.astype(BF16), b.astype(BF16), preferred_element_type=F32)


def _wkv_head_pair(r, k, v, a, ld, k_k, k_a, r_k, gn_g, gn_b, z):
    c = WKV_CHUNK
    n = RWKV_HEAD
    tb = r.shape[0]
    nc = tb // c
    head0 = lax.broadcasted_iota(jnp.int32, (1, LANES), 1) < n

    def head_sum(x):
        s0 = jnp.sum(jnp.where(head0, x, 0.0), axis=-1, keepdims=True)
        s1 = jnp.sum(jnp.where(head0, 0.0, x), axis=-1, keepdims=True)
        return jnp.where(head0, s0, s1)

    def stack(x):
        xb = x.astype(BF16)
        zero = jnp.zeros_like(xb)
        return jnp.concatenate([jnp.where(head0, xb, zero), jnp.where(head0, zero, xb)], axis=1)

    def fold(x):
        return x[:, :c] + x[:, c:]

    kkr = k * k_k
    kk = kkr / jnp.maximum(jnp.sqrt(head_sum(kkr * kkr)), 1e-12)
    km = k * (1.0 + (a - 1.0) * k_a)
    bv = kk * a

    tpos = jnp.bitwise_and(lax.broadcasted_iota(jnp.int32, (tb, 1), 0), c - 1)
    cs = ld
    shift = 1
    while shift < c:
        cs = cs + jnp.where(tpos >= shift, pltpu.roll(cs, shift, axis=0), 0.0)
        shift *= 2

    to3 = lambda x: x.reshape(nc, c, LANES)
    cs3, ld3 = to3(cs), to3(ld)
    cs_last = cs3[:, c - 1:c, :]
    e_neg = jnp.exp(-cs3)
    e_tail = jnp.exp(cs_last - cs3)
    at_s = stack(-to3(kk) * jnp.exp(cs3 - ld3))
    rt = to3(r) * jnp.exp(cs3)
    rt_s = stack(rt)
    bt_s = stack(to3(bv) * e_neg)
    kt_s = stack(to3(km) * e_neg)
    bh_s = stack(to3(bv) * e_tail)
    kh_s = stack(to3(km) * e_tail)
    v_s = stack(to3(v))
    dec = jnp.exp(cs_last)

    c2 = 2 * c
    g = _bmm_nt(jnp.concatenate([at_s, rt_s], axis=1), jnp.concatenate([bt_s, kt_s], axis=1))
    ti = jnp.bitwise_and(lax.broadcasted_iota(jnp.int32, (1, c2, c2), 1), c - 1)
    si = jnp.bitwise_and(lax.broadcasted_iota(jnp.int32, (1, c2, c2), 2), c - 1)
    strict = ti > si
    incl = ti >= si
    lab = jnp.where(strict, g[:, :c2, :c2], 0.0)
    lak = jnp.where(strict, g[:, :c2, c2:], 0.0)
    mrb = jnp.where(incl, g[:, c2:, :c2], 0.0)
    mrk = jnp.where(incl, g[:, c2:, c2:], 0.0)

    eye2 = lax.broadcasted_iota(jnp.int32, (1, c2, c2), 1) == lax.broadcasted_iota(jnp.int32, (1, c2, c2), 2)
    t_inv = jnp.where(eye2, 1.0, lab)
    lpow = lab
    for _ in range(int(math.log2(c)) - 1):
        lpow = _bmm(lpow, lpow)
        t_inv = t_inv + _bmm(t_inv, lpow)

    lakv = _bmm(lak, v_s)
    wu = _bmm(t_inv, jnp.concatenate([at_s, lakv.astype(BF16)], axis=2))
    qy = _bmm(mrb, wu)
    q = rt + fold(qy[:, :, :LANES])
    yp = fold(qy[:, :, LANES:] + _bmm(mrk, v_s))
    eye_l = (lax.broadcasted_iota(jnp.int32, (1, LANES, LANES), 1)
             == lax.broadcasted_iota(jnp.int32, (1, LANES, LANES), 2))
    a_t = jnp.where(eye_l, dec, 0.0) + _bmm_tn(bh_s, wu[:, :, :LANES])
    g_t = _bmm_tn(jnp.concatenate([bh_s, kh_s], axis=1),
                  jnp.concatenate([wu[:, :, LANES:].astype(BF16), v_s], axis=1))

    ys = []
    for ci in range(nc):
        ys.append(yp[ci] + _dot(q[ci], z))
        z = _dot(a_t[ci], z) + g_t[ci]
    y = jnp.concatenate(ys, axis=0)

    mu = head_sum(y) * (1.0 / n)
    yc = y - mu
    var = head_sum(yc * yc) * (1.0 / n)
    yn = yc * lax.rsqrt(var + GN_EPS) * gn_g + gn_b
    bonus = head_sum(r * km * r_k) * v
    return yn + bonus, z


def _wkv_kernel(r_ref, k_ref, v_ref, a_ref, ld_ref, kk_ref, ka_ref, rk_ref, gg_ref, gb_ref,
                y_ref, z_ref):
    @pl.when(pl.program_id(1) == 0)
    def _():
        z_ref[...] = jnp.zeros_like(z_ref)

    for bi in range(r_ref.shape[0]):
        y, z = _wkv_head_pair(r_ref[bi], k_ref[bi], v_ref[bi], a_ref[bi], ld_ref[bi], kk_ref[...], ka_ref[...],
                              rk_ref[...], gg_ref[...], gb_ref[...], z_ref[bi])
        y_ref[bi] = y
        z_ref[bi] = z


def _wkv(r, k, v, a, ld, k_k, k_a, r_k, gn_g, gn_b):
    b, lp, d = r.shape
    tb = _pick(lp, (384, 256, 128))
    tok_spec = pl.BlockSpec((b, tb, LANES), lambda j, t: (0, t, j))
    par_spec = pl.BlockSpec((1, LANES), lambda j, t: (0, j))
    return pl.pallas_call(
        _wkv_kernel,
        grid=(d // LANES, lp // tb),
        in_specs=[tok_spec] * 5 + [par_spec] * 5,
        out_specs=tok_spec,
        out_shape=jax.ShapeDtypeStruct((b, lp, d), F32),
        scratch_shapes=[pltpu.VMEM((b, LANES, LANES), F32)],
        compiler_params=_params("parallel", "arbitrary"),
        name="wkv7_chunked",
    )(r, k, v, a, ld, k_k, k_a, r_k, gn_g, gn_b)


def _router_kernel(x_ref, w_ref, o_ref):
    logits = jnp.dot(x_ref[...], w_ref[...], preferred_element_type=F32, precision=lax.Precision.HIGHEST)
    lane = lax.broadcasted_iota(jnp.int32, logits.shape, 1).astype(F32)
    neg = -jnp.inf
    lg = jnp.where(lane < N_EXPERTS, logits, neg)
    m1 = jnp.max(lg, axis=-1, keepdims=True)
    i1 = jnp.min(jnp.where(lg == m1, lane, float(LANES)), axis=-1, keepdims=True)
    lg2 = jnp.where(lane == i1, neg, lg)
    m2 = jnp.max(lg2, axis=-1, keepdims=True)
    i2 = jnp.min(jnp.where(lg2 == m2, lane, float(LANES)), axis=-1, keepdims=True)
    e2 = jnp.exp(m2 - m1)
    den = 1.0 + e2
    o_ref[...] = jnp.where(lane == 0.0, i1, jnp.where(lane == 1.0, i2, jnp.where(
        lane == 2.0, 1.0 / den, jnp.where(lane == 3.0, e2 / den, 0.0))))


def _router(x, w_pad):
    m, d = x.shape
    tm = _pick(m, (768, 384, 256, 128))
    return pl.pallas_call(
        _router_kernel,
        grid=(m // tm,),
        in_specs=[pl.BlockSpec((tm, d), lambda i: (i, 0)), pl.BlockSpec((d, LANES), lambda i: (0, 0))],
        out_specs=pl.BlockSpec((tm, LANES), lambda i: (i, 0)),
        out_shape=jax.ShapeDtypeStruct((m, LANES), F32),
        compiler_params=_params("parallel"),
        name="moe_router",
    )(x, w_pad)


def _t5_bucket(dist):
    n = jnp.maximum(dist, 0)
    is_small = n < MAX_EXACT
    nf = jnp.maximum(n, 1).astype(F32)
    large = MAX_EXACT + (jnp.log(nf / MAX_EXACT) / math.log(MAX_DISTANCE / MAX_EXACT)
                         * (NUM_BUCKETS - MAX_EXACT)).astype(jnp.int32)
    large = jnp.minimum(large, NUM_BUCKETS - 1)
    return jnp.where(is_small, n, large)


def _bias_tables(rel_bias):
    rb = rel_bias.astype(F32)
    d_band = BLOCK + jnp.arange(BLOCK)[:, None] - jnp.arange(2 * BLOCK)[None, :]
    ok = (d_band >= 0) & (d_band < WINDOW)
    onehot = (_t5_bucket(d_band)[..., None] == jnp.arange(NUM_BUCKETS)).astype(F32)
    looked_up = jnp.einsum("qkn,nh->hqk", onehot, rb, precision=lax.Precision.HIGHEST)
    band = jnp.where(ok[None], looked_up, NEG_INF)
    meta = rb[_t5_bucket(jnp.array(WINDOW))]
    return band, meta


def kernel(x, meta_tokens, rel_bias, ln_mix_g, ln_mix_b, ln_ffn_g, ln_ffn_b, attn_w_qkv, attn_b_qkv, attn_sinks, attn_w_o, rwkv_mu, rwkv_w0, rwkv_w1, rwkv_w2, rwkv_a0, rwkv_a1, rwkv_a2, rwkv_g1, rwkv_g2, rwkv_k_k, rwkv_k_a, rwkv_r_k, rwkv_w_rkv, rwkv_lnx_g, rwkv_lnx_b, rwkv_w_o, ffn_w_gu, ffn_w_down, moe_router, moe_w_gu, moe_w_down):
    b, seq, d = x.shape
    lp = seq + BLOCK
    m = b * lp
    row = lambda t: t.reshape(1, -1).astype(F32)

    meta = jnp.broadcast_to(meta_tokens.astype(x.dtype)[None], (b, N_META, d))
    h = jnp.concatenate([jnp.zeros((b, PAD, d), x.dtype), meta, x], axis=1).reshape(m, d)

    q_dim = N_HEADS * HEAD_DIM
    e_dim = attn_w_qkv.shape[2]
    col_scale = jnp.where(jnp.arange(e_dim) < q_dim, ATTN_SCALE, 1.0).astype(F32)
    qkv = _qkv_proj(h, attn_w_qkv[0].astype(BF16), row(attn_b_qkv[0]), row(col_scale))
    bias_band, bias_meta = _bias_tables(rel_bias)
    o = _attention(qkv.reshape(b, lp, e_dim), bias_band, attn_sinks[0].astype(F32), bias_meta)
    h = _proj_res_ln(o.reshape(m, q_dim), attn_w_o[0].astype(BF16), h, row(ln_mix_g[0]), row(ln_mix_b[0]))
    h = _ffn(h, ffn_w_gu[0].astype(BF16), ffn_w_down[0].astype(BF16), row(ln_ffn_g[0]), row(ln_ffn_b[0]), lp, b)

    r, k, v, a, ld, g = _rwkv_proj(
        h, rwkv_mu[0], rwkv_w_rkv[0].astype(BF16), row(rwkv_w0[0]), rwkv_w1[0].astype(BF16),
        rwkv_w2[0].astype(BF16), row(rwkv_a0[0]), rwkv_a1[0].astype(BF16), rwkv_a2[0].astype(BF16),
        rwkv_g1[0].astype(BF16), rwkv_g2[0].astype(BF16), lp, b)
    t3 = lambda t: t.reshape(b, lp, d)
    y = _wkv(t3(r), t3(k), t3(v), t3(a), t3(ld), row(rwkv_k_k[0]), row(rwkv_k_a[0]), row(rwkv_r_k[0]),
             row(rwkv_lnx_g[0]), row(rwkv_lnx_b[0]))
    h = _proj_res_ln(y.reshape(m, d), rwkv_w_o[0].astype(BF16), h, row(ln_mix_g[1]), row(ln_mix_b[1]), mult=g)
    w_router = jnp.pad(moe_router[0].astype(F32), ((0, 0), (0, LANES - N_EXPERTS)))
    routed = _router(h, w_router)
    expert_idx = routed[:, :2].astype(jnp.int32)
    y_slots = _moe_experts(h, expert_idx, moe_w_gu[0].astype(BF16), moe_w_down[0].astype(BF16))
    return _moe_combine(h, y_slots, routed, row(ln_ffn_g[1]), row(ln_ffn_b[1]), b, lp)
```

```python
import functools
import math

import jax
import jax.numpy as jnp
from jax import lax
from jax.experimental import pallas as pl
from jax.experimental.pallas import tpu as pltpu

F32 = jnp.float32
BF16 = jnp.bfloat16

N_META = 16
N_HEADS = 16
N_KV_HEADS = 4
HEAD_DIM = 64
GROUP = N_HEADS // N_KV_HEADS
WINDOW = 128
BLOCK = 128
PAD = BLOCK - N_META
ATTN_SCALE = 1.0 / math.sqrt(HEAD_DIM)
NEG_INF = -1e30
NUM_BUCKETS = 32
MAX_EXACT = NUM_BUCKETS // 2
MAX_DISTANCE = 128
RWKV_HEAD = 64
GN_EPS = 64e-5
N_EXPERTS = 8
DEPTH = 2
ALPHA = (2 * DEPTH) ** 0.25
LN_EPS = 1e-5

WKV_CHUNK = 64
WKV_BLOCK_TOKENS = 384
WKV_BLOCK_PAIRS = 2
LANES = 128
DMA_ISSUE_UNROLL = 8
VMEM_LIMIT_BYTES = 56 * 1024 * 1024


def _params(*sem):
    return pltpu.CompilerParams(dimension_semantics=sem, vmem_limit_bytes=VMEM_LIMIT_BYTES)


def _pick(n, candidates):
    for c in candidates:
        if n % c == 0:
            return c
    raise ValueError(f"no tile in {candidates} divides {n}")


def _pad_row_mask(first_row, rows, lp, nbatch):
    r = first_row + lax.broadcasted_iota(jnp.int32, (rows, 1), 0)
    mask = r < PAD
    for bi in range(1, nbatch):
        mask = jnp.logical_or(mask, jnp.logical_and(r >= bi * lp, r < bi * lp + PAD))
    return mask


def _layer_norm(z, g, b):
    mu = jnp.mean(z, axis=-1, keepdims=True)
    zc = z - mu
    var = jnp.mean(zc * zc, axis=-1, keepdims=True)
    return zc * lax.rsqrt(var + LN_EPS) * g + b


def _dot(a, b):
    return jnp.dot(a.astype(BF16), b.astype(BF16), preferred_element_type=F32)


def _dot_nt(a, b):
    return lax.dot_general(a.astype(BF16), b.astype(BF16), (((1,), (1,)), ((), ())),
                           preferred_element_type=F32)


def _qkv_kernel(x_ref, w_ref, b_ref, s_ref, o_ref):
    acc = _dot(x_ref[...], w_ref[...])
    o_ref[...] = ((acc + b_ref[...]) * s_ref[...]).astype(o_ref.dtype)


def _qkv_proj(x, w, b, s):
    m, d = x.shape
    n = w.shape[1]
    tm = _pick(m, (768, 384, 256, 128))
    return pl.pallas_call(
        _qkv_kernel,
        grid=(m // tm,),
        in_specs=[pl.BlockSpec((tm, d), lambda i: (i, 0)),
                  pl.BlockSpec((d, n), lambda i: (0, 0)),
                  pl.BlockSpec((1, n), lambda i: (0, 0)),
                  pl.BlockSpec((1, n), lambda i: (0, 0))],
        out_specs=pl.BlockSpec((tm, n), lambda i: (i, 0)),
        out_shape=jax.ShapeDtypeStruct((m, n), BF16),
        compiler_params=_params("parallel"),
        name="qkv_proj",
    )(x, w, b, s)


def _attn_kernel(sink_ref, mbias_ref, cur_ref, prev_ref, meta_ref, bias_ref, o_ref):
    n = pl.program_id(1)
    q_dim = N_HEADS * HEAD_DIM
    kv_dim = N_KV_HEADS * HEAD_DIM
    cur = cur_ref[0]
    prev = prev_ref[0]
    meta = meta_ref[0][PAD:, :]

    col = lax.broadcasted_iota(jnp.int32, (1, 2 * BLOCK), 1)
    key_ok = (col + (n - 1) * BLOCK) >= PAD
    qrow = jnp.bitwise_and(lax.broadcasted_iota(jnp.int32, (GROUP * BLOCK, N_META), 0), BLOCK - 1)
    midx = lax.broadcasted_iota(jnp.int32, (GROUP * BLOCK, N_META), 1)
    meta_ok = (n * BLOCK + qrow - PAD - midx) >= WINDOW
    grp = jnp.right_shift(lax.broadcasted_iota(jnp.int32, (GROUP * BLOCK, 1), 0), int(math.log2(BLOCK)))

    for h in range(N_KV_HEADS):
        ks = slice(h * HEAD_DIM, (h + 1) * HEAD_DIM)
        vs = slice(kv_dim + h * HEAD_DIM, kv_dim + (h + 1) * HEAD_DIM)
        kb = jnp.concatenate([prev[:, ks], cur[:, q_dim + h * HEAD_DIM:q_dim + (h + 1) * HEAD_DIM]], axis=0)
        vb = jnp.concatenate([prev[:, vs], cur[:, q_dim + kv_dim + h * HEAD_DIM:
                                               q_dim + kv_dim + (h + 1) * HEAD_DIM]], axis=0)
        km = meta[:, ks]
        vm = meta[:, vs]
        q4 = jnp.concatenate(
            [cur[:, (h * GROUP + g) * HEAD_DIM:(h * GROUP + g + 1) * HEAD_DIM] for g in range(GROUP)], axis=0)

        sink_col = jnp.zeros((GROUP * BLOCK, 1), F32)
        mb_col = jnp.zeros((GROUP * BLOCK, 1), F32)
        for g in range(GROUP):
            sink_col = jnp.where(grp == g, sink_ref[h * GROUP + g], sink_col)
            mb_col = jnp.where(grp == g, mbias_ref[h * GROUP + g], mb_col)

        bias4 = bias_ref[h * GROUP:(h + 1) * GROUP].reshape(GROUP * BLOCK, 2 * BLOCK)
        s = jnp.where(key_ok, _dot_nt(q4, kb) + bias4, NEG_INF)
        sm = jnp.where(meta_ok, _dot_nt(q4, km) + mb_col, NEG_INF)
        mx = jnp.maximum(jnp.maximum(jnp.max(s, axis=-1, keepdims=True),
                                     jnp.max(sm, axis=-1, keepdims=True)), sink_col)
        p = jnp.exp(s - mx)
        pm = jnp.exp(sm - mx)
        denom = (jnp.sum(p, axis=-1, keepdims=True) + jnp.sum(pm, axis=-1, keepdims=True)
                 + jnp.exp(sink_col - mx))
        o4 = (_dot(p, vb) + _dot(pm, vm)) / denom
        for g in range(GROUP):
            hd = h * GROUP + g
            o_ref[0, :, hd * HEAD_DIM:(hd + 1) * HEAD_DIM] = o4[g * BLOCK:(g + 1) * BLOCK].astype(o_ref.dtype)


def _attention(qkv, bias_tbl, sinks, meta_bias):
    b, lp, e = qkv.shape
    nb = lp // BLOCK
    q_dim = N_HEADS * HEAD_DIM
    kv2 = 2 * N_KV_HEADS * HEAD_DIM
    kv_blk = q_dim // kv2
    smem = pl.BlockSpec(memory_space=pltpu.SMEM)
    return pl.pallas_call(
        _attn_kernel,
        grid=(b, nb),
        in_specs=[smem, smem,
                  pl.BlockSpec((1, BLOCK, e), lambda i, j: (i, j, 0)),
                  pl.BlockSpec((1, BLOCK, kv2), lambda i, j: (i, jnp.maximum(j - 1, 0), kv_blk)),
                  pl.BlockSpec((1, BLOCK, kv2), lambda i, j: (i, 0, kv_blk)),
                  pl.BlockSpec((N_HEADS, BLOCK, 2 * BLOCK), lambda i, j: (0, 0, 0))],
        out_specs=pl.BlockSpec((1, BLOCK, q_dim), lambda i, j: (i, j, 0)),
        out_shape=jax.ShapeDtypeStruct((b, lp, q_dim), BF16),
        compiler_params=_params("parallel", "parallel"),
        name="swa_attention",
    )(sinks, meta_bias, qkv, qkv, qkv, bias_tbl)


def _proj_ln_kernel(a_ref, w_ref, res_ref, g_ref, b_ref, o_ref):
    y = _dot(a_ref[...], w_ref[...])
    o_ref[...] = _layer_norm(ALPHA * res_ref[...] + y, g_ref[...], b_ref[...])


def _proj_res_ln(a, w, res, g, b):
    m, k = a.shape
    d = w.shape[1]
    tm = _pick(m, (768, 384, 256, 128))
    row = lambda i: (i, 0)
    fixed = lambda i: (0, 0)
    return pl.pallas_call(
        _proj_ln_kernel,
        grid=(m // tm,),
        in_specs=[pl.BlockSpec((tm, k), row), pl.BlockSpec((k, d), fixed), pl.BlockSpec((tm, d), row),
                  pl.BlockSpec((1, d), fixed), pl.BlockSpec((1, d), fixed)],
        out_specs=pl.BlockSpec((tm, d), row),
        out_shape=jax.ShapeDtypeStruct((m, d), F32),
        compiler_params=_params("parallel"),
        name="proj_res_ln",
    )(a, w, res, g, b)


def _split_bf16(x):
    hi = x.astype(BF16)
    return hi, (x - hi.astype(F32)).astype(BF16)


def _route(h, w_router):
    h_hi, h_lo = _split_bf16(h)
    w_hi, w_lo = _split_bf16(w_router)
    logits = (jnp.dot(h_hi, w_hi, preferred_element_type=F32) + jnp.dot(h_hi, w_lo, preferred_element_type=F32)
              + jnp.dot(h_lo, w_hi, preferred_element_type=F32))
    lane = lax.broadcasted_iota(jnp.int32, logits.shape, 1).astype(F32)
    neg = -jnp.inf
    lg = jnp.where(lane < N_EXPERTS, logits, neg)
    m1 = jnp.max(lg, axis=-1, keepdims=True)
    i1 = jnp.min(jnp.where(lg == m1, lane, float(LANES)), axis=-1, keepdims=True)
    lg2 = jnp.where(lane == i1, neg, lg)
    m2 = jnp.max(lg2, axis=-1, keepdims=True)
    i2 = jnp.min(jnp.where(lg2 == m2, lane, float(LANES)), axis=-1, keepdims=True)
    e2 = jnp.exp(m2 - m1)
    den = 1.0 + e2
    return jnp.where(lane == 0.0, i1, jnp.where(lane == 1.0, i2, jnp.where(
        lane == 2.0, 1.0 / den, jnp.where(lane == 3.0, e2 / den, 0.0))))


def _proj_ln_route_kernel(a_ref, m_ref, w_ref, res_ref, g_ref, b_ref, wr_ref, o_ref, ro_ref):
    a = a_ref[...].astype(F32) * m_ref[...].astype(F32)
    h = _layer_norm(ALPHA * res_ref[...] + _dot(a, w_ref[...]), g_ref[...], b_ref[...])
    o_ref[...] = h
    ro_ref[...] = _route(h, wr_ref[...])


def _proj_res_ln_route(a, mult, w, res, g, b, w_router):
    m, k = a.shape
    d = w.shape[1]
    tm = _pick(m, (768, 384, 256, 128))
    row = lambda i: (i, 0)
    fixed = lambda i: (0, 0)
    return pl.pallas_call(
        _proj_ln_route_kernel,
        grid=(m // tm,),
        in_specs=[pl.BlockSpec((tm, k), row), pl.BlockSpec((tm, k), row), pl.BlockSpec((k, d), fixed),
                  pl.BlockSpec((tm, d), row), pl.BlockSpec((1, d), fixed), pl.BlockSpec((1, d), fixed),
                  pl.BlockSpec((d, LANES), fixed)],
        out_specs=[pl.BlockSpec((tm, d), row), pl.BlockSpec((tm, LANES), row)],
        out_shape=[jax.ShapeDtypeStruct((m, d), F32), jax.ShapeDtypeStruct((m, LANES), F32)],
        compiler_params=_params("parallel"),
        name="proj_res_ln_route",
    )(a, mult, w, res, g, b, w_router)


def _silu_mul(gate, up):
    return gate * (1.0 / (1.0 + jnp.exp(-gate))) * up


def _ffn_kernel(x_ref, wg_ref, wu_ref, wd_ref, g_ref, b_ref, o_ref, acc_ref, xb_ref, *, lp, nbatch, tm):
    i = pl.program_id(0)
    f = pl.program_id(1)

    @pl.when(f == 0)
    def _():
        acc_ref[...] = jnp.zeros_like(acc_ref)
        xb_ref[...] = x_ref[...].astype(BF16)

    xb = xb_ref[...]
    gate = jnp.dot(xb, wg_ref[...], preferred_element_type=F32)
    up = jnp.dot(xb, wu_ref[...], preferred_element_type=F32)
    acc_ref[...] += jnp.dot(_silu_mul(gate, up).astype(BF16), wd_ref[...], preferred_element_type=F32)

    @pl.when(f == pl.num_programs(1) - 1)
    def _():
        y = _layer_norm(ALPHA * x_ref[...] + acc_ref[...], g_ref[...], b_ref[...])
        o_ref[...] = jnp.where(_pad_row_mask(i * tm, tm, lp, nbatch), 0.0, y)


def _ffn(x, w_gu, w_down, g, b, lp, nbatch):
    m, d = x.shape
    ff = w_down.shape[0]
    tm = _pick(m, (384, 256, 128))
    tf = _pick(ff, (1408, 896, 512, 256, 128))
    nf = ff // tf
    row = lambda i, f: (i, 0)
    fixed = lambda i, f: (0, 0)
    return pl.pallas_call(
        functools.partial(_ffn_kernel, lp=lp, nbatch=nbatch, tm=tm),
        grid=(m // tm, nf),
        in_specs=[pl.BlockSpec((tm, d), row),
                  pl.BlockSpec((d, tf), lambda i, f: (0, f)),
                  pl.BlockSpec((d, tf), lambda i, f: (0, f + nf)),
                  pl.BlockSpec((tf, d), lambda i, f: (f, 0)),
                  pl.BlockSpec((1, d), fixed), pl.BlockSpec((1, d), fixed)],
        out_specs=pl.BlockSpec((tm, d), row),
        out_shape=jax.ShapeDtypeStruct((m, d), F32),
        scratch_shapes=[pltpu.VMEM((tm, d), F32), pltpu.VMEM((tm, d), BF16)],
        compiler_params=_params("parallel", "arbitrary"),
        name="swiglu_res_ln",
    )(x, w_gu, w_gu, w_down, g, b)


def _moe_kernel(tile_e_ref, nact_ref, src_ref, src_next_ref, dst_ref, dst_prev_ref, x_hbm, wg_ref, wu_ref,
                wd_ref, y_hbm, xbuf, xb_ref, acc_ref, ybuf, gsem, ssem, *, tm, nf):
    j = pl.program_id(0)
    f = pl.program_id(1)
    last_tile = j == pl.num_programs(0) - 1
    active = j < nact_ref[0]
    slot = lax.rem(j, 2)
    other = 1 - slot
    rows = tm // nf

    def gather_row(idx_ref, s, i):
        t = idx_ref[0, 0, i]
        pltpu.make_async_copy(x_hbm.at[pl.ds(t, 1)], xbuf.at[s, pl.ds(i, 1)], gsem.at[s]).start()

    def scatter_row(idx_ref, s, i):
        t = idx_ref[0, 0, i]
        pltpu.make_async_copy(ybuf.at[s, pl.ds(i, 1)], y_hbm.at[pl.ds(t, 1)], ssem.at[s]).start()

    def gather_wait(s):
        pltpu.make_async_copy(x_hbm.at[pl.ds(0, tm)], xbuf.at[s], gsem.at[s]).wait()

    def scatter_wait(s):
        pltpu.make_async_copy(ybuf.at[s], y_hbm.at[pl.ds(0, tm)], ssem.at[s]).wait()

    def issue_step_dmas():
        for i in range(rows):
            gather_row(src_next_ref, other, f * rows + i)
            scatter_row(dst_prev_ref, other, f * rows + i)

    @pl.when(f == 0)
    def _():
        @pl.when(j == 0)
        def _():
            ybuf[...] = jnp.zeros_like(ybuf)

            def body(i, carry):
                gather_row(src_ref, 0, i)
                return carry
            lax.fori_loop(0, tm, body, 0, unroll=DMA_ISSUE_UNROLL)

        gather_wait(slot)
        xb_ref[...] = xbuf[slot].astype(BF16)
        acc_ref[...] = jnp.zeros_like(acc_ref)

    @pl.when(active)
    def _():
        issue_step_dmas()
        xb = xb_ref[...]
        gate = jnp.dot(xb, wg_ref[0], preferred_element_type=F32)
        up = jnp.dot(xb, wu_ref[0], preferred_element_type=F32)
        acc_ref[...] += jnp.dot(_silu_mul(gate, up).astype(BF16), wd_ref[0], preferred_element_type=F32)

    @pl.when(jnp.logical_not(active))
    def _():
        issue_step_dmas()

    @pl.when(f == nf - 1)
    def _():
        @pl.when(j >= 1)
        def _():
            scatter_wait(slot)

        ybuf[slot] = acc_ref[...]

        @pl.when(last_tile)
        def _():
            def body(i, carry):
                scatter_row(dst_ref, slot, i)
                return carry
            lax.fori_loop(0, tm, body, 0, unroll=DMA_ISSUE_UNROLL)
            scatter_wait(slot)
            scatter_wait(other)
            gather_wait(other)


def _moe_experts(x, expert_idx, w_gu, w_down):
    m, d = x.shape
    ne, ff = w_down.shape[0], w_down.shape[1]
    nslots = 2 * m
    tm = _pick(nslots, (512, 256, 128))
    tf = _pick(ff, (1792, 896, 512, 256, 128))
    nf = ff // tf
    n_tiles = nslots // tm + ne
    p_rows = n_tiles * tm

    e_flat = expert_idx.reshape(nslots)
    slot_ids = jnp.arange(nslots, dtype=jnp.int32)
    _, order = lax.sort_key_val(e_flat, slot_ids)
    counts = jnp.sum((e_flat[:, None] == jnp.arange(ne, dtype=jnp.int32)[None, :]).astype(jnp.int32), axis=0)
    gsize = ((counts + tm - 1) // tm) * tm
    gend = jnp.cumsum(gsize)
    tile_e = jnp.minimum(jnp.searchsorted(gend, jnp.arange(n_tiles, dtype=jnp.int32) * tm, side="right"),
                         ne - 1).astype(jnp.int32)
    pos = jnp.arange(p_rows, dtype=jnp.int32)
    pos_e = jnp.repeat(tile_e, tm)
    rank = pos - (gend - gsize)[pos_e]
    is_fill = rank >= counts[pos_e]
    slot_sorted = order[jnp.clip((jnp.cumsum(counts) - counts)[pos_e] + rank, 0, nslots - 1)]
    fill_rank = jnp.cumsum(is_fill.astype(jnp.int32)) - 1
    src_sorted = jnp.where(is_fill, 0, slot_sorted // 2)
    dst_sorted = jnp.where(is_fill, nslots + fill_rank, (slot_sorted % 2) * m + slot_sorted // 2)
    nact = (gend[-1:] // tm).astype(jnp.int32)
    src3 = src_sorted.reshape(n_tiles, 1, tm)
    dst3 = dst_sorted.reshape(n_tiles, 1, tm)

    def f_eff(j, f, nact_ref):
        return jnp.where(j < nact_ref[0], f, nf - 1)

    smem_blk = lambda imap: pl.BlockSpec((1, 1, tm), imap, memory_space=pltpu.SMEM)
    grid_spec = pltpu.PrefetchScalarGridSpec(
        num_scalar_prefetch=2,
        grid=(n_tiles, nf),
        in_specs=[smem_blk(lambda j, f, te, na: (j, 0, 0)),
                  smem_blk(lambda j, f, te, na: (jnp.minimum(j + 1, n_tiles - 1), 0, 0)),
                  smem_blk(lambda j, f, te, na: (j, 0, 0)),
                  smem_blk(lambda j, f, te, na: (jnp.where(j == 0, n_tiles - 1, j - 1), 0, 0)),
                  pl.BlockSpec(memory_space=pl.ANY),
                  pl.BlockSpec((1, d, tf), lambda j, f, te, na: (te[j], 0, f_eff(j, f, na))),
                  pl.BlockSpec((1, d, tf), lambda j, f, te, na: (te[j], 0, f_eff(j, f, na) + nf)),
                  pl.BlockSpec((1, tf, d), lambda j, f, te, na: (te[j], f_eff(j, f, na), 0))],
        out_specs=pl.BlockSpec(memory_space=pl.ANY),
        scratch_shapes=[pltpu.VMEM((2, tm, d), F32), pltpu.VMEM((tm, d), BF16), pltpu.VMEM((tm, d), F32),
                        pltpu.VMEM((2, tm, d), F32), pltpu.SemaphoreType.DMA((2,)),
                        pltpu.SemaphoreType.DMA((2,))],
    )
    return pl.pallas_call(
        functools.partial(_moe_kernel, tm=tm, nf=nf),
        grid_spec=grid_spec,
        out_shape=jax.ShapeDtypeStruct((p_rows, d), F32),
        compiler_params=_params("arbitrary", "arbitrary"),
        name="moe_grouped_swiglu",
    )(tile_e, nact, src3, src3, dst3, dst3, x, w_gu, w_gu, w_down)


def _combine_kernel(x_ref, y1_ref, y2_ref, ro_ref, g_ref, b_ref, o_ref):
    ro = ro_ref[...]
    lane = lax.broadcasted_iota(jnp.int32, ro.shape, 1)
    w1 = jnp.sum(jnp.where(lane == 2, ro, 0.0), axis=-1, keepdims=True)
    w2 = jnp.sum(jnp.where(lane == 3, ro, 0.0), axis=-1, keepdims=True)
    z = ALPHA * x_ref[...] + w1 * y1_ref[...] + w2 * y2_ref[...]
    o_ref[0] = _layer_norm(z, g_ref[...], b_ref[...])


def _moe_combine(x, y_slots, router_out, g, b, nbatch, lp):
    m, d = x.shape
    tm = BLOCK
    nb = lp // tm
    choice2 = m // tm
    row = lambda i, j: (i * nb + j + 1, 0)
    fixed = lambda i, j: (0, 0)
    return pl.pallas_call(
        _combine_kernel,
        grid=(nbatch, nb - 1),
        in_specs=[pl.BlockSpec((tm, d), row), pl.BlockSpec((tm, d), row),
                  pl.BlockSpec((tm, d), lambda i, j: (choice2 + i * nb + j + 1, 0)),
                  pl.BlockSpec((tm, LANES), row),
                  pl.BlockSpec((1, d), fixed), pl.BlockSpec((1, d), fixed)],
        out_specs=pl.BlockSpec((1, tm, d), lambda i, j: (i, j, 0)),
        out_shape=jax.ShapeDtypeStruct((nbatch, lp - tm, d), F32),
        compiler_params=_params("parallel", "parallel"),
        name="moe_combine_ln",
    )(x, y_slots, y_slots, router_out, g, b)


def _rwkv_proj_kernel(x_ref, xp_ref, mu_ref, wr_ref, wk_ref, wv_ref, w0_ref, w1_ref, w2_ref,
                      a0_ref, a1_ref, a2_ref, g1_ref, g2_ref,
                      r_ref, k_ref, v_ref, a_ref, ld_ref, g_ref, *, lp, nbatch):
    x = x_ref[...]
    tm = x.shape[0]
    rolled = pltpu.roll(x, 1, axis=0)
    prev_row = xp_ref[7:8, :]
    first = lax.broadcasted_iota(jnp.int32, (tm, 1), 0) == 0
    xx = jnp.where(first, prev_row, rolled) - x
    xx = jnp.where(_pad_row_mask(pl.program_id(0) * tm, tm, lp, nbatch), 0.0, xx)

    def mix(i):
        return x + xx * mu_ref[i:i + 1, :]

    r_ref[...] = _dot(mix(0), wr_ref[...]).astype(r_ref.dtype)
    k_ref[...] = _dot(mix(2), wk_ref[...]).astype(k_ref.dtype)
    v_ref[...] = _dot(mix(3), wv_ref[...]).astype(v_ref.dtype)
    wl = w0_ref[...] + _dot(jnp.tanh(_dot(mix(1), w1_ref[...])), w2_ref[...])
    z = -wl
    softplus = jnp.maximum(z, 0.0) + jnp.log(1.0 + jnp.exp(-jnp.abs(z)))
    ld_ref[...] = -jnp.exp(-softplus - 0.5)
    al = a0_ref[...] + _dot(_dot(mix(4), a1_ref[...]), a2_ref[...])
    a_ref[...] = (1.0 / (1.0 + jnp.exp(-al))).astype(a_ref.dtype)
    gl = _dot(mix(5), g1_ref[...])
    g_ref[...] = _dot(1.0 / (1.0 + jnp.exp(-gl)), g2_ref[...]).astype(g_ref.dtype)


def _rwkv_proj(x, mu, w_rkv, w0, w1, w2, a0, a1, a2, g1, g2, lp, nbatch):
    m, d = x.shape
    tm = _pick(m, (384, 256, 128))
    row = lambda i: (i, 0)
    fixed = lambda i: (0, 0)
    full = lambda arr: pl.BlockSpec(arr.shape, fixed)
    out = lambda dt: jax.ShapeDtypeStruct((m, d), dt)
    return pl.pallas_call(
        functools.partial(_rwkv_proj_kernel, lp=lp, nbatch=nbatch),
        grid=(m // tm,),
        in_specs=[pl.BlockSpec((tm, d), row),
                  pl.BlockSpec((8, d), lambda i: (jnp.maximum(i * (tm // 8) - 1, 0), 0)),
                  full(mu), full(w_rkv[0]), full(w_rkv[1]), full(w_rkv[2]),
                  full(w0), full(w1), full(w2), full(a0), full(a1), full(a2), full(g1), full(g2)],
        out_specs=[pl.BlockSpec((tm, d), row)] * 6,
        out_shape=[out(BF16), out(BF16), out(BF16), out(BF16), out(F32), out(BF16)],
        compiler_params=_params("parallel"),
        name="rwkv_proj",
    )(x, x, mu, w_rkv[0], w_rkv[1], w_rkv[2], w0, w1, w2, a0, a1, a2, g1, g2)


def _bmm(a, b):
    return jnp.einsum("cij,cjk->cik", a.astype(BF16), b.astype(BF16), preferred_element_type=F32)


def _bmm_nt(a, b):
    return jnp.einsum("cik,cjk->cij", a.astype(BF16), b.astype(BF16), preferred_element_type=F32)


def _bmm_tn(a, b):
    return jnp.einsum("cki,ckj->cij", a.astype(BF16), b.astype(BF16), preferred_element_type=F32)


def _wkv_head_pair(r, k, v, a, ld, k_k, k_a, r_k, gn_g, gn_b, z):
    c = WKV_CHUNK
    n = RWKV_HEAD
    tb = r.shape[0]
    nc = tb // c
    r, k, v, a = (t.astype(F32) for t in (r, k, v, a))
    head0 = lax.broadcasted_iota(jnp.int32, (1, LANES), 1) < n

    def head_sum(x):
        s0 = jnp.sum(jnp.where(head0, x, 0.0), axis=-1, keepdims=True)
        s1 = jnp.sum(jnp.where(head0, 0.0, x), axis=-1, keepdims=True)
        return jnp.where(head0, s0, s1)

    def stack(x):
        xb = x.astype(BF16)
        zero = jnp.zeros_like(xb)
        return jnp.concatenate([jnp.where(head0, xb, zero), jnp.where(head0, zero, xb)], axis=1)

    def fold(x):
        return x[:, :c] + x[:, c:]

    kkr = k * k_k
    kk = kkr / jnp.maximum(jnp.sqrt(head_sum(kkr * kkr)), 1e-12)
    km = k * (1.0 + (a - 1.0) * k_a)
    bv = kk * a

    tpos = jnp.bitwise_and(lax.broadcasted_iota(jnp.int32, (tb, 1), 0), c - 1)
    cs = ld
    shift = 1
    while shift < c:
        cs = cs + jnp.where(tpos >= shift, pltpu.roll(cs, shift, axis=0), 0.0)
        shift *= 2

    to3 = lambda x: x.reshape(nc, c, LANES)
    cs3, ld3 = to3(cs), to3(ld)
    cs_last = cs3[:, c - 1:c, :]
    e_neg = jnp.exp(-cs3)
    e_tail = jnp.exp(cs_last - cs3)
    at_s = stack(-to3(kk) * jnp.exp(cs3 - ld3))
    rt = to3(r) * jnp.exp(cs3)
    rt_s = stack(rt)
    bt_s = stack(to3(bv) * e_neg)
    kt_s = stack(to3(km) * e_neg)
    bh_s = stack(to3(bv) * e_tail)
    kh_s = stack(to3(km) * e_tail)
    v_s = stack(to3(v))
    dec = jnp.exp(cs_last)

    c2 = 2 * c
    g = _bmm_nt(jnp.concatenate([at_s, rt_s], axis=1), jnp.concatenate([bt_s, kt_s], axis=1))
    ti = jnp.bitwise_and(lax.broadcasted_iota(jnp.int32, (1, c2, c2), 1), c - 1)
    si = jnp.bitwise_and(lax.broadcasted_iota(jnp.int32, (1, c2, c2), 2), c - 1)
    strict = ti > si
    incl = ti >= si
    lab = jnp.where(strict, g[:, :c2, :c2], 0.0)
    lak = jnp.where(strict, g[:, :c2, c2:], 0.0)
    mrb = jnp.where(incl, g[:, c2:, :c2], 0.0)
    mrk = jnp.where(incl, g[:, c2:, c2:], 0.0)

    eye2 = lax.broadcasted_iota(jnp.int32, (1, c2, c2), 1) == lax.broadcasted_iota(jnp.int32, (1, c2, c2), 2)
    t_inv = jnp.where(eye2, 1.0, lab)
    lpow = lab
    for _ in range(int(math.log2(c)) - 1):
        lpow = _bmm(lpow, lpow)
        t_inv = t_inv + _bmm(t_inv, lpow)

    lakv = _bmm(lak, v_s)
    wu = _bmm(t_inv, jnp.concatenate([at_s, lakv.astype(BF16)], axis=2))
    qy = _bmm(mrb, wu)
    q = rt + fold(qy[:, :, :LANES])
    yp = fold(qy[:, :, LANES:] + _bmm(mrk, v_s))
    eye_l = (lax.broadcasted_iota(jnp.int32, (1, LANES, LANES), 1)
             == lax.broadcasted_iota(jnp.int32, (1, LANES, LANES), 2))
    a_t = jnp.where(eye_l, dec, 0.0) + _bmm_tn(bh_s, wu[:, :, :LANES])
    g_t = _bmm_tn(jnp.concatenate([bh_s, kh_s], axis=1),
                  jnp.concatenate([wu[:, :, LANES:].astype(BF16), v_s], axis=1))

    ys = []
    for ci in range(nc):
        ys.append(yp[ci] + _dot(q[ci], z))
        z = _dot(a_t[ci], z) + g_t[ci]
    y = jnp.concatenate(ys, axis=0)

    mu = head_sum(y) * (1.0 / n)
    yc = y - mu
    var = head_sum(yc * yc) * (1.0 / n)
    yn = yc * lax.rsqrt(var + GN_EPS) * gn_g + gn_b
    bonus = head_sum(r * km * r_k) * v
    return yn + bonus, z


def _wkv_kernel(r_ref, k_ref, v_ref, a_ref, ld_ref, kk_ref, ka_ref, rk_ref, gg_ref, gb_ref,
                y_ref, z_ref):
    @pl.when(pl.program_id(1) == 0)
    def _():
        z_ref[...] = jnp.zeros_like(z_ref)

    for bi in range(r_ref.shape[0]):
        for pi in range(r_ref.shape[2] // LANES):
            ls = slice(pi * LANES, (pi + 1) * LANES)
            y, z = _wkv_head_pair(r_ref[bi, :, ls], k_ref[bi, :, ls], v_ref[bi, :, ls], a_ref[bi, :, ls],
                                  ld_ref[bi, :, ls], kk_ref[:, ls], ka_ref[:, ls], rk_ref[:, ls], gg_ref[:, ls],
                                  gb_ref[:, ls], z_ref[bi, pi])
            y_ref[bi, :, ls] = y
            z_ref[bi, pi] = z


def _wkv(r, k, v, a, ld, k_k, k_a, r_k, gn_g, gn_b):
    b, lp, d = r.shape
    tb = _pick(lp, (WKV_BLOCK_TOKENS, 128, 64))
    width = WKV_BLOCK_PAIRS * LANES
    tok_spec = pl.BlockSpec((b, tb, width), lambda j, t: (0, t, j))
    par_spec = pl.BlockSpec((1, width), lambda j, t: (0, j))
    return pl.pallas_call(
        _wkv_kernel,
        grid=(d // width, lp // tb),
        in_specs=[tok_spec] * 5 + [par_spec] * 5,
        out_specs=tok_spec,
        out_shape=jax.ShapeDtypeStruct((b, lp, d), F32),
        scratch_shapes=[pltpu.VMEM((b, WKV_BLOCK_PAIRS, LANES, LANES), F32)],
        compiler_params=_params("parallel", "arbitrary"),
        name="wkv7_chunked",
    )(r, k, v, a, ld, k_k, k_a, r_k, gn_g, gn_b)


def _t5_bucket(dist):
    n = jnp.maximum(dist, 0)
    is_small = n < MAX_EXACT
    nf = jnp.maximum(n, 1).astype(F32)
    large = MAX_EXACT + (jnp.log(nf / MAX_EXACT) / math.log(MAX_DISTANCE / MAX_EXACT)
                         * (NUM_BUCKETS - MAX_EXACT)).astype(jnp.int32)
    large = jnp.minimum(large, NUM_BUCKETS - 1)
    return jnp.where(is_small, n, large)


def _bias_tables(rel_bias):
    rb = rel_bias.astype(F32)
    d_band = BLOCK + jnp.arange(BLOCK)[:, None] - jnp.arange(2 * BLOCK)[None, :]
    ok = (d_band >= 0) & (d_band < WINDOW)
    onehot = (_t5_bucket(d_band)[..., None] == jnp.arange(NUM_BUCKETS)).astype(F32)
    looked_up = jnp.einsum("qkn,nh->hqk", onehot, rb, precision=lax.Precision.HIGHEST)
    band = jnp.where(ok[None], looked_up, NEG_INF)
    meta = rb[_t5_bucket(jnp.array(WINDOW))]
    return band, meta


def kernel(x, meta_tokens, rel_bias, ln_mix_g, ln_mix_b, ln_ffn_g, ln_ffn_b, attn_w_qkv, attn_b_qkv, attn_sinks, attn_w_o, rwkv_mu, rwkv_w0, rwkv_w1, rwkv_w2, rwkv_a0, rwkv_a1, rwkv_a2, rwkv_g1, rwkv_g2, rwkv_k_k, rwkv_k_a, rwkv_r_k, rwkv_w_rkv, rwkv_lnx_g, rwkv_lnx_b, rwkv_w_o, ffn_w_gu, ffn_w_down, moe_router, moe_w_gu, moe_w_down):
    b, seq, d = x.shape
    lp = seq + BLOCK
    m = b * lp
    row = lambda t: t.reshape(1, -1).astype(F32)

    meta = jnp.broadcast_to(meta_tokens.astype(x.dtype)[None], (b, N_META, d))
    h = jnp.concatenate([jnp.zeros((b, PAD, d), x.dtype), meta, x], axis=1).reshape(m, d)

    q_dim = N_HEADS * HEAD_DIM
    e_dim = attn_w_qkv.shape[2]
    col_scale = jnp.where(jnp.arange(e_dim) < q_dim, ATTN_SCALE, 1.0).astype(F32)
    qkv = _qkv_proj(h, attn_w_qkv[0].astype(BF16), row(attn_b_qkv[0]), row(col_scale))
    bias_band, bias_meta = _bias_tables(rel_bias)
    o = _attention(qkv.reshape(b, lp, e_dim), bias_band, attn_sinks[0].astype(F32), bias_meta)
    h = _proj_res_ln(o.reshape(m, q_dim), attn_w_o[0].astype(BF16), h, row(ln_mix_g[0]), row(ln_mix_b[0]))
    h = _ffn(h, ffn_w_gu[0].astype(BF16), ffn_w_down[0].astype(BF16), row(ln_ffn_g[0]), row(ln_ffn_b[0]), lp, b)

    r, k, v, a, ld, g = _rwkv_proj(
        h, rwkv_mu[0], rwkv_w_rkv[0].astype(BF16), row(rwkv_w0[0]), rwkv_w1[0].astype(BF16),
        rwkv_w2[0].astype(BF16), row(rwkv_a0[0]), rwkv_a1[0].astype(BF16), rwkv_a2[0].astype(BF16),
        rwkv_g1[0].astype(BF16), rwkv_g2[0].astype(BF16), lp, b)
    t3 = lambda t: t.reshape(b, lp, d)
    y = _wkv(t3(r), t3(k), t3(v), t3(a), t3(ld), row(rwkv_k_k[0]), row(rwkv_k_a[0]), row(rwkv_r_k[0]),
             row(rwkv_lnx_g[0]), row(rwkv_lnx_b[0]))
    w_router = jnp.pad(moe_router[0].astype(F32), ((0, 0), (0, LANES - N_EXPERTS)))
    h, routed = _proj_res_ln_route(y.reshape(m, d), g, rwkv_w_o[0].astype(BF16), h, row(ln_mix_g[1]),
                                   row(ln_mix_b[1]), w_router)
    expert_idx = routed[:, :2].astype(jnp.int32)
    y_slots = _moe_experts(h, expert_idx, moe_w_gu[0].astype(BF16), moe_w_down[0].astype(BF16))
    return _moe_combine(h, y_slots, routed, row(ln_ffn_g[1]), row(ln_ffn_b[1]), b, lp)
```

```python
import functools
import math

import jax
import jax.numpy as jnp
from jax import lax
from jax.experimental import pallas as pl
from jax.experimental.pallas import tpu as pltpu

F32 = jnp.float32
BF16 = jnp.bfloat16

N_META = 16
N_HEADS = 16
N_KV_HEADS = 4
HEAD_DIM = 64
GROUP = N_HEADS // N_KV_HEADS
WINDOW = 128
BLOCK = 128
PAD = BLOCK - N_META
ATTN_SCALE = 1.0 / math.sqrt(HEAD_DIM)
NEG_INF = -1e30
NUM_BUCKETS = 32
MAX_EXACT = NUM_BUCKETS // 2
MAX_DISTANCE = 128
RWKV_HEAD = 64
GN_EPS = 64e-5
N_EXPERTS = 8
DEPTH = 2
ALPHA = (2 * DEPTH) ** 0.25
LN_EPS = 1e-5

WKV_CHUNK = 64
WKV_BLOCK_TOKENS = 384
WKV_BLOCK_PAIRS = 2
LANES = 128
DMA_ISSUE_UNROLL = 8
VMEM_LIMIT_BYTES = 56 * 1024 * 1024


def _params(*sem, flags=None):
    return pltpu.CompilerParams(dimension_semantics=sem, vmem_limit_bytes=VMEM_LIMIT_BYTES, flags=flags)


def _pick(n, candidates):
    for c in candidates:
        if n % c == 0:
            return c
    raise ValueError(f"no tile in {candidates} divides {n}")


def _pad_row_mask(first_row, rows, lp, nbatch):
    r = first_row + lax.broadcasted_iota(jnp.int32, (rows, 1), 0)
    mask = r < PAD
    for bi in range(1, nbatch):
        mask = jnp.logical_or(mask, jnp.logical_and(r >= bi * lp, r < bi * lp + PAD))
    return mask


def _layer_norm(z, g, b):
    mu = jnp.mean(z, axis=-1, keepdims=True)
    zc = z - mu
    var = jnp.mean(zc * zc, axis=-1, keepdims=True)
    return zc * lax.rsqrt(var + LN_EPS) * g + b


def _dot(a, b):
    return jnp.dot(a.astype(BF16), b.astype(BF16), preferred_element_type=F32)


def _dot_nt(a, b):
    return lax.dot_general(a.astype(BF16), b.astype(BF16), (((1,), (1,)), ((), ())),
                           preferred_element_type=F32)


def _qkv_kernel(x_ref, w_ref, b_ref, s_ref, o_ref):
    acc = _dot(x_ref[...], w_ref[...])
    o_ref[...] = ((acc + b_ref[...]) * s_ref[...]).astype(o_ref.dtype)


def _qkv_proj(x, w, b, s):
    m, d = x.shape
    n = w.shape[1]
    tm = _pick(m, (768, 384, 256, 128))
    return pl.pallas_call(
        _qkv_kernel,
        grid=(m // tm,),
        in_specs=[pl.BlockSpec((tm, d), lambda i: (i, 0)),
                  pl.BlockSpec((d, n), lambda i: (0, 0)),
                  pl.BlockSpec((1, n), lambda i: (0, 0)),
                  pl.BlockSpec((1, n), lambda i: (0, 0))],
        out_specs=pl.BlockSpec((tm, n), lambda i: (i, 0)),
        out_shape=jax.ShapeDtypeStruct((m, n), BF16),
        compiler_params=_params("parallel"),
        name="qkv_proj",
    )(x, w, b, s)


def _attn_kernel(sink_ref, cur_ref, prev_ref, meta_ref, bias_ref, mbias_ref, o_ref):
    q_dim = N_HEADS * HEAD_DIM
    kv_dim = N_KV_HEADS * HEAD_DIM
    cur = cur_ref[0]
    prev = prev_ref[0]
    meta = meta_ref[0]

    for h in range(N_KV_HEADS):
        ks = slice(h * HEAD_DIM, (h + 1) * HEAD_DIM)
        vs = slice(kv_dim + h * HEAD_DIM, kv_dim + (h + 1) * HEAD_DIM)
        kb = jnp.concatenate([prev[:, ks], cur[:, q_dim + h * HEAD_DIM:q_dim + (h + 1) * HEAD_DIM]], axis=0)
        vb = jnp.concatenate([prev[:, vs], cur[:, q_dim + kv_dim + h * HEAD_DIM:
                                               q_dim + kv_dim + (h + 1) * HEAD_DIM]], axis=0)
        km = meta[:, ks]
        vm = meta[:, vs]
        q4 = jnp.concatenate(
            [cur[:, (h * GROUP + g) * HEAD_DIM:(h * GROUP + g + 1) * HEAD_DIM] for g in range(GROUP)], axis=0)

        bias4 = bias_ref[0, h * GROUP:(h + 1) * GROUP].reshape(GROUP * BLOCK, 2 * BLOCK)
        mbias4 = mbias_ref[0, h * GROUP:(h + 1) * GROUP].reshape(GROUP * BLOCK, BLOCK)
        s = _dot_nt(q4, kb) + bias4
        sm = _dot_nt(q4, km) + mbias4
        ps, pms, denoms = [], [], []
        for g in range(GROUP):
            rows = slice(g * BLOCK, (g + 1) * BLOCK)
            sink = sink_ref[h * GROUP + g]
            s0, s1, s2 = s[rows, :BLOCK], s[rows, BLOCK:], sm[rows]
            mx = jnp.maximum(jnp.max(jnp.maximum(jnp.maximum(s0, s1), s2), axis=-1, keepdims=True), sink)
            p0, p1, pm = jnp.exp(s0 - mx), jnp.exp(s1 - mx), jnp.exp(s2 - mx)
            denoms.append(jnp.sum(p0 + p1 + pm, axis=-1, keepdims=True) + jnp.exp(sink - mx))
            ps.append(jnp.concatenate([p0, p1], axis=1).astype(BF16))
            pms.append(pm.astype(BF16))
        o4 = _dot(jnp.concatenate(ps, axis=0), vb) + _dot(jnp.concatenate(pms, axis=0), vm)
        for g in range(GROUP):
            hd = h * GROUP + g
            o_ref[0, :, hd * HEAD_DIM:(hd + 1) * HEAD_DIM] = (
                o4[g * BLOCK:(g + 1) * BLOCK] / denoms[g]).astype(o_ref.dtype)


def _attention(qkv, bias_tbl, meta_tbl, sinks):
    b, lp, e = qkv.shape
    nb = lp // BLOCK
    q_dim = N_HEADS * HEAD_DIM
    kv2 = 2 * N_KV_HEADS * HEAD_DIM
    kv_blk = q_dim // kv2
    smem = pl.BlockSpec(memory_space=pltpu.SMEM)
    cls = lambda i, j: (jnp.minimum(j, 2), 0, 0, 0)
    return pl.pallas_call(
        _attn_kernel,
        grid=(b, nb),
        in_specs=[smem,
                  pl.BlockSpec((1, BLOCK, e), lambda i, j: (i, j, 0)),
                  pl.BlockSpec((1, BLOCK, kv2), lambda i, j: (i, jnp.maximum(j - 1, 0), kv_blk)),
                  pl.BlockSpec((1, BLOCK, kv2), lambda i, j: (i, 0, kv_blk)),
                  pl.BlockSpec((1, N_HEADS, BLOCK, 2 * BLOCK), cls),
                  pl.BlockSpec((1, N_HEADS, BLOCK, BLOCK), cls)],
        out_specs=pl.BlockSpec((1, BLOCK, q_dim), lambda i, j: (i, j, 0)),
        out_shape=jax.ShapeDtypeStruct((b, lp, q_dim), BF16),
        compiler_params=_params("parallel", "parallel"),
        name="swa_attention",
    )(sinks, qkv, qkv, qkv, bias_tbl, meta_tbl)


def _proj_ln_kernel(a_ref, w_ref, res_ref, g_ref, b_ref, o_ref):
    y = _dot(a_ref[...], w_ref[...])
    o_ref[...] = _layer_norm(ALPHA * res_ref[...] + y, g_ref[...], b_ref[...])


def _proj_res_ln(a, w, res, g, b):
    m, k = a.shape
    d = w.shape[1]
    tm = _pick(m, (768, 384, 256, 128))
    row = lambda i: (i, 0)
    fixed = lambda i: (0, 0)
    return pl.pallas_call(
        _proj_ln_kernel,
        grid=(m // tm,),
        in_specs=[pl.BlockSpec((tm, k), row), pl.BlockSpec((k, d), fixed), pl.BlockSpec((tm, d), row),
                  pl.BlockSpec((1, d), fixed), pl.BlockSpec((1, d), fixed)],
        out_specs=pl.BlockSpec((tm, d), row),
        out_shape=jax.ShapeDtypeStruct((m, d), F32),
        compiler_params=_params("parallel"),
        name="proj_res_ln",
    )(a, w, res, g, b)


def _split_bf16(x):
    hi = x.astype(BF16)
    return hi, (x - hi.astype(F32)).astype(BF16)


def _route(h, w_router):
    h_hi, h_lo = _split_bf16(h)
    w_hi, w_lo = _split_bf16(w_router)
    logits = (jnp.dot(h_hi, w_hi, preferred_element_type=F32) + jnp.dot(h_hi, w_lo, preferred_element_type=F32)
              + jnp.dot(h_lo, w_hi, preferred_element_type=F32))
    lane = lax.broadcasted_iota(jnp.int32, logits.shape, 1).astype(F32)
    neg = -jnp.inf
    lg = jnp.where(lane < N_EXPERTS, logits, neg)
    m1 = jnp.max(lg, axis=-1, keepdims=True)
    i1 = jnp.min(jnp.where(lg == m1, lane, float(LANES)), axis=-1, keepdims=True)
    lg2 = jnp.where(lane == i1, neg, lg)
    m2 = jnp.max(lg2, axis=-1, keepdims=True)
    i2 = jnp.min(jnp.where(lg2 == m2, lane, float(LANES)), axis=-1, keepdims=True)
    e2 = jnp.exp(m2 - m1)
    den = 1.0 + e2
    return jnp.where(lane == 0.0, i1, jnp.where(lane == 1.0, i2, jnp.where(
        lane == 2.0, 1.0 / den, jnp.where(lane == 3.0, e2 / den, 0.0))))


def _proj_ln_route_kernel(a_ref, m_ref, w_ref, res_ref, g_ref, b_ref, wr_ref, o_ref, ro_ref):
    a = a_ref[...].astype(F32) * m_ref[...].astype(F32)
    h = _layer_norm(ALPHA * res_ref[...] + _dot(a, w_ref[...]), g_ref[...], b_ref[...])
    o_ref[...] = h
    ro_ref[...] = _route(h, wr_ref[...])


def _proj_res_ln_route(a, mult, w, res, g, b, w_router):
    m, k = a.shape
    d = w.shape[1]
    tm = _pick(m, (768, 384, 256, 128))
    row = lambda i: (i, 0)
    fixed = lambda i: (0, 0)
    return pl.pallas_call(
        _proj_ln_route_kernel,
        grid=(m // tm,),
        in_specs=[pl.BlockSpec((tm, k), row), pl.BlockSpec((tm, k), row), pl.BlockSpec((k, d), fixed),
                  pl.BlockSpec((tm, d), row), pl.BlockSpec((1, d), fixed), pl.BlockSpec((1, d), fixed),
                  pl.BlockSpec((d, LANES), fixed)],
        out_specs=[pl.BlockSpec((tm, d), row), pl.BlockSpec((tm, LANES), row)],
        out_shape=[jax.ShapeDtypeStruct((m, d), F32), jax.ShapeDtypeStruct((m, LANES), F32)],
        compiler_params=_params("parallel"),
        name="proj_res_ln_route",
    )(a, mult, w, res, g, b, w_router)


def _silu_mul(gate, up):
    return gate * (1.0 / (1.0 + jnp.exp(-gate))) * up


def _ffn_kernel(x_ref, wg_ref, wu_ref, wd_ref, g_ref, b_ref, o_ref, acc_ref, xb_ref, *, lp, nbatch, tm):
    i = pl.program_id(0)
    f = pl.program_id(1)

    @pl.when(f == 0)
    def _():
        acc_ref[...] = jnp.zeros_like(acc_ref)
        xb_ref[...] = x_ref[...].astype(BF16)

    xb = xb_ref[...]
    gate = jnp.dot(xb, wg_ref[...], preferred_element_type=F32)
    up = jnp.dot(xb, wu_ref[...], preferred_element_type=F32)
    acc_ref[...] += jnp.dot(_silu_mul(gate, up).astype(BF16), wd_ref[...], preferred_element_type=F32)

    @pl.when(f == pl.num_programs(1) - 1)
    def _():
        y = _layer_norm(ALPHA * x_ref[...] + acc_ref[...], g_ref[...], b_ref[...])
        o_ref[...] = jnp.where(_pad_row_mask(i * tm, tm, lp, nbatch), 0.0, y)


def _ffn(x, w_gu, w_down, g, b, lp, nbatch):
    m, d = x.shape
    ff = w_down.shape[0]
    tm = _pick(m, (768, 384, 256, 128))
    tf = _pick(ff, (1408, 896, 512, 256, 128))
    nf = ff // tf
    row = lambda i, f: (i, 0)
    fixed = lambda i, f: (0, 0)
    return pl.pallas_call(
        functools.partial(_ffn_kernel, lp=lp, nbatch=nbatch, tm=tm),
        grid=(m // tm, nf),
        in_specs=[pl.BlockSpec((tm, d), row),
                  pl.BlockSpec((d, tf), lambda i, f: (0, f)),
                  pl.BlockSpec((d, tf), lambda i, f: (0, f + nf)),
                  pl.BlockSpec((tf, d), lambda i, f: (f, 0)),
                  pl.BlockSpec((1, d), fixed), pl.BlockSpec((1, d), fixed)],
        out_specs=pl.BlockSpec((tm, d), row),
        out_shape=jax.ShapeDtypeStruct((m, d), F32),
        scratch_shapes=[pltpu.VMEM((tm, d), F32), pltpu.VMEM((tm, d), BF16)],
        compiler_params=_params("parallel", "arbitrary"),
        name="swiglu_res_ln",
    )(x, w_gu, w_gu, w_down, g, b)


def _moe_kernel(tile_e_ref, nact_ref, src_ref, src_next_ref, dst_ref, dst_prev_ref, x_hbm, wg_ref, wu_ref,
                wd_ref, y_hbm, xbuf, xb_ref, acc_ref, ybuf, gsem, ssem, *, tm, nf):
    j = pl.program_id(0)
    f = pl.program_id(1)
    last_tile = j == pl.num_programs(0) - 1
    active = j < nact_ref[0]
    slot = lax.rem(j, 2)
    other = 1 - slot
    rows = tm // nf

    def gather_row(idx_ref, s, i):
        t = idx_ref[0, 0, i]
        pltpu.make_async_copy(x_hbm.at[pl.ds(t, 1)], xbuf.at[s, pl.ds(i, 1)], gsem.at[s]).start()

    def scatter_row(idx_ref, s, i):
        t = idx_ref[0, 0, i]
        pltpu.make_async_copy(ybuf.at[s, pl.ds(i, 1)], y_hbm.at[pl.ds(t, 1)], ssem.at[s]).start()

    def gather_wait(s):
        pltpu.make_async_copy(x_hbm.at[pl.ds(0, tm)], xbuf.at[s], gsem.at[s]).wait()

    def scatter_wait(s):
        pltpu.make_async_copy(ybuf.at[s], y_hbm.at[pl.ds(0, tm)], ssem.at[s]).wait()

    def issue_step_dmas():
        for i in range(rows):
            gather_row(src_next_ref, other, f * rows + i)
            scatter_row(dst_prev_ref, other, f * rows + i)

    @pl.when(f == 0)
    def _():
        @pl.when(j == 0)
        def _():
            ybuf[...] = jnp.zeros_like(ybuf)

            def body(i, carry):
                gather_row(src_ref, 0, i)
                return carry
            lax.fori_loop(0, tm, body, 0, unroll=DMA_ISSUE_UNROLL)

        gather_wait(slot)
        xb_ref[...] = xbuf[slot].astype(BF16)
        acc_ref[...] = jnp.zeros_like(acc_ref)

    @pl.when(active)
    def _():
        issue_step_dmas()
        xb = xb_ref[...]
        gate = jnp.dot(xb, wg_ref[0], preferred_element_type=F32)
        up = jnp.dot(xb, wu_ref[0], preferred_element_type=F32)
        acc_ref[...] += jnp.dot(_silu_mul(gate, up).astype(BF16), wd_ref[0], preferred_element_type=F32)

    @pl.when(jnp.logical_not(active))
    def _():
        issue_step_dmas()

    @pl.when(f == nf - 1)
    def _():
        @pl.when(j >= 1)
        def _():
            scatter_wait(slot)

        ybuf[slot] = acc_ref[...]

        @pl.when(last_tile)
        def _():
            def body(i, carry):
                scatter_row(dst_ref, slot, i)
                return carry
            lax.fori_loop(0, tm, body, 0, unroll=DMA_ISSUE_UNROLL)
            scatter_wait(slot)
            scatter_wait(other)
            gather_wait(other)


def _moe_experts(x, expert_idx, w_gu, w_down):
    m, d = x.shape
    ne, ff = w_down.shape[0], w_down.shape[1]
    nslots = 2 * m
    tm = _pick(nslots, (512, 256, 128))
    tf = _pick(ff, (1792, 896, 512, 256, 128))
    nf = ff // tf
    n_tiles = nslots // tm + ne
    p_rows = n_tiles * tm

    e_flat = expert_idx.reshape(nslots)
    slot_ids = jnp.arange(nslots, dtype=jnp.int32)
    _, order = lax.sort_key_val(e_flat, slot_ids)
    counts = jnp.sum((e_flat[:, None] == jnp.arange(ne, dtype=jnp.int32)[None, :]).astype(jnp.int32), axis=0)
    gsize = ((counts + tm - 1) // tm) * tm
    gend = jnp.cumsum(gsize)
    tile_e = jnp.minimum(jnp.searchsorted(gend, jnp.arange(n_tiles, dtype=jnp.int32) * tm, side="right"),
                         ne - 1).astype(jnp.int32)
    pos = jnp.arange(p_rows, dtype=jnp.int32)
    pos_e = jnp.repeat(tile_e, tm)
    rank = pos - (gend - gsize)[pos_e]
    is_fill = rank >= counts[pos_e]
    slot_sorted = order[jnp.clip((jnp.cumsum(counts) - counts)[pos_e] + rank, 0, nslots - 1)]
    fill_rank = jnp.cumsum(is_fill.astype(jnp.int32)) - 1
    src_sorted = jnp.where(is_fill, 0, slot_sorted // 2)
    dst_sorted = jnp.where(is_fill, nslots + fill_rank, (slot_sorted % 2) * m + slot_sorted // 2)
    nact = (gend[-1:] // tm).astype(jnp.int32)
    src3 = src_sorted.reshape(n_tiles, 1, tm)
    dst3 = dst_sorted.reshape(n_tiles, 1, tm)

    def f_eff(j, f, nact_ref):
        return jnp.where(j < nact_ref[0], f, nf - 1)

    smem_blk = lambda imap: pl.BlockSpec((1, 1, tm), imap, memory_space=pltpu.SMEM)
    grid_spec = pltpu.PrefetchScalarGridSpec(
        num_scalar_prefetch=2,
        grid=(n_tiles, nf),
        in_specs=[smem_blk(lambda j, f, te, na: (j, 0, 0)),
                  smem_blk(lambda j, f, te, na: (jnp.minimum(j + 1, n_tiles - 1), 0, 0)),
                  smem_blk(lambda j, f, te, na: (j, 0, 0)),
                  smem_blk(lambda j, f, te, na: (jnp.where(j == 0, n_tiles - 1, j - 1), 0, 0)),
                  pl.BlockSpec(memory_space=pl.ANY),
                  pl.BlockSpec((1, d, tf), lambda j, f, te, na: (te[j], 0, f_eff(j, f, na))),
                  pl.BlockSpec((1, d, tf), lambda j, f, te, na: (te[j], 0, f_eff(j, f, na) + nf)),
                  pl.BlockSpec((1, tf, d), lambda j, f, te, na: (te[j], f_eff(j, f, na), 0))],
        out_specs=pl.BlockSpec(memory_space=pl.ANY),
        scratch_shapes=[pltpu.VMEM((2, tm, d), F32), pltpu.VMEM((tm, d), BF16), pltpu.VMEM((tm, d), F32),
                        pltpu.VMEM((2, tm, d), F32), pltpu.SemaphoreType.DMA((2,)),
                        pltpu.SemaphoreType.DMA((2,))],
    )
    return pl.pallas_call(
        functools.partial(_moe_kernel, tm=tm, nf=nf),
        grid_spec=grid_spec,
        out_shape=jax.ShapeDtypeStruct((p_rows, d), F32),
        compiler_params=_params("arbitrary", "arbitrary"),
        name="moe_grouped_swiglu",
    )(tile_e, nact, src3, src3, dst3, dst3, x, w_gu, w_gu, w_down)


def _combine_kernel(x_ref, y1_ref, y2_ref, ro_ref, g_ref, b_ref, o_ref):
    ro = ro_ref[...]
    lane = lax.broadcasted_iota(jnp.int32, ro.shape, 1)
    w1 = jnp.sum(jnp.where(lane == 2, ro, 0.0), axis=-1, keepdims=True)
    w2 = jnp.sum(jnp.where(lane == 3, ro, 0.0), axis=-1, keepdims=True)
    z = ALPHA * x_ref[...] + w1 * y1_ref[...] + w2 * y2_ref[...]
    o_ref[0] = _layer_norm(z, g_ref[...], b_ref[...])


def _moe_combine(x, y_slots, router_out, g, b, nbatch, lp):
    m, d = x.shape
    tm = BLOCK
    nb = lp // tm
    choice2 = m // tm
    row = lambda i, j: (i * nb + j + 1, 0)
    fixed = lambda i, j: (0, 0)
    return pl.pallas_call(
        _combine_kernel,
        grid=(nbatch, nb - 1),
        in_specs=[pl.BlockSpec((tm, d), row), pl.BlockSpec((tm, d), row),
                  pl.BlockSpec((tm, d), lambda i, j: (choice2 + i * nb + j + 1, 0)),
                  pl.BlockSpec((tm, LANES), row),
                  pl.BlockSpec((1, d), fixed), pl.BlockSpec((1, d), fixed)],
        out_specs=pl.BlockSpec((1, tm, d), lambda i, j: (i, j, 0)),
        out_shape=jax.ShapeDtypeStruct((nbatch, lp - tm, d), F32),
        compiler_params=_params("parallel", "parallel"),
        name="moe_combine_ln",
    )(x, y_slots, y_slots, router_out, g, b)


def _rwkv_proj_kernel(x_ref, xp_ref, mu_ref, wr_ref, wk_ref, wv_ref, w0_ref, w1_ref, w2_ref,
                      a0_ref, a1_ref, a2_ref, g1_ref, g2_ref,
                      r_ref, k_ref, v_ref, a_ref, ld_ref, g_ref, *, lp, nbatch):
    x = x_ref[...]
    tm = x.shape[0]
    rolled = pltpu.roll(x, 1, axis=0)
    prev_row = xp_ref[7:8, :]
    first = lax.broadcasted_iota(jnp.int32, (tm, 1), 0) == 0
    xx = jnp.where(first, prev_row, rolled) - x
    xx = jnp.where(_pad_row_mask(pl.program_id(0) * tm, tm, lp, nbatch), 0.0, xx)

    def mix(i):
        return x + xx * mu_ref[i:i + 1, :]

    r_ref[...] = _dot(mix(0), wr_ref[...]).astype(r_ref.dtype)
    k_ref[...] = _dot(mix(2), wk_ref[...]).astype(k_ref.dtype)
    v_ref[...] = _dot(mix(3), wv_ref[...]).astype(v_ref.dtype)
    wl = w0_ref[...] + _dot(jnp.tanh(_dot(mix(1), w1_ref[...])), w2_ref[...])
    z = -wl
    softplus = jnp.maximum(z, 0.0) + jnp.log(1.0 + jnp.exp(-jnp.abs(z)))
    ld_ref[...] = -jnp.exp(-softplus - 0.5)
    al = a0_ref[...] + _dot(_dot(mix(4), a1_ref[...]), a2_ref[...])
    a_ref[...] = (1.0 / (1.0 + jnp.exp(-al))).astype(a_ref.dtype)
    gl = _dot(mix(5), g1_ref[...])
    g_ref[...] = _dot(1.0 / (1.0 + jnp.exp(-gl)), g2_ref[...]).astype(g_ref.dtype)


def _rwkv_proj(x, mu, w_rkv, w0, w1, w2, a0, a1, a2, g1, g2, lp, nbatch):
    m, d = x.shape
    tm = _pick(m, (768, 384, 256, 128))
    row = lambda i: (i, 0)
    fixed = lambda i: (0, 0)
    full = lambda arr: pl.BlockSpec(arr.shape, fixed)
    out = lambda dt: jax.ShapeDtypeStruct((m, d), dt)
    return pl.pallas_call(
        functools.partial(_rwkv_proj_kernel, lp=lp, nbatch=nbatch),
        grid=(m // tm,),
        in_specs=[pl.BlockSpec((tm, d), row),
                  pl.BlockSpec((8, d), lambda i: (jnp.maximum(i * (tm // 8) - 1, 0), 0)),
                  full(mu), full(w_rkv[0]), full(w_rkv[1]), full(w_rkv[2]),
                  full(w0), full(w1), full(w2), full(a0), full(a1), full(a2), full(g1), full(g2)],
        out_specs=[pl.BlockSpec((tm, d), row)] * 6,
        out_shape=[out(BF16), out(BF16), out(BF16), out(BF16), out(F32), out(BF16)],
        compiler_params=_params("parallel"),
        name="rwkv_proj",
    )(x, x, mu, w_rkv[0], w_rkv[1], w_rkv[2], w0, w1, w2, a0, a1, a2, g1, g2)


def _bmm(a, b):
    return jnp.einsum("cij,cjk->cik", a.astype(BF16), b.astype(BF16), preferred_element_type=F32)


def _bmm_nt(a, b):
    return jnp.einsum("cik,cjk->cij", a.astype(BF16), b.astype(BF16), preferred_element_type=F32)


def _bmm_tn(a, b):
    return jnp.einsum("cki,ckj->cij", a.astype(BF16), b.astype(BF16), preferred_element_type=F32)


def _wkv_streams(r, k, v, a, ld, k_k, k_a, r_k, gn_g, gn_b, z):
    c = WKV_CHUNK
    n = RWKV_HEAD
    ns, tb, _ = r.shape
    nc = tb // c
    r, k, v, a = (t.astype(F32) for t in (r, k, v, a))
    head0 = lax.broadcasted_iota(jnp.int32, (1, 1, LANES), 2) < n

    def head_sum(x):
        s0 = jnp.sum(jnp.where(head0, x, 0.0), axis=-1, keepdims=True)
        s1 = jnp.sum(jnp.where(head0, 0.0, x), axis=-1, keepdims=True)
        return jnp.where(head0, s0, s1)

    def stack(x):
        xb = x.astype(BF16)
        zero = jnp.zeros_like(xb)
        return jnp.concatenate([jnp.where(head0, xb, zero), jnp.where(head0, zero, xb)], axis=1)

    def fold(x):
        return x[:, :c] + x[:, c:]

    kkr = k * k_k
    kk = kkr / jnp.maximum(jnp.sqrt(head_sum(kkr * kkr)), 1e-12)
    km = k * (1.0 + (a - 1.0) * k_a)
    bv = kk * a

    tpos = jnp.bitwise_and(lax.broadcasted_iota(jnp.int32, (ns * tb, 1), 0), c - 1)
    cs = ld.reshape(ns * tb, LANES)
    shift = 1
    while shift < c:
        cs = cs + jnp.where(tpos >= shift, pltpu.roll(cs, shift, axis=0), 0.0)
        shift *= 2

    to3 = lambda x: x.reshape(ns * nc, c, LANES)
    cs3, ld3 = to3(cs), to3(ld)
    cs_last = cs3[:, c - 1:c, :]
    e_neg = jnp.exp(-cs3)
    e_tail = jnp.exp(cs_last - cs3)
    at_s = stack(-to3(kk) * jnp.exp(cs3 - ld3))
    rt = to3(r) * jnp.exp(cs3)
    rt_s = stack(rt)
    bt_s = stack(to3(bv) * e_neg)
    kt_s = stack(to3(km) * e_neg)
    bh_s = stack(to3(bv) * e_tail)
    kh_s = stack(to3(km) * e_tail)
    v_s = stack(to3(v))
    dec = jnp.exp(cs_last)

    c2 = 2 * c
    g = _bmm_nt(jnp.concatenate([at_s, rt_s], axis=1), jnp.concatenate([bt_s, kt_s], axis=1))
    ti = jnp.bitwise_and(lax.broadcasted_iota(jnp.int32, (1, c2, c2), 1), c - 1)
    si = jnp.bitwise_and(lax.broadcasted_iota(jnp.int32, (1, c2, c2), 2), c - 1)
    strict = ti > si
    incl = ti >= si
    lab = jnp.where(strict, g[:, :c2, :c2], 0.0)
    lak = jnp.where(strict, g[:, :c2, c2:], 0.0)
    mrb = jnp.where(incl, g[:, c2:, :c2], 0.0)
    mrk = jnp.where(incl, g[:, c2:, c2:], 0.0)

    eye2 = lax.broadcasted_iota(jnp.int32, (1, c2, c2), 1) == lax.broadcasted_iota(jnp.int32, (1, c2, c2), 2)
    t_inv = jnp.where(eye2, 1.0, lab)
    lpow = lab
    for _ in range(int(math.log2(c)) - 1):
        lpow = _bmm(lpow, lpow)
        t_inv = t_inv + _bmm(t_inv, lpow)

    lakv = _bmm(lak, v_s)
    wu = _bmm(t_inv, jnp.concatenate([at_s, lakv.astype(BF16)], axis=2))
    qy = _bmm(mrb, wu)
    q = rt + fold(qy[:, :, :LANES])
    yp = fold(qy[:, :, LANES:] + _bmm(mrk, v_s))
    eye_l = (lax.broadcasted_iota(jnp.int32, (1, LANES, LANES), 1)
             == lax.broadcasted_iota(jnp.int32, (1, LANES, LANES), 2))
    a_t = jnp.where(eye_l, dec, 0.0) + _bmm_tn(bh_s, wu[:, :, :LANES])
    g_t = _bmm_tn(jnp.concatenate([bh_s, kh_s], axis=1),
                  jnp.concatenate([wu[:, :, LANES:].astype(BF16), v_s], axis=1))

    per_stream = lambda x: x.reshape(ns, nc, *x.shape[1:])
    q, yp, a_t, g_t = per_stream(q), per_stream(yp), per_stream(a_t), per_stream(g_t)
    ys = []
    for ci in range(nc):
        ys.append(yp[:, ci] + _bmm(q[:, ci], z))
        z = _bmm(a_t[:, ci], z) + g_t[:, ci]
    y = jnp.concatenate(ys, axis=1)

    mu = head_sum(y) * (1.0 / n)
    yc = y - mu
    var = head_sum(yc * yc) * (1.0 / n)
    yn = yc * lax.rsqrt(var + GN_EPS) * gn_g + gn_b
    bonus = head_sum(r * km * r_k) * v
    return yn + bonus, z


def _wkv_kernel(r_ref, k_ref, v_ref, a_ref, ld_ref, kk_ref, ka_ref, rk_ref, gg_ref, gb_ref,
                y_ref, z_ref):
    @pl.when(pl.program_id(1) == 0)
    def _():
        z_ref[...] = jnp.zeros_like(z_ref)

    nbatch, npairs = r_ref.shape[0], r_ref.shape[2] // LANES
    streams = [(bi, slice(pi * LANES, (pi + 1) * LANES)) for bi in range(nbatch) for pi in range(npairs)]
    tok = lambda ref: jnp.stack([ref[bi, :, ls] for bi, ls in streams])
    par = lambda ref: jnp.stack([ref[:, ls] for _, ls in streams])
    y, z = _wkv_streams(tok(r_ref), tok(k_ref), tok(v_ref), tok(a_ref), tok(ld_ref), par(kk_ref), par(ka_ref),
                        par(rk_ref), par(gg_ref), par(gb_ref), z_ref[...])
    for si, (bi, ls) in enumerate(streams):
        y_ref[bi, :, ls] = y[si]
    z_ref[...] = z


def _wkv(r, k, v, a, ld, k_k, k_a, r_k, gn_g, gn_b):
    b, lp, d = r.shape
    tb = _pick(lp, (WKV_BLOCK_TOKENS, 128, 64))
    width = WKV_BLOCK_PAIRS * LANES
    tok_spec = pl.BlockSpec((b, tb, width), lambda j, t: (0, t, j))
    par_spec = pl.BlockSpec((1, width), lambda j, t: (0, j))
    return pl.pallas_call(
        _wkv_kernel,
        grid=(d // width, lp // tb),
        in_specs=[tok_spec] * 5 + [par_spec] * 5,
        out_specs=tok_spec,
        out_shape=jax.ShapeDtypeStruct((b, lp, d), F32),
        scratch_shapes=[pltpu.VMEM((b * WKV_BLOCK_PAIRS, LANES, LANES), F32)],
        compiler_params=_params("parallel", "arbitrary"),
        name="wkv7_chunked",
    )(r, k, v, a, ld, k_k, k_a, r_k, gn_g, gn_b)


def _t5_bucket(dist):
    n = jnp.maximum(dist, 0)
    is_small = n < MAX_EXACT
    nf = jnp.maximum(n, 1).astype(F32)
    large = MAX_EXACT + (jnp.log(nf / MAX_EXACT) / math.log(MAX_DISTANCE / MAX_EXACT)
                         * (NUM_BUCKETS - MAX_EXACT)).astype(jnp.int32)
    large = jnp.minimum(large, NUM_BUCKETS - 1)
    return jnp.where(is_small, n, large)


def _bias_tables(rel_bias):
    rb = rel_bias.astype(F32)
    d_band = BLOCK + jnp.arange(BLOCK)[:, None] - jnp.arange(2 * BLOCK)[None, :]
    ok = (d_band >= 0) & (d_band < WINDOW)
    onehot = (_t5_bucket(d_band)[..., None] == jnp.arange(NUM_BUCKETS)).astype(F32)
    looked_up = jnp.einsum("qkn,nh->hqk", onehot, rb, precision=lax.Precision.HIGHEST)
    band = jnp.where(ok[None], looked_up, NEG_INF)
    col = jnp.arange(2 * BLOCK)
    band3 = jnp.stack([jnp.where((col + (n - 1) * BLOCK >= PAD)[None, None, :], band, NEG_INF) for n in range(3)])
    meta = rb[_t5_bucket(jnp.array(WINDOW))]
    mcol = jnp.arange(BLOCK)[None, :] - PAD
    d_meta = lambda n: n * BLOCK + jnp.arange(BLOCK)[:, None] - PAD - mcol
    meta3 = jnp.stack([jnp.where(((mcol >= 0) & (d_meta(n) >= WINDOW))[None], meta[:, None, None], NEG_INF)
                       for n in range(3)])
    return band3, meta3


def kernel(x, meta_tokens, rel_bias, ln_mix_g, ln_mix_b, ln_ffn_g, ln_ffn_b, attn_w_qkv, attn_b_qkv, attn_sinks, attn_w_o, rwkv_mu, rwkv_w0, rwkv_w1, rwkv_w2, rwkv_a0, rwkv_a1, rwkv_a2, rwkv_g1, rwkv_g2, rwkv_k_k, rwkv_k_a, rwkv_r_k, rwkv_w_rkv, rwkv_lnx_g, rwkv_lnx_b, rwkv_w_o, ffn_w_gu, ffn_w_down, moe_router, moe_w_gu, moe_w_down):
    b, seq, d = x.shape
    lp = seq + BLOCK
    m = b * lp
    row = lambda t: t.reshape(1, -1).astype(F32)

    meta = jnp.broadcast_to(meta_tokens.astype(x.dtype)[None], (b, N_META, d))
    h = jnp.concatenate([jnp.zeros((b, PAD, d), x.dtype), meta, x], axis=1).reshape(m, d)

    q_dim = N_HEADS * HEAD_DIM
    e_dim = attn_w_qkv.shape[2]
    col_scale = jnp.where(jnp.arange(e_dim) < q_dim, ATTN_SCALE, 1.0).astype(F32)
    qkv = _qkv_proj(h, attn_w_qkv[0].astype(BF16), row(attn_b_qkv[0]), row(col_scale))
    bias_band, bias_meta = _bias_tables(rel_bias)
    o = _attention(qkv.reshape(b, lp, e_dim), bias_band, bias_meta, attn_sinks[0].astype(F32))
    h = _proj_res_ln(o.reshape(m, q_dim), attn_w_o[0].astype(BF16), h, row(ln_mix_g[0]), row(ln_mix_b[0]))
    h = _ffn(h, ffn_w_gu[0].astype(BF16), ffn_w_down[0].astype(BF16), row(ln_ffn_g[0]), row(ln_ffn_b[0]), lp, b)

    r, k, v, a, ld, g = _rwkv_proj(
        h, rwkv_mu[0], rwkv_w_rkv[0].astype(BF16), row(rwkv_w0[0]), rwkv_w1[0].astype(BF16),
        rwkv_w2[0].astype(BF16), row(rwkv_a0[0]), rwkv_a1[0].astype(BF16), rwkv_a2[0].astype(BF16),
        rwkv_g1[0].astype(BF16), rwkv_g2[0].astype(BF16), lp, b)
    t3 = lambda t: t.reshape(b, lp, d)
    y = _wkv(t3(r), t3(k), t3(v), t3(a), t3(ld), row(rwkv_k_k[0]), row(rwkv_k_a[0]), row(rwkv_r_k[0]),
             row(rwkv_lnx_g[0]), row(rwkv_lnx_b[0]))
    w_router = jnp.pad(moe_router[0].astype(F32), ((0, 0), (0, LANES - N_EXPERTS)))
    h, routed = _proj_res_ln_route(y.reshape(m, d), g, rwkv_w_o[0].astype(BF16), h, row(ln_mix_g[1]),
                                   row(ln_mix_b[1]), w_router)
    expert_idx = routed[:, :2].astype(jnp.int32)
    y_slots = _moe_experts(h, expert_idx, moe_w_gu[0].astype(BF16), moe_w_down[0].astype(BF16))
    return _moe_combine(h, y_slots, routed, row(ln_ffn_g[1]), row(ln_ffn_b[1]), b, lp)
```

```python
import functools
import math

import jax
import jax.numpy as jnp
from jax import lax
from jax.experimental import pallas as pl
from jax.experimental.pallas import tpu as pltpu

F32 = jnp.float32
BF16 = jnp.bfloat16

N_META = 16
N_HEADS = 16
N_KV_HEADS = 4
HEAD_DIM = 64
GROUP = N_HEADS // N_KV_HEADS
WINDOW = 128
BLOCK = 128
PAD = BLOCK - N_META
ATTN_SCALE = 1.0 / math.sqrt(HEAD_DIM)
NEG_INF = -1e30
NUM_BUCKETS = 32
MAX_EXACT = NUM_BUCKETS // 2
MAX_DISTANCE = 128
RWKV_HEAD = 64
GN_EPS = 64e-5
N_EXPERTS = 8
DEPTH = 2
ALPHA = (2 * DEPTH) ** 0.25
LN_EPS = 1e-5

WKV_CHUNK = 64
WKV_BLOCK_TOKENS = 384
WKV_BLOCK_PAIRS = 2
LANES = 128
DMA_ISSUE_UNROLL = 8
VMEM_LIMIT_BYTES = 56 * 1024 * 1024


def _params(*sem, flags=None):
    return pltpu.CompilerParams(dimension_semantics=sem, vmem_limit_bytes=VMEM_LIMIT_BYTES, flags=flags)


def _pick(n, candidates):
    for c in candidates:
        if n % c == 0:
            return c
    raise ValueError(f"no tile in {candidates} divides {n}")


def _pad_row_mask(first_row, rows, lp, nbatch):
    r = first_row + lax.broadcasted_iota(jnp.int32, (rows, 1), 0)
    mask = r < PAD
    for bi in range(1, nbatch):
        mask = jnp.logical_or(mask, jnp.logical_and(r >= bi * lp, r < bi * lp + PAD))
    return mask


def _layer_norm(z, g, b):
    mu = jnp.mean(z, axis=-1, keepdims=True)
    zc = z - mu
    var = jnp.mean(zc * zc, axis=-1, keepdims=True)
    return zc * lax.rsqrt(var + LN_EPS) * g + b


def _dot(a, b):
    return jnp.dot(a.astype(BF16), b.astype(BF16), preferred_element_type=F32)


def _dot_nt(a, b):
    return lax.dot_general(a.astype(BF16), b.astype(BF16), (((1,), (1,)), ((), ())),
                           preferred_element_type=F32)


def _qkv_kernel(x_ref, w_ref, b_ref, s_ref, o_ref):
    acc = _dot(x_ref[...], w_ref[...])
    o_ref[...] = ((acc + b_ref[...]) * s_ref[...]).astype(o_ref.dtype)


def _qkv_proj(x, w, b, s):
    m, d = x.shape
    n = w.shape[1]
    tm = _pick(m, (768, 384, 256, 128))
    return pl.pallas_call(
        _qkv_kernel,
        grid=(m // tm,),
        in_specs=[pl.BlockSpec((tm, d), lambda i: (i, 0)),
                  pl.BlockSpec((d, n), lambda i: (0, 0)),
                  pl.BlockSpec((1, n), lambda i: (0, 0)),
                  pl.BlockSpec((1, n), lambda i: (0, 0))],
        out_specs=pl.BlockSpec((tm, n), lambda i: (i, 0)),
        out_shape=jax.ShapeDtypeStruct((m, n), BF16),
        compiler_params=_params("parallel"),
        name="qkv_proj",
    )(x, w, b, s)


def _attn_kernel(sink_ref, cur_ref, prev_ref, meta_ref, bias_ref, mbias_ref, o_ref):
    q_dim = N_HEADS * HEAD_DIM
    kv_dim = N_KV_HEADS * HEAD_DIM
    cur = cur_ref[0]
    prev = prev_ref[0]
    meta = meta_ref[0]

    for h in range(N_KV_HEADS):
        ks = slice(h * HEAD_DIM, (h + 1) * HEAD_DIM)
        vs = slice(kv_dim + h * HEAD_DIM, kv_dim + (h + 1) * HEAD_DIM)
        kb = jnp.concatenate([prev[:, ks], cur[:, q_dim + h * HEAD_DIM:q_dim + (h + 1) * HEAD_DIM]], axis=0)
        vb = jnp.concatenate([prev[:, vs], cur[:, q_dim + kv_dim + h * HEAD_DIM:
                                               q_dim + kv_dim + (h + 1) * HEAD_DIM]], axis=0)
        km = meta[:, ks]
        vm = meta[:, vs]
        q4 = jnp.concatenate(
            [cur[:, (h * GROUP + g) * HEAD_DIM:(h * GROUP + g + 1) * HEAD_DIM] for g in range(GROUP)], axis=0)

        bias4 = bias_ref[0, h * GROUP:(h + 1) * GROUP].reshape(GROUP * BLOCK, 2 * BLOCK)
        mbias4 = mbias_ref[0, h * GROUP:(h + 1) * GROUP].reshape(GROUP * BLOCK, BLOCK)
        s = _dot_nt(q4, kb) + bias4
        sm = _dot_nt(q4, km) + mbias4
        ps, pms, denoms = [], [], []
        for g in range(GROUP):
            rows = slice(g * BLOCK, (g + 1) * BLOCK)
            sink = sink_ref[h * GROUP + g]
            s0, s1, s2 = s[rows, :BLOCK], s[rows, BLOCK:], sm[rows]
            mx = jnp.maximum(jnp.max(jnp.maximum(jnp.maximum(s0, s1), s2), axis=-1, keepdims=True), sink)
            p0, p1, pm = jnp.exp(s0 - mx), jnp.exp(s1 - mx), jnp.exp(s2 - mx)
            denoms.append(jnp.sum(p0 + p1 + pm, axis=-1, keepdims=True) + jnp.exp(sink - mx))
            ps.append(jnp.concatenate([p0, p1], axis=1).astype(BF16))
            pms.append(pm.astype(BF16))
        o4 = _dot(jnp.concatenate(ps, axis=0), vb) + _dot(jnp.concatenate(pms, axis=0), vm)
        for g in range(GROUP):
            hd = h * GROUP + g
            o_ref[0, :, hd * HEAD_DIM:(hd + 1) * HEAD_DIM] = (
                o4[g * BLOCK:(g + 1) * BLOCK] / denoms[g]).astype(o_ref.dtype)


def _attention(qkv, bias_tbl, meta_tbl, sinks):
    b, lp, e = qkv.shape
    nb = lp // BLOCK
    q_dim = N_HEADS * HEAD_DIM
    kv2 = 2 * N_KV_HEADS * HEAD_DIM
    kv_blk = q_dim // kv2
    smem = pl.BlockSpec(memory_space=pltpu.SMEM)
    cls = lambda i, j: (jnp.minimum(j, 2), 0, 0, 0)
    return pl.pallas_call(
        _attn_kernel,
        grid=(b, nb),
        in_specs=[smem,
                  pl.BlockSpec((1, BLOCK, e), lambda i, j: (i, j, 0)),
                  pl.BlockSpec((1, BLOCK, kv2), lambda i, j: (i, jnp.maximum(j - 1, 0), kv_blk)),
                  pl.BlockSpec((1, BLOCK, kv2), lambda i, j: (i, 0, kv_blk)),
                  pl.BlockSpec((1, N_HEADS, BLOCK, 2 * BLOCK), cls),
                  pl.BlockSpec((1, N_HEADS, BLOCK, BLOCK), cls)],
        out_specs=pl.BlockSpec((1, BLOCK, q_dim), lambda i, j: (i, j, 0)),
        out_shape=jax.ShapeDtypeStruct((b, lp, q_dim), BF16),
        compiler_params=_params("parallel", "parallel"),
        name="swa_attention",
    )(sinks, qkv, qkv, qkv, bias_tbl, meta_tbl)


def _proj_ln_kernel(a_ref, w_ref, res_ref, g_ref, b_ref, o_ref):
    y = _dot(a_ref[...], w_ref[...])
    o_ref[...] = _layer_norm(ALPHA * res_ref[...] + y, g_ref[...], b_ref[...])


def _proj_res_ln(a, w, res, g, b):
    m, k = a.shape
    d = w.shape[1]
    tm = _pick(m, (768, 384, 256, 128))
    row = lambda i: (i, 0)
    fixed = lambda i: (0, 0)
    return pl.pallas_call(
        _proj_ln_kernel,
        grid=(m // tm,),
        in_specs=[pl.BlockSpec((tm, k), row), pl.BlockSpec((k, d), fixed), pl.BlockSpec((tm, d), row),
                  pl.BlockSpec((1, d), fixed), pl.BlockSpec((1, d), fixed)],
        out_specs=pl.BlockSpec((tm, d), row),
        out_shape=jax.ShapeDtypeStruct((m, d), F32),
        compiler_params=_params("parallel"),
        name="proj_res_ln",
    )(a, w, res, g, b)


def _split_bf16(x):
    hi = x.astype(BF16)
    return hi, (x - hi.astype(F32)).astype(BF16)


def _route(h, w_router):
    h_hi, h_lo = _split_bf16(h)
    w_hi, w_lo = _split_bf16(w_router)
    logits = (jnp.dot(h_hi, w_hi, preferred_element_type=F32) + jnp.dot(h_hi, w_lo, preferred_element_type=F32)
              + jnp.dot(h_lo, w_hi, preferred_element_type=F32))
    lane = lax.broadcasted_iota(jnp.int32, logits.shape, 1).astype(F32)
    neg = -jnp.inf
    lg = jnp.where(lane < N_EXPERTS, logits, neg)
    m1 = jnp.max(lg, axis=-1, keepdims=True)
    i1 = jnp.min(jnp.where(lg == m1, lane, float(LANES)), axis=-1, keepdims=True)
    lg2 = jnp.where(lane == i1, neg, lg)
    m2 = jnp.max(lg2, axis=-1, keepdims=True)
    i2 = jnp.min(jnp.where(lg2 == m2, lane, float(LANES)), axis=-1, keepdims=True)
    e2 = jnp.exp(m2 - m1)
    den = 1.0 + e2
    return jnp.where(lane == 0.0, i1, jnp.where(lane == 1.0, i2, jnp.where(
        lane == 2.0, 1.0 / den, jnp.where(lane == 3.0, e2 / den, 0.0))))


def _proj_ln_route_kernel(a_ref, m_ref, w_ref, res_ref, g_ref, b_ref, wr_ref, o_ref, ro_ref):
    a = a_ref[...].astype(F32) * m_ref[...].astype(F32)
    h = _layer_norm(ALPHA * res_ref[...] + _dot(a, w_ref[...]), g_ref[...], b_ref[...])
    o_ref[...] = h
    ro_ref[...] = _route(h, wr_ref[...])


def _proj_res_ln_route(a, mult, w, res, g, b, w_router):
    m, k = a.shape
    d = w.shape[1]
    tm = _pick(m, (768, 384, 256, 128))
    row = lambda i: (i, 0)
    fixed = lambda i: (0, 0)
    return pl.pallas_call(
        _proj_ln_route_kernel,
        grid=(m // tm,),
        in_specs=[pl.BlockSpec((tm, k), row), pl.BlockSpec((tm, k), row), pl.BlockSpec((k, d), fixed),
                  pl.BlockSpec((tm, d), row), pl.BlockSpec((1, d), fixed), pl.BlockSpec((1, d), fixed),
                  pl.BlockSpec((d, LANES), fixed)],
        out_specs=[pl.BlockSpec((tm, d), row), pl.BlockSpec((tm, LANES), row)],
        out_shape=[jax.ShapeDtypeStruct((m, d), F32), jax.ShapeDtypeStruct((m, LANES), F32)],
        compiler_params=_params("parallel"),
        name="proj_res_ln_route",
    )(a, mult, w, res, g, b, w_router)


def _silu_mul(gate, up):
    return gate * (1.0 / (1.0 + jnp.exp(-gate))) * up


def _ffn_kernel(x_ref, wg_ref, wu_ref, wd_ref, g_ref, b_ref, o_ref, acc_ref, xb_ref, *, lp, nbatch, tm):
    i = pl.program_id(0)
    f = pl.program_id(1)

    @pl.when(f == 0)
    def _():
        acc_ref[...] = jnp.zeros_like(acc_ref)
        xb_ref[...] = x_ref[...].astype(BF16)

    xb = xb_ref[...]
    gate = jnp.dot(xb, wg_ref[...], preferred_element_type=F32)
    up = jnp.dot(xb, wu_ref[...], preferred_element_type=F32)
    acc_ref[...] += jnp.dot(_silu_mul(gate, up).astype(BF16), wd_ref[...], preferred_element_type=F32)

    @pl.when(f == pl.num_programs(1) - 1)
    def _():
        y = _layer_norm(ALPHA * x_ref[...] + acc_ref[...], g_ref[...], b_ref[...])
        o_ref[...] = jnp.where(_pad_row_mask(i * tm, tm, lp, nbatch), 0.0, y)


def _ffn(x, w_gu, w_down, g, b, lp, nbatch):
    m, d = x.shape
    ff = w_down.shape[0]
    tm = _pick(m, (768, 384, 256, 128))
    tf = _pick(ff, (1408, 896, 512, 256, 128))
    nf = ff // tf
    row = lambda i, f: (i, 0)
    fixed = lambda i, f: (0, 0)
    fblk = lambda i, f: jnp.where(lax.rem(i, 2) == 1, nf - 1 - f, f)
    return pl.pallas_call(
        functools.partial(_ffn_kernel, lp=lp, nbatch=nbatch, tm=tm),
        grid=(m // tm, nf),
        in_specs=[pl.BlockSpec((tm, d), row),
                  pl.BlockSpec((d, tf), lambda i, f: (0, fblk(i, f))),
                  pl.BlockSpec((d, tf), lambda i, f: (0, fblk(i, f) + nf)),
                  pl.BlockSpec((tf, d), lambda i, f: (fblk(i, f), 0)),
                  pl.BlockSpec((1, d), fixed), pl.BlockSpec((1, d), fixed)],
        out_specs=pl.BlockSpec((tm, d), row),
        out_shape=jax.ShapeDtypeStruct((m, d), F32),
        scratch_shapes=[pltpu.VMEM((tm, d), F32), pltpu.VMEM((tm, d), BF16)],
        compiler_params=_params("parallel", "arbitrary"),
        name="swiglu_res_ln",
    )(x, w_gu, w_gu, w_down, g, b)


def _moe_kernel(tile_e_ref, nact_ref, src_ref, src_next_ref, dst_ref, dst_prev_ref, x_hbm, wg_ref, wu_ref,
                wd_ref, y_hbm, xbuf, xb_ref, acc_ref, ybuf, gsem, ssem, *, tm, nf):
    j = pl.program_id(0)
    f = pl.program_id(1)
    last_tile = j == pl.num_programs(0) - 1
    active = j < nact_ref[0]
    slot = lax.rem(j, 2)
    other = 1 - slot

    def gather_row(idx_ref, s, i):
        t = idx_ref[0, 0, i]
        pltpu.make_async_copy(x_hbm.at[pl.ds(t, 1)], xbuf.at[s, pl.ds(i, 1)], gsem.at[s]).start()

    def scatter_row(idx_ref, s, i):
        t = idx_ref[0, 0, i]
        pltpu.make_async_copy(ybuf.at[s, pl.ds(i, 1)], y_hbm.at[pl.ds(t, 1)], ssem.at[s]).start()

    def gather_wait(s):
        pltpu.make_async_copy(x_hbm.at[pl.ds(0, tm)], xbuf.at[s], gsem.at[s]).wait()

    def scatter_wait(s):
        pltpu.make_async_copy(ybuf.at[s], y_hbm.at[pl.ds(0, tm)], ssem.at[s]).wait()

    @pl.when(f == 0)
    def _():
        @pl.when(j == 0)
        def _():
            ybuf[...] = jnp.zeros_like(ybuf)

            def body(i, carry):
                gather_row(src_ref, 0, i)
                return carry
            lax.fori_loop(0, tm, body, 0, unroll=DMA_ISSUE_UNROLL)

        gather_wait(slot)
        xb_ref[...] = xbuf[slot].astype(BF16)
        acc_ref[...] = jnp.zeros_like(acc_ref)
        for i in range(tm):
            gather_row(src_next_ref, other, i)
            scatter_row(dst_prev_ref, other, i)

    @pl.when(active)
    def _():
        xb = xb_ref[...]
        gate = jnp.dot(xb, wg_ref[0], preferred_element_type=F32)
        up = jnp.dot(xb, wu_ref[0], preferred_element_type=F32)
        acc_ref[...] += jnp.dot(_silu_mul(gate, up).astype(BF16), wd_ref[0], preferred_element_type=F32)

    @pl.when(f == nf - 1)
    def _():
        @pl.when(j >= 1)
        def _():
            scatter_wait(slot)

        ybuf[slot] = acc_ref[...]

        @pl.when(last_tile)
        def _():
            def body(i, carry):
                scatter_row(dst_ref, slot, i)
                return carry
            lax.fori_loop(0, tm, body, 0, unroll=DMA_ISSUE_UNROLL)
            scatter_wait(slot)
            scatter_wait(other)
            gather_wait(other)


def _moe_experts(x, expert_idx, w_gu, w_down):
    m, d = x.shape
    ne, ff = w_down.shape[0], w_down.shape[1]
    nslots = 2 * m
    tm = _pick(nslots, (512, 256, 128))
    tf = _pick(ff, (1792, 896, 512, 256, 128))
    nf = ff // tf
    n_tiles = nslots // tm + ne
    p_rows = n_tiles * tm

    e_flat = expert_idx.reshape(nslots)
    slot_ids = jnp.arange(nslots, dtype=jnp.int32)
    _, order = lax.sort_key_val(e_flat, slot_ids)
    counts = jnp.sum((e_flat[:, None] == jnp.arange(ne, dtype=jnp.int32)[None, :]).astype(jnp.int32), axis=0)
    gsize = ((counts + tm - 1) // tm) * tm
    gend = jnp.cumsum(gsize)
    tile_e = jnp.minimum(jnp.searchsorted(gend, jnp.arange(n_tiles, dtype=jnp.int32) * tm, side="right"),
                         ne - 1).astype(jnp.int32)
    pos = jnp.arange(p_rows, dtype=jnp.int32)
    pos_e = jnp.repeat(tile_e, tm)
    rank = pos - (gend - gsize)[pos_e]
    is_fill = rank >= counts[pos_e]
    slot_sorted = order[jnp.clip((jnp.cumsum(counts) - counts)[pos_e] + rank, 0, nslots - 1)]
    fill_rank = jnp.cumsum(is_fill.astype(jnp.int32)) - 1
    src_sorted = jnp.where(is_fill, 0, slot_sorted // 2)
    dst_sorted = jnp.where(is_fill, nslots + fill_rank, (slot_sorted % 2) * m + slot_sorted // 2)
    nact = (gend[-1:] // tm).astype(jnp.int32)
    src3 = src_sorted.reshape(n_tiles, 1, tm)
    dst3 = dst_sorted.reshape(n_tiles, 1, tm)

    def f_eff(j, f, nact_ref):
        snake = lambda jj, ff_: jnp.where(lax.rem(jj, 2) == 1, nf - 1 - ff_, ff_)
        return jnp.where(j < nact_ref[0], snake(j, f), snake(nact_ref[0] - 1, nf - 1))

    smem_blk = lambda imap: pl.BlockSpec((1, 1, tm), imap, memory_space=pltpu.SMEM)
    grid_spec = pltpu.PrefetchScalarGridSpec(
        num_scalar_prefetch=2,
        grid=(n_tiles, nf),
        in_specs=[smem_blk(lambda j, f, te, na: (j, 0, 0)),
                  smem_blk(lambda j, f, te, na: (jnp.minimum(j + 1, n_tiles - 1), 0, 0)),
                  smem_blk(lambda j, f, te, na: (j, 0, 0)),
                  smem_blk(lambda j, f, te, na: (jnp.where(j == 0, n_tiles - 1, j - 1), 0, 0)),
                  pl.BlockSpec(memory_space=pl.ANY),
                  pl.BlockSpec((1, d, tf), lambda j, f, te, na: (te[j], 0, f_eff(j, f, na))),
                  pl.BlockSpec((1, d, tf), lambda j, f, te, na: (te[j], 0, f_eff(j, f, na) + nf)),
                  pl.BlockSpec((1, tf, d), lambda j, f, te, na: (te[j], f_eff(j, f, na), 0))],
        out_specs=pl.BlockSpec(memory_space=pl.ANY),
        scratch_shapes=[pltpu.VMEM((2, tm, d), F32), pltpu.VMEM((tm, d), BF16), pltpu.VMEM((tm, d), F32),
                        pltpu.VMEM((2, tm, d), F32), pltpu.SemaphoreType.DMA((2,)),
                        pltpu.SemaphoreType.DMA((2,))],
    )
    return pl.pallas_call(
        functools.partial(_moe_kernel, tm=tm, nf=nf),
        grid_spec=grid_spec,
        out_shape=jax.ShapeDtypeStruct((p_rows, d), F32),
        compiler_params=_params("arbitrary", "arbitrary"),
        name="moe_grouped_swiglu",
    )(tile_e, nact, src3, src3, dst3, dst3, x, w_gu, w_gu, w_down)


def _combine_kernel(x_ref, y1_ref, y2_ref, ro_ref, g_ref, b_ref, o_ref):
    ro = ro_ref[...]
    lane = lax.broadcasted_iota(jnp.int32, ro.shape, 1)
    w1 = jnp.sum(jnp.where(lane == 2, ro, 0.0), axis=-1, keepdims=True)
    w2 = jnp.sum(jnp.where(lane == 3, ro, 0.0), axis=-1, keepdims=True)
    z = ALPHA * x_ref[...] + w1 * y1_ref[...] + w2 * y2_ref[...]
    o_ref[0] = _layer_norm(z, g_ref[...], b_ref[...])


def _moe_combine(x, y_slots, router_out, g, b, nbatch, lp):
    m, d = x.shape
    tm = BLOCK
    nb = lp // tm
    choice2 = m // tm
    row = lambda i, j: (i * nb + j + 1, 0)
    fixed = lambda i, j: (0, 0)
    return pl.pallas_call(
        _combine_kernel,
        grid=(nbatch, nb - 1),
        in_specs=[pl.BlockSpec((tm, d), row), pl.BlockSpec((tm, d), row),
                  pl.BlockSpec((tm, d), lambda i, j: (choice2 + i * nb + j + 1, 0)),
                  pl.BlockSpec((tm, LANES), row),
                  pl.BlockSpec((1, d), fixed), pl.BlockSpec((1, d), fixed)],
        out_specs=pl.BlockSpec((1, tm, d), lambda i, j: (i, j, 0)),
        out_shape=jax.ShapeDtypeStruct((nbatch, lp - tm, d), F32),
        compiler_params=_params("parallel", "parallel"),
        name="moe_combine_ln",
    )(x, y_slots, y_slots, router_out, g, b)


def _rwkv_proj_kernel(x_ref, xp_ref, mu_ref, wr_ref, wk_ref, wv_ref, w0_ref, w1_ref, w2_ref,
                      a0_ref, a1_ref, a2_ref, g1_ref, g2_ref,
                      r_ref, k_ref, v_ref, a_ref, ld_ref, g_ref, *, lp, nbatch):
    x = x_ref[...]
    tm = x.shape[0]
    rolled = pltpu.roll(x, 1, axis=0)
    prev_row = xp_ref[7:8, :]
    first = lax.broadcasted_iota(jnp.int32, (tm, 1), 0) == 0
    xx = jnp.where(first, prev_row, rolled) - x
    xx = jnp.where(_pad_row_mask(pl.program_id(0) * tm, tm, lp, nbatch), 0.0, xx)

    def mix(i):
        return x + xx * mu_ref[i:i + 1, :]

    r_ref[...] = _dot(mix(0), wr_ref[...]).astype(r_ref.dtype)
    k_ref[...] = _dot(mix(2), wk_ref[...]).astype(k_ref.dtype)
    v_ref[...] = _dot(mix(3), wv_ref[...]).astype(v_ref.dtype)
    wl = w0_ref[...] + _dot(jnp.tanh(_dot(mix(1), w1_ref[...])), w2_ref[...])
    z = -wl
    softplus = jnp.maximum(z, 0.0) + jnp.log(1.0 + jnp.exp(-jnp.abs(z)))
    ld_ref[...] = -jnp.exp(-softplus - 0.5)
    al = a0_ref[...] + _dot(_dot(mix(4), a1_ref[...]), a2_ref[...])
    a_ref[...] = (1.0 / (1.0 + jnp.exp(-al))).astype(a_ref.dtype)
    gl = _dot(mix(5), g1_ref[...])
    g_ref[...] = _dot(1.0 / (1.0 + jnp.exp(-gl)), g2_ref[...]).astype(g_ref.dtype)


def _rwkv_proj(x, mu, w_rkv, w0, w1, w2, a0, a1, a2, g1, g2, lp, nbatch):
    m, d = x.shape
    tm = _pick(m, (768, 384, 256, 128))
    row = lambda i: (i, 0)
    fixed = lambda i: (0, 0)
    full = lambda arr: pl.BlockSpec(arr.shape, fixed)
    out = lambda dt: jax.ShapeDtypeStruct((m, d), dt)
    return pl.pallas_call(
        functools.partial(_rwkv_proj_kernel, lp=lp, nbatch=nbatch),
        grid=(m // tm,),
        in_specs=[pl.BlockSpec((tm, d), row),
                  pl.BlockSpec((8, d), lambda i: (jnp.maximum(i * (tm // 8) - 1, 0), 0)),
                  full(mu), full(w_rkv[0]), full(w_rkv[1]), full(w_rkv[2]),
                  full(w0), full(w1), full(w2), full(a0), full(a1), full(a2), full(g1), full(g2)],
        out_specs=[pl.BlockSpec((tm, d), row)] * 6,
        out_shape=[out(BF16), out(BF16), out(BF16), out(BF16), out(F32), out(BF16)],
        compiler_params=_params("parallel"),
        name="rwkv_proj",
    )(x, x, mu, w_rkv[0], w_rkv[1], w_rkv[2], w0, w1, w2, a0, a1, a2, g1, g2)


def _bmm(a, b):
    return jnp.einsum("cij,cjk->cik", a.astype(BF16), b.astype(BF16), preferred_element_type=F32)


def _bmm_nt(a, b):
    return jnp.einsum("cik,cjk->cij", a.astype(BF16), b.astype(BF16), preferred_element_type=F32)


def _bmm_tn(a, b):
    return jnp.einsum("cki,ckj->cij", a.astype(BF16), b.astype(BF16), preferred_element_type=F32)


def _wkv_streams(r, k, v, a, ld, k_k, k_a, r_k, gn_g, gn_b, z):
    c = WKV_CHUNK
    n = RWKV_HEAD
    ns, tb, _ = r.shape
    nc = tb // c
    r, k, v, a = (t.astype(F32) for t in (r, k, v, a))
    head0 = lax.broadcasted_iota(jnp.int32, (1, 1, LANES), 2) < n

    def head_sum(x):
        s0 = jnp.sum(jnp.where(head0, x, 0.0), axis=-1, keepdims=True)
        s1 = jnp.sum(jnp.where(head0, 0.0, x), axis=-1, keepdims=True)
        return jnp.where(head0, s0, s1)

    def stack(x):
        xb = x.astype(BF16)
        zero = jnp.zeros_like(xb)
        return jnp.concatenate([jnp.where(head0, xb, zero), jnp.where(head0, zero, xb)], axis=1)

    def fold(x):
        return x[:, :c] + x[:, c:]

    kkr = k * k_k
    kk = kkr / jnp.maximum(jnp.sqrt(head_sum(kkr * kkr)), 1e-12)
    km = k * (1.0 + (a - 1.0) * k_a)
    bv = kk * a

    tpos = jnp.bitwise_and(lax.broadcasted_iota(jnp.int32, (ns * tb, 1), 0), c - 1)
    cs = ld.reshape(ns * tb, LANES)
    shift = 1
    while shift < c:
        cs = cs + jnp.where(tpos >= shift, pltpu.roll(cs, shift, axis=0), 0.0)
        shift *= 2

    to3 = lambda x: x.reshape(ns * nc, c, LANES)
    cs3, ld3 = to3(cs), to3(ld)
    cs_last = cs3[:, c - 1:c, :]
    e_neg = jnp.exp(-cs3)
    e_tail = jnp.exp(cs_last - cs3)
    at_s = stack(-to3(kk) * jnp.exp(cs3 - ld3))
    rt = to3(r) * jnp.exp(cs3)
    rt_s = stack(rt)
    bt_s = stack(to3(bv) * e_neg)
    kt_s = stack(to3(km) * e_neg)
    bh_s = stack(to3(bv) * e_tail)
    kh_s = stack(to3(km) * e_tail)
    v_s = stack(to3(v))
    dec = jnp.exp(cs_last)

    c2 = 2 * c
    g = _bmm_nt(jnp.concatenate([at_s, rt_s], axis=1), jnp.concatenate([bt_s, kt_s], axis=1))
    ti = jnp.bitwise_and(lax.broadcasted_iota(jnp.int32, (1, c2, c2), 1), c - 1)
    si = jnp.bitwise_and(lax.broadcasted_iota(jnp.int32, (1, c2, c2), 2), c - 1)
    strict = ti > si
    incl = ti >= si
    lab = jnp.where(strict, g[:, :c2, :c2], 0.0)
    lak = jnp.where(strict, g[:, :c2, c2:], 0.0)
    mrb = jnp.where(incl, g[:, c2:, :c2], 0.0)
    mrk = jnp.where(incl, g[:, c2:, c2:], 0.0)

    eye2 = lax.broadcasted_iota(jnp.int32, (1, c2, c2), 1) == lax.broadcasted_iota(jnp.int32, (1, c2, c2), 2)
    t_inv = jnp.where(eye2, 1.0, lab)
    lpow = _bmm(lab, lab)
    for _ in range(int(math.log2(c)) - 2):
        both = _bmm(jnp.concatenate([lpow, t_inv], axis=1), lpow)
        t_inv = t_inv + both[:, c2:]
        lpow = both[:, :c2]
    t_inv = t_inv + _bmm(t_inv, lpow)

    lv = _bmm(jnp.concatenate([lak, mrk], axis=1), v_s)
    lakv, mrkv = lv[:, :c2], lv[:, c2:]
    wu = _bmm(t_inv, jnp.concatenate([at_s, lakv.astype(BF16)], axis=2))
    qy = _bmm(mrb, wu)
    q = rt + fold(qy[:, :, :LANES])
    yp = fold(qy[:, :, LANES:] + mrkv)
    eye_l = (lax.broadcasted_iota(jnp.int32, (1, LANES, LANES), 1)
             == lax.broadcasted_iota(jnp.int32, (1, LANES, LANES), 2))
    a_t = jnp.where(eye_l, dec, 0.0) + _bmm_tn(bh_s, wu[:, :, :LANES])
    g_t = _bmm_tn(jnp.concatenate([bh_s, kh_s], axis=1),
                  jnp.concatenate([wu[:, :, LANES:].astype(BF16), v_s], axis=1))

    per_stream = lambda x: x.reshape(ns, nc, *x.shape[1:])
    q, yp, a_t, g_t = per_stream(q), per_stream(yp), per_stream(a_t), per_stream(g_t)
    ys = []
    for ci in range(nc):
        both = _bmm(jnp.concatenate([q[:, ci].astype(BF16), a_t[:, ci].astype(BF16)], axis=1), z)
        ys.append(yp[:, ci] + both[:, :c])
        z = both[:, c:] + g_t[:, ci]
    y = jnp.concatenate(ys, axis=1)

    mu = head_sum(y) * (1.0 / n)
    yc = y - mu
    var = head_sum(yc * yc) * (1.0 / n)
    yn = yc * lax.rsqrt(var + GN_EPS) * gn_g + gn_b
    bonus = head_sum(r * km * r_k) * v
    return yn + bonus, z


def _wkv_kernel(r_ref, k_ref, v_ref, a_ref, ld_ref, kk_ref, ka_ref, rk_ref, gg_ref, gb_ref,
                y_ref, z_ref):
    @pl.when(pl.program_id(1) == 0)
    def _():
        z_ref[...] = jnp.zeros_like(z_ref)

    nbatch, npairs = r_ref.shape[0], r_ref.shape[2] // LANES
    streams = [(bi, slice(pi * LANES, (pi + 1) * LANES)) for bi in range(nbatch) for pi in range(npairs)]
    tok = lambda ref: jnp.stack([ref[bi, :, ls] for bi, ls in streams])
    par = lambda ref: jnp.stack([ref[:, ls] for _, ls in streams])
    y, z = _wkv_streams(tok(r_ref), tok(k_ref), tok(v_ref), tok(a_ref), tok(ld_ref), par(kk_ref), par(ka_ref),
                        par(rk_ref), par(gg_ref), par(gb_ref), z_ref[...])
    for si, (bi, ls) in enumerate(streams):
        y_ref[bi, :, ls] = y[si]
    z_ref[...] = z


def _wkv(r, k, v, a, ld, k_k, k_a, r_k, gn_g, gn_b):
    b, lp, d = r.shape
    tb = _pick(lp, (WKV_BLOCK_TOKENS, 128, 64))
    width = WKV_BLOCK_PAIRS * LANES
    tok_spec = pl.BlockSpec((b, tb, width), lambda j, t: (0, t, j))
    par_spec = pl.BlockSpec((1, width), lambda j, t: (0, j))
    return pl.pallas_call(
        _wkv_kernel,
        grid=(d // width, lp // tb),
        in_specs=[tok_spec] * 5 + [par_spec] * 5,
        out_specs=tok_spec,
        out_shape=jax.ShapeDtypeStruct((b, lp, d), F32),
        scratch_shapes=[pltpu.VMEM((b * WKV_BLOCK_PAIRS, LANES, LANES), F32)],
        compiler_params=_params("parallel", "arbitrary"),
        name="wkv7_chunked",
    )(r, k, v, a, ld, k_k, k_a, r_k, gn_g, gn_b)


def _t5_bucket(dist):
    n = jnp.maximum(dist, 0)
    is_small = n < MAX_EXACT
    nf = jnp.maximum(n, 1).astype(F32)
    large = MAX_EXACT + (jnp.log(nf / MAX_EXACT) / math.log(MAX_DISTANCE / MAX_EXACT)
                         * (NUM_BUCKETS - MAX_EXACT)).astype(jnp.int32)
    large = jnp.minimum(large, NUM_BUCKETS - 1)
    return jnp.where(is_small, n, large)


def _bias_tables(rel_bias):
    rb = rel_bias.astype(F32)
    d_band = BLOCK + jnp.arange(BLOCK)[:, None] - jnp.arange(2 * BLOCK)[None, :]
    ok = (d_band >= 0) & (d_band < WINDOW)
    onehot = (_t5_bucket(d_band)[..., None] == jnp.arange(NUM_BUCKETS)).astype(F32)
    looked_up = jnp.einsum("qkn,nh->hqk", onehot, rb, precision=lax.Precision.HIGHEST)
    band = jnp.where(ok[None], looked_up, NEG_INF)
    col = jnp.arange(2 * BLOCK)
    band3 = jnp.stack([jnp.where((col + (n - 1) * BLOCK >= PAD)[None, None, :], band, NEG_INF) for n in range(3)])
    meta = rb[_t5_bucket(jnp.array(WINDOW))]
    mcol = jnp.arange(BLOCK)[None, :] - PAD
    d_meta = lambda n: n * BLOCK + jnp.arange(BLOCK)[:, None] - PAD - mcol
    meta3 = jnp.stack([jnp.where(((mcol >= 0) & (d_meta(n) >= WINDOW))[None], meta[:, None, None], NEG_INF)
                       for n in range(3)])
    return band3, meta3


def kernel(x, meta_tokens, rel_bias, ln_mix_g, ln_mix_b, ln_ffn_g, ln_ffn_b, attn_w_qkv, attn_b_qkv, attn_sinks, attn_w_o, rwkv_mu, rwkv_w0, rwkv_w1, rwkv_w2, rwkv_a0, rwkv_a1, rwkv_a2, rwkv_g1, rwkv_g2, rwkv_k_k, rwkv_k_a, rwkv_r_k, rwkv_w_rkv, rwkv_lnx_g, rwkv_lnx_b, rwkv_w_o, ffn_w_gu, ffn_w_down, moe_router, moe_w_gu, moe_w_down):
    b, seq, d = x.shape
    lp = seq + BLOCK
    m = b * lp
    row = lambda t: t.reshape(1, -1).astype(F32)

    meta = jnp.broadcast_to(meta_tokens.astype(x.dtype)[None], (b, N_META, d))
    h = jnp.concatenate([jnp.zeros((b, PAD, d), x.dtype), meta, x], axis=1).reshape(m, d)

    q_dim = N_HEADS * HEAD_DIM
    e_dim = attn_w_qkv.shape[2]
    col_scale = jnp.where(jnp.arange(e_dim) < q_dim, ATTN_SCALE, 1.0).astype(F32)
    qkv = _qkv_proj(h, attn_w_qkv[0].astype(BF16), row(attn_b_qkv[0]), row(col_scale))
    bias_band, bias_meta = _bias_tables(rel_bias)
    o = _attention(qkv.reshape(b, lp, e_dim), bias_band, bias_meta, attn_sinks[0].astype(F32))
    h = _proj_res_ln(o.reshape(m, q_dim), attn_w_o[0].astype(BF16), h, row(ln_mix_g[0]), row(ln_mix_b[0]))
    h = _ffn(h, ffn_w_gu[0].astype(BF16), ffn_w_down[0].astype(BF16), row(ln_ffn_g[0]), row(ln_ffn_b[0]), lp, b)

    r, k, v, a, ld, g = _rwkv_proj(
        h, rwkv_mu[0], rwkv_w_rkv[0].astype(BF16), row(rwkv_w0[0]), rwkv_w1[0].astype(BF16),
        rwkv_w2[0].astype(BF16), row(rwkv_a0[0]), rwkv_a1[0].astype(BF16), rwkv_a2[0].astype(BF16),
        rwkv_g1[0].astype(BF16), rwkv_g2[0].astype(BF16), lp, b)
    t3 = lambda t: t.reshape(b, lp, d)
    y = _wkv(t3(r), t3(k), t3(v), t3(a), t3(ld), row(rwkv_k_k[0]), row(rwkv_k_a[0]), row(rwkv_r_k[0]),
             row(rwkv_lnx_g[0]), row(rwkv_lnx_b[0]))
    w_router = jnp.pad(moe_router[0].astype(F32), ((0, 0), (0, LANES - N_EXPERTS)))
    h, routed = _proj_res_ln_route(y.reshape(m, d), g, rwkv_w_o[0].astype(BF16), h, row(ln_mix_g[1]),
                                   row(ln_mix_b[1]), w_router)
    expert_idx = routed[:, :2].astype(jnp.int32)
    y_slots = _moe_experts(h, expert_idx, moe_w_gu[0].astype(BF16), moe_w_down[0].astype(BF16))
    return _moe_combine(h, y_slots, routed, row(ln_ffn_g[1]), row(ln_ffn_b[1]), b, lp)
```

```python
import functools
import math

import jax
import jax.numpy as jnp
from jax import lax
from jax.experimental import pallas as pl
from jax.experimental.pallas import tpu as pltpu

F32 = jnp.float32
BF16 = jnp.bfloat16

N_META = 16
N_HEADS = 16
N_KV_HEADS = 4
HEAD_DIM = 64
GROUP = N_HEADS // N_KV_HEADS
WINDOW = 128
BLOCK = 128
PAD = BLOCK - N_META
ATTN_SCALE = 1.0 / math.sqrt(HEAD_DIM)
NEG_INF = -1e30
NUM_BUCKETS = 32
MAX_EXACT = NUM_BUCKETS // 2
MAX_DISTANCE = 128
RWKV_HEAD = 64
GN_EPS = 64e-5
N_EXPERTS = 8
DEPTH = 2
ALPHA = (2 * DEPTH) ** 0.25
LN_EPS = 1e-5

WKV_CHUNK = 64
WKV_BLOCK_TOKENS = 384
WKV_BLOCK_PAIRS = 2
LANES = 128
DMA_ISSUE_UNROLL = 8
VMEM_LIMIT_BYTES = 56 * 1024 * 1024


def _params(*sem, flags=None):
    return pltpu.CompilerParams(dimension_semantics=sem, vmem_limit_bytes=VMEM_LIMIT_BYTES, flags=flags)


def _pick(n, candidates):
    for c in candidates:
        if n % c == 0:
            return c
    raise ValueError(f"no tile in {candidates} divides {n}")


def _pad_row_mask(first_row, rows, lp, nbatch):
    r = first_row + lax.broadcasted_iota(jnp.int32, (rows, 1), 0)
    mask = r < PAD
    for bi in range(1, nbatch):
        mask = jnp.logical_or(mask, jnp.logical_and(r >= bi * lp, r < bi * lp + PAD))
    return mask


def _layer_norm(z, g, b):
    mu = jnp.mean(z, axis=-1, keepdims=True)
    zc = z - mu
    var = jnp.mean(zc * zc, axis=-1, keepdims=True)
    return zc * lax.rsqrt(var + LN_EPS) * g + b


def _dot(a, b):
    return jnp.dot(a.astype(BF16), b.astype(BF16), preferred_element_type=F32)


def _dot_nt(a, b):
    return lax.dot_general(a.astype(BF16), b.astype(BF16), (((1,), (1,)), ((), ())),
                           preferred_element_type=F32)


def _qkv_kernel(x_ref, w_ref, b_ref, s_ref, o_ref):
    acc = _dot(x_ref[...], w_ref[...])
    o_ref[...] = ((acc + b_ref[...]) * s_ref[...]).astype(o_ref.dtype)


def _qkv_proj(x, w, b, s):
    m, d = x.shape
    n = w.shape[1]
    tm = _pick(m, (768, 384, 256, 128))
    return pl.pallas_call(
        _qkv_kernel,
        grid=(m // tm,),
        in_specs=[pl.BlockSpec((tm, d), lambda i: (i, 0)),
                  pl.BlockSpec((d, n), lambda i: (0, 0)),
                  pl.BlockSpec((1, n), lambda i: (0, 0)),
                  pl.BlockSpec((1, n), lambda i: (0, 0))],
        out_specs=pl.BlockSpec((tm, n), lambda i: (i, 0)),
        out_shape=jax.ShapeDtypeStruct((m, n), BF16),
        compiler_params=_params("parallel"),
        name="qkv_proj",
    )(x, w, b, s)


def _attn_kernel(sink_ref, cur_ref, prev_ref, meta_ref, bias_ref, mbias_ref, o_ref):
    q_dim = N_HEADS * HEAD_DIM
    kv_dim = N_KV_HEADS * HEAD_DIM
    cur = cur_ref[0]
    prev = prev_ref[0]
    meta = meta_ref[0]

    for h in range(N_KV_HEADS):
        ks = slice(h * HEAD_DIM, (h + 1) * HEAD_DIM)
        vs = slice(kv_dim + h * HEAD_DIM, kv_dim + (h + 1) * HEAD_DIM)
        kb = jnp.concatenate([prev[:, ks], cur[:, q_dim + h * HEAD_DIM:q_dim + (h + 1) * HEAD_DIM]], axis=0)
        vb = jnp.concatenate([prev[:, vs], cur[:, q_dim + kv_dim + h * HEAD_DIM:
                                               q_dim + kv_dim + (h + 1) * HEAD_DIM]], axis=0)
        km = meta[:, ks]
        vm = meta[:, vs]
        q4 = jnp.concatenate(
            [cur[:, (h * GROUP + g) * HEAD_DIM:(h * GROUP + g + 1) * HEAD_DIM] for g in range(GROUP)], axis=0)

        bias4 = bias_ref[0, h * GROUP:(h + 1) * GROUP].reshape(GROUP * BLOCK, 2 * BLOCK)
        mbias4 = mbias_ref[0, h * GROUP:(h + 1) * GROUP].reshape(GROUP * BLOCK, BLOCK)
        s = _dot_nt(q4, kb) + bias4
        sm = _dot_nt(q4, km) + mbias4
        ps, pms, denoms = [], [], []
        for g in range(GROUP):
            rows = slice(g * BLOCK, (g + 1) * BLOCK)
            sink = sink_ref[h * GROUP + g]
            s0, s1, s2 = s[rows, :BLOCK], s[rows, BLOCK:], sm[rows]
            mx = jnp.maximum(jnp.max(jnp.maximum(jnp.maximum(s0, s1), s2), axis=-1, keepdims=True), sink)
            p0, p1, pm = jnp.exp(s0 - mx), jnp.exp(s1 - mx), jnp.exp(s2 - mx)
            denoms.append(jnp.sum(p0 + p1 + pm, axis=-1, keepdims=True) + jnp.exp(sink - mx))
            ps.append(jnp.concatenate([p0, p1], axis=1).astype(BF16))
            pms.append(pm.astype(BF16))
        o4 = _dot(jnp.concatenate(ps, axis=0), vb) + _dot(jnp.concatenate(pms, axis=0), vm)
        for g in range(GROUP):
            hd = h * GROUP + g
            o_ref[0, :, hd * HEAD_DIM:(hd + 1) * HEAD_DIM] = (
                o4[g * BLOCK:(g + 1) * BLOCK] / denoms[g]).astype(o_ref.dtype)


def _attention(qkv, bias_tbl, meta_tbl, sinks):
    b, lp, e = qkv.shape
    nb = lp // BLOCK
    q_dim = N_HEADS * HEAD_DIM
    kv2 = 2 * N_KV_HEADS * HEAD_DIM
    kv_blk = q_dim // kv2
    smem = pl.BlockSpec(memory_space=pltpu.SMEM)
    cls = lambda i, j: (jnp.minimum(j, 2), 0, 0, 0)
    return pl.pallas_call(
        _attn_kernel,
        grid=(b, nb),
        in_specs=[smem,
                  pl.BlockSpec((1, BLOCK, e), lambda i, j: (i, j, 0)),
                  pl.BlockSpec((1, BLOCK, kv2), lambda i, j: (i, jnp.maximum(j - 1, 0), kv_blk)),
                  pl.BlockSpec((1, BLOCK, kv2), lambda i, j: (i, 0, kv_blk)),
                  pl.BlockSpec((1, N_HEADS, BLOCK, 2 * BLOCK), cls),
                  pl.BlockSpec((1, N_HEADS, BLOCK, BLOCK), cls)],
        out_specs=pl.BlockSpec((1, BLOCK, q_dim), lambda i, j: (i, j, 0)),
        out_shape=jax.ShapeDtypeStruct((b, lp, q_dim), BF16),
        compiler_params=_params("parallel", "parallel"),
        name="swa_attention",
    )(sinks, qkv, qkv, qkv, bias_tbl, meta_tbl)


def _proj_ln_kernel(a_ref, w_ref, res_ref, g_ref, b_ref, o_ref):
    y = _dot(a_ref[...], w_ref[...])
    o_ref[...] = _layer_norm(ALPHA * res_ref[...] + y, g_ref[...], b_ref[...])


def _proj_res_ln(a, w, res, g, b):
    m, k = a.shape
    d = w.shape[1]
    tm = _pick(m, (768, 384, 256, 128))
    row = lambda i: (i, 0)
    fixed = lambda i: (0, 0)
    return pl.pallas_call(
        _proj_ln_kernel,
        grid=(m // tm,),
        in_specs=[pl.BlockSpec((tm, k), row), pl.BlockSpec((k, d), fixed), pl.BlockSpec((tm, d), row),
                  pl.BlockSpec((1, d), fixed), pl.BlockSpec((1, d), fixed)],
        out_specs=pl.BlockSpec((tm, d), row),
        out_shape=jax.ShapeDtypeStruct((m, d), F32),
        compiler_params=_params("parallel"),
        name="proj_res_ln",
    )(a, w, res, g, b)


def _split_bf16(x):
    hi = x.astype(BF16)
    return hi, (x - hi.astype(F32)).astype(BF16)


def _router_weights(w):
    d, ne = w.shape
    w_hi, w_lo = _split_bf16(w.astype(F32))
    zeros = lambda n: jnp.zeros((d, n), BF16)
    return jnp.stack([jnp.concatenate([w_hi, w_lo, zeros(LANES - 2 * ne)], axis=1),
                      jnp.concatenate([w_hi, zeros(LANES - ne)], axis=1)])


def _route(h, w_router):
    h_hi, h_lo = _split_bf16(h)
    first = jnp.dot(h_hi, w_router[0], preferred_element_type=F32)
    logits = (first + pltpu.roll(first, LANES - N_EXPERTS, axis=1)
              + jnp.dot(h_lo, w_router[1], preferred_element_type=F32))
    lane = lax.broadcasted_iota(jnp.int32, logits.shape, 1).astype(F32)
    neg = -jnp.inf
    lg = jnp.where(lane < N_EXPERTS, logits, neg)
    m1 = jnp.max(lg, axis=-1, keepdims=True)
    i1 = jnp.min(jnp.where(lg == m1, lane, float(LANES)), axis=-1, keepdims=True)
    lg2 = jnp.where(lane == i1, neg, lg)
    m2 = jnp.max(lg2, axis=-1, keepdims=True)
    i2 = jnp.min(jnp.where(lg2 == m2, lane, float(LANES)), axis=-1, keepdims=True)
    e2 = jnp.exp(m2 - m1)
    den = 1.0 + e2
    return jnp.where(lane == 0.0, i1, jnp.where(lane == 1.0, i2, jnp.where(
        lane == 2.0, 1.0 / den, jnp.where(lane == 3.0, e2 / den, 0.0))))


def _proj_ln_route_kernel(a_ref, m_ref, w_ref, res_ref, g_ref, b_ref, wr_ref, o_ref, ro_ref):
    a = a_ref[...].astype(F32) * m_ref[...].astype(F32)
    h = _layer_norm(ALPHA * res_ref[...] + _dot(a, w_ref[...]), g_ref[...], b_ref[...])
    o_ref[...] = h
    ro_ref[...] = _route(h, wr_ref[...])


def _proj_res_ln_route(a, mult, w, res, g, b, w_router):
    m, k = a.shape
    d = w.shape[1]
    tm = _pick(m, (768, 384, 256, 128))
    row = lambda i: (i, 0)
    fixed = lambda i: (0, 0)
    return pl.pallas_call(
        _proj_ln_route_kernel,
        grid=(m // tm,),
        in_specs=[pl.BlockSpec((tm, k), row), pl.BlockSpec((tm, k), row), pl.BlockSpec((k, d), fixed),
                  pl.BlockSpec((tm, d), row), pl.BlockSpec((1, d), fixed), pl.BlockSpec((1, d), fixed),
                  pl.BlockSpec((2, d, LANES), lambda i: (0, 0, 0))],
        out_specs=[pl.BlockSpec((tm, d), row), pl.BlockSpec((tm, LANES), row)],
        out_shape=[jax.ShapeDtypeStruct((m, d), F32), jax.ShapeDtypeStruct((m, LANES), F32)],
        compiler_params=_params("parallel"),
        name="proj_res_ln_route",
    )(a, mult, w, res, g, b, w_router)


def _silu_mul(gate, up):
    return gate * (1.0 / (1.0 + jnp.exp(-gate))) * up


def _ffn_kernel(x_ref, wg_ref, wu_ref, wd_ref, g_ref, b_ref, o_ref, acc_ref, xb_ref, *, lp, nbatch, tm):
    i = pl.program_id(0)
    f = pl.program_id(1)

    @pl.when(f == 0)
    def _():
        acc_ref[...] = jnp.zeros_like(acc_ref)
        xb_ref[...] = x_ref[...].astype(BF16)

    xb = xb_ref[...]
    gate = jnp.dot(xb, wg_ref[...], preferred_element_type=F32)
    up = jnp.dot(xb, wu_ref[...], preferred_element_type=F32)
    acc_ref[...] += jnp.dot(_silu_mul(gate, up).astype(BF16), wd_ref[...], preferred_element_type=F32)

    @pl.when(f == pl.num_programs(1) - 1)
    def _():
        y = _layer_norm(ALPHA * x_ref[...] + acc_ref[...], g_ref[...], b_ref[...])
        o_ref[...] = jnp.where(_pad_row_mask(i * tm, tm, lp, nbatch), 0.0, y)


def _ffn(x, w_gu, w_down, g, b, lp, nbatch):
    m, d = x.shape
    ff = w_down.shape[0]
    tm = _pick(m, (768, 384, 256, 128))
    tf = _pick(ff, (1408, 896, 512, 256, 128))
    nf = ff // tf
    row = lambda i, f: (i, 0)
    fixed = lambda i, f: (0, 0)
    fblk = lambda i, f: jnp.where(lax.rem(i, 2) == 1, nf - 1 - f, f)
    return pl.pallas_call(
        functools.partial(_ffn_kernel, lp=lp, nbatch=nbatch, tm=tm),
        grid=(m // tm, nf),
        in_specs=[pl.BlockSpec((tm, d), row),
                  pl.BlockSpec((d, tf), lambda i, f: (0, fblk(i, f))),
                  pl.BlockSpec((d, tf), lambda i, f: (0, fblk(i, f) + nf)),
                  pl.BlockSpec((tf, d), lambda i, f: (fblk(i, f), 0)),
                  pl.BlockSpec((1, d), fixed), pl.BlockSpec((1, d), fixed)],
        out_specs=pl.BlockSpec((tm, d), row),
        out_shape=jax.ShapeDtypeStruct((m, d), F32),
        scratch_shapes=[pltpu.VMEM((tm, d), F32), pltpu.VMEM((tm, d), BF16)],
        compiler_params=_params("parallel", "arbitrary"),
        name="swiglu_res_ln",
    )(x, w_gu, w_gu, w_down, g, b)


def _moe_kernel(tile_e_ref, nact_ref, src_ref, src_next_ref, dst_ref, dst_prev_ref, x_hbm, wg_ref, wu_ref,
                wd_ref, y_hbm, xbuf, xb_ref, acc_ref, ybuf, gsem, ssem, *, tm, nf):
    j = pl.program_id(0)
    f = pl.program_id(1)
    last_tile = j == pl.num_programs(0) - 1
    active = j < nact_ref[0]
    slot = lax.rem(j, 2)
    other = 1 - slot

    def gather_row(idx_ref, s, i):
        t = idx_ref[0, 0, i]
        pltpu.make_async_copy(x_hbm.at[pl.ds(t, 1)], xbuf.at[s, pl.ds(i, 1)], gsem.at[s]).start()

    def scatter_row(idx_ref, s, i):
        t = idx_ref[0, 0, i]
        pltpu.make_async_copy(ybuf.at[s, pl.ds(i, 1)], y_hbm.at[pl.ds(t, 1)], ssem.at[s]).start()

    def gather_wait(s):
        pltpu.make_async_copy(x_hbm.at[pl.ds(0, tm)], xbuf.at[s], gsem.at[s]).wait()

    def scatter_wait(s):
        pltpu.make_async_copy(ybuf.at[s], y_hbm.at[pl.ds(0, tm)], ssem.at[s]).wait()

    @pl.when(f == 0)
    def _():
        @pl.when(j == 0)
        def _():
            ybuf[...] = jnp.zeros_like(ybuf)

            def body(i, carry):
                gather_row(src_ref, 0, i)
                return carry
            lax.fori_loop(0, tm, body, 0, unroll=DMA_ISSUE_UNROLL)

        gather_wait(slot)
        xb_ref[...] = xbuf[slot].astype(BF16)
        acc_ref[...] = jnp.zeros_like(acc_ref)
        for i in range(tm):
            gather_row(src_next_ref, other, i)
            scatter_row(dst_prev_ref, other, i)

    @pl.when(active)
    def _():
        xb = xb_ref[...]
        gate = jnp.dot(xb, wg_ref[0], preferred_element_type=F32)
        up = jnp.dot(xb, wu_ref[0], preferred_element_type=F32)
        acc_ref[...] += jnp.dot(_silu_mul(gate, up).astype(BF16), wd_ref[0], preferred_element_type=F32)

    @pl.when(f == nf - 1)
    def _():
        @pl.when(j >= 1)
        def _():
            scatter_wait(slot)

        ybuf[slot] = acc_ref[...]

        @pl.when(last_tile)
        def _():
            def body(i, carry):
                scatter_row(dst_ref, slot, i)
                return carry
            lax.fori_loop(0, tm, body, 0, unroll=DMA_ISSUE_UNROLL)
            scatter_wait(slot)
            scatter_wait(other)
            gather_wait(other)


def _moe_experts(x, expert_idx, w_gu, w_down):
    m, d = x.shape
    ne, ff = w_down.shape[0], w_down.shape[1]
    nslots = 2 * m
    tm = _pick(nslots, (512, 256, 128))
    tf = _pick(ff, (1792, 896, 512, 256, 128))
    nf = ff // tf
    n_tiles = nslots // tm + ne
    p_rows = n_tiles * tm

    e_flat = expert_idx.reshape(nslots)
    slot_ids = jnp.arange(nslots, dtype=jnp.int32)
    _, order = lax.sort_key_val(e_flat, slot_ids)
    counts = jnp.sum((e_flat[:, None] == jnp.arange(ne, dtype=jnp.int32)[None, :]).astype(jnp.int32), axis=0)
    gsize = ((counts + tm - 1) // tm) * tm
    gend = jnp.cumsum(gsize)
    tile_e = jnp.minimum(jnp.searchsorted(gend, jnp.arange(n_tiles, dtype=jnp.int32) * tm, side="right"),
                         ne - 1).astype(jnp.int32)
    pos = jnp.arange(p_rows, dtype=jnp.int32)
    pos_e = jnp.repeat(tile_e, tm)
    rank = pos - (gend - gsize)[pos_e]
    is_fill = rank >= counts[pos_e]
    slot_sorted = order[jnp.clip((jnp.cumsum(counts) - counts)[pos_e] + rank, 0, nslots - 1)]
    fill_rank = jnp.cumsum(is_fill.astype(jnp.int32)) - 1
    src_sorted = jnp.where(is_fill, 0, slot_sorted // 2)
    dst_sorted = jnp.where(is_fill, nslots + fill_rank, (slot_sorted % 2) * m + slot_sorted // 2)
    nact = (gend[-1:] // tm).astype(jnp.int32)
    src3 = src_sorted.reshape(n_tiles, 1, tm)
    dst3 = dst_sorted.reshape(n_tiles, 1, tm)

    def f_eff(j, f, nact_ref):
        snake = lambda jj, ff_: jnp.where(lax.rem(jj, 2) == 1, nf - 1 - ff_, ff_)
        return jnp.where(j < nact_ref[0], snake(j, f), snake(nact_ref[0] - 1, nf - 1))

    smem_blk = lambda imap: pl.BlockSpec((1, 1, tm), imap, memory_space=pltpu.SMEM)
    grid_spec = pltpu.PrefetchScalarGridSpec(
        num_scalar_prefetch=2,
        grid=(n_tiles, nf),
        in_specs=[smem_blk(lambda j, f, te, na: (j, 0, 0)),
                  smem_blk(lambda j, f, te, na: (jnp.minimum(j + 1, n_tiles - 1), 0, 0)),
                  smem_blk(lambda j, f, te, na: (j, 0, 0)),
                  smem_blk(lambda j, f, te, na: (jnp.where(j == 0, n_tiles - 1, j - 1), 0, 0)),
                  pl.BlockSpec(memory_space=pl.ANY),
                  pl.BlockSpec((1, d, tf), lambda j, f, te, na: (te[j], 0, f_eff(j, f, na))),
                  pl.BlockSpec((1, d, tf), lambda j, f, te, na: (te[j], 0, f_eff(j, f, na) + nf)),
                  pl.BlockSpec((1, tf, d), lambda j, f, te, na: (te[j], f_eff(j, f, na), 0))],
        out_specs=pl.BlockSpec(memory_space=pl.ANY),
        scratch_shapes=[pltpu.VMEM((2, tm, d), F32), pltpu.VMEM((tm, d), BF16), pltpu.VMEM((tm, d), F32),
                        pltpu.VMEM((2, tm, d), F32), pltpu.SemaphoreType.DMA((2,)),
                        pltpu.SemaphoreType.DMA((2,))],
    )
    return pl.pallas_call(
        functools.partial(_moe_kernel, tm=tm, nf=nf),
        grid_spec=grid_spec,
        out_shape=jax.ShapeDtypeStruct((p_rows, d), F32),
        compiler_params=_params("arbitrary", "arbitrary"),
        name="moe_grouped_swiglu",
    )(tile_e, nact, src3, src3, dst3, dst3, x, w_gu, w_gu, w_down)


def _combine_kernel(*refs, sub):
    g_ref, b_ref, o_ref = refs[4 * sub:]
    for s in range(sub):
        x_ref, y1_ref, y2_ref, ro_ref = refs[4 * s:4 * s + 4]
        ro = ro_ref[...]
        lane = lax.broadcasted_iota(jnp.int32, ro.shape, 1)
        w1 = jnp.sum(jnp.where(lane == 2, ro, 0.0), axis=-1, keepdims=True)
        w2 = jnp.sum(jnp.where(lane == 3, ro, 0.0), axis=-1, keepdims=True)
        z = ALPHA * x_ref[...] + w1 * y1_ref[...] + w2 * y2_ref[...]
        o_ref[0, s * BLOCK:(s + 1) * BLOCK] = _layer_norm(z, g_ref[...], b_ref[...])


def _moe_combine(x, y_slots, router_out, g, b, nbatch, lp):
    m, d = x.shape
    nb = lp // BLOCK
    sub = _pick(nb - 1, (4, 2, 1))
    choice2 = m // BLOCK
    fixed = lambda i, j: (0, 0)
    ins, specs = [], []
    for s in range(sub):
        row = lambda i, j, s=s: (i * nb + j * sub + s + 1, 0)
        row2 = lambda i, j, s=s: (choice2 + i * nb + j * sub + s + 1, 0)
        ins += [x, y_slots, y_slots, router_out]
        specs += [pl.BlockSpec((BLOCK, d), row), pl.BlockSpec((BLOCK, d), row), pl.BlockSpec((BLOCK, d), row2),
                  pl.BlockSpec((BLOCK, LANES), row)]
    return pl.pallas_call(
        functools.partial(_combine_kernel, sub=sub),
        grid=(nbatch, (nb - 1) // sub),
        in_specs=specs + [pl.BlockSpec((1, d), fixed), pl.BlockSpec((1, d), fixed)],
        out_specs=pl.BlockSpec((1, sub * BLOCK, d), lambda i, j: (i, j, 0)),
        out_shape=jax.ShapeDtypeStruct((nbatch, lp - BLOCK, d), F32),
        compiler_params=_params("parallel", "parallel"),
        name="moe_combine_ln",
    )(*ins, g, b)


def _rwkv_proj_kernel(x_ref, xp_ref, mu_ref, wr_ref, wk_ref, wv_ref, w0_ref, w1_ref, w2_ref,
                      a0_ref, a1_ref, a2_ref, g1_ref, g2_ref,
                      r_ref, k_ref, v_ref, a_ref, ld_ref, g_ref, *, lp, nbatch):
    x = x_ref[...]
    tm = x.shape[0]
    rolled = pltpu.roll(x, 1, axis=0)
    prev_row = xp_ref[7:8, :]
    first = lax.broadcasted_iota(jnp.int32, (tm, 1), 0) == 0
    xx = jnp.where(first, prev_row, rolled) - x
    xx = jnp.where(_pad_row_mask(pl.program_id(0) * tm, tm, lp, nbatch), 0.0, xx)

    def mix(i):
        return x + xx * mu_ref[i:i + 1, :]

    r_ref[...] = _dot(mix(0), wr_ref[...]).astype(r_ref.dtype)
    k_ref[...] = _dot(mix(2), wk_ref[...]).astype(k_ref.dtype)
    v_ref[...] = _dot(mix(3), wv_ref[...]).astype(v_ref.dtype)
    wl = w0_ref[...] + _dot(jnp.tanh(_dot(mix(1), w1_ref[...])), w2_ref[...])
    z = -wl
    softplus = jnp.maximum(z, 0.0) + jnp.log(1.0 + jnp.exp(-jnp.abs(z)))
    ld_ref[...] = -jnp.exp(-softplus - 0.5)
    al = a0_ref[...] + _dot(_dot(mix(4), a1_ref[...]), a2_ref[...])
    a_ref[...] = (1.0 / (1.0 + jnp.exp(-al))).astype(a_ref.dtype)
    gl = _dot(mix(5), g1_ref[...])
    g_ref[...] = _dot(1.0 / (1.0 + jnp.exp(-gl)), g2_ref[...]).astype(g_ref.dtype)


def _rwkv_proj(x, mu, w_rkv, w0, w1, w2, a0, a1, a2, g1, g2, lp, nbatch):
    m, d = x.shape
    tm = _pick(m, (768, 384, 256, 128))
    row = lambda i: (i, 0)
    fixed = lambda i: (0, 0)
    full = lambda arr: pl.BlockSpec(arr.shape, fixed)
    out = lambda dt: jax.ShapeDtypeStruct((m, d), dt)
    return pl.pallas_call(
        functools.partial(_rwkv_proj_kernel, lp=lp, nbatch=nbatch),
        grid=(m // tm,),
        in_specs=[pl.BlockSpec((tm, d), row),
                  pl.BlockSpec((8, d), lambda i: (jnp.maximum(i * (tm // 8) - 1, 0), 0)),
                  full(mu), full(w_rkv[0]), full(w_rkv[1]), full(w_rkv[2]),
                  full(w0), full(w1), full(w2), full(a0), full(a1), full(a2), full(g1), full(g2)],
        out_specs=[pl.BlockSpec((tm, d), row)] * 6,
        out_shape=[out(BF16), out(BF16), out(BF16), out(BF16), out(F32), out(BF16)],
        compiler_params=_params("parallel"),
        name="rwkv_proj",
    )(x, x, mu, w_rkv[0], w_rkv[1], w_rkv[2], w0, w1, w2, a0, a1, a2, g1, g2)


def _bmm(a, b):
    return jnp.einsum("cij,cjk->cik", a.astype(BF16), b.astype(BF16), preferred_element_type=F32)


def _bmm_nt(a, b):
    return jnp.einsum("cik,cjk->cij", a.astype(BF16), b.astype(BF16), preferred_element_type=F32)


def _bmm_tn(a, b):
    return jnp.einsum("cki,ckj->cij", a.astype(BF16), b.astype(BF16), preferred_element_type=F32)


def _wkv_streams(r, k, v, a, ld, k_k, k_a, r_k, gn_g, gn_b, z):
    c = WKV_CHUNK
    n = RWKV_HEAD
    ns, tb, _ = r.shape
    nc = tb // c
    r, k, v, a = (t.astype(F32) for t in (r, k, v, a))
    head0 = lax.broadcasted_iota(jnp.int32, (1, 1, LANES), 2) < n

    def head_sum(x):
        s0 = jnp.sum(jnp.where(head0, x, 0.0), axis=-1, keepdims=True)
        s1 = jnp.sum(jnp.where(head0, 0.0, x), axis=-1, keepdims=True)
        return jnp.where(head0, s0, s1)

    def stack(x):
        xb = x.astype(BF16)
        zero = jnp.zeros_like(xb)
        return jnp.concatenate([jnp.where(head0, xb, zero), jnp.where(head0, zero, xb)], axis=1)

    def fold(x):
        return x[:, :c] + x[:, c:]

    kkr = k * k_k
    kk = kkr / jnp.maximum(jnp.sqrt(head_sum(kkr * kkr)), 1e-12)
    km = k * (1.0 + (a - 1.0) * k_a)
    bv = kk * a

    tpos = jnp.bitwise_and(lax.broadcasted_iota(jnp.int32, (ns * tb, 1), 0), c - 1)
    cs = ld.reshape(ns * tb, LANES)
    shift = 1
    while shift < c:
        cs = cs + jnp.where(tpos >= shift, pltpu.roll(cs, shift, axis=0), 0.0)
        shift *= 2

    to3 = lambda x: x.reshape(ns * nc, c, LANES)
    cs3, ld3 = to3(cs), to3(ld)
    cs_last = cs3[:, c - 1:c, :]
    e_neg = jnp.exp(-cs3)
    e_tail = jnp.exp(cs_last - cs3)
    at_s = stack(-to3(kk) * jnp.exp(cs3 - ld3))
    rt = to3(r) * jnp.exp(cs3)
    rt_s = stack(rt)
    bt_s = stack(to3(bv) * e_neg)
    kt_s = stack(to3(km) * e_neg)
    bh_s = stack(to3(bv) * e_tail)
    kh_s = stack(to3(km) * e_tail)
    v_s = stack(to3(v))
    dec = jnp.exp(cs_last)

    c2 = 2 * c
    g = _bmm_nt(jnp.concatenate([at_s, rt_s], axis=1), jnp.concatenate([bt_s, kt_s], axis=1))
    ti = jnp.bitwise_and(lax.broadcasted_iota(jnp.int32, (1, c2, c2), 1), c - 1)
    si = jnp.bitwise_and(lax.broadcasted_iota(jnp.int32, (1, c2, c2), 2), c - 1)
    strict = ti > si
    incl = ti >= si
    lab = jnp.where(strict, g[:, :c2, :c2], 0.0)
    lab_b = lab.astype(BF16)
    lak = jnp.where(strict, g[:, :c2, c2:], 0.0).astype(BF16)
    mrb = jnp.where(incl, g[:, c2:, :c2], 0.0).astype(BF16)
    mrk = jnp.where(incl, g[:, c2:, c2:], 0.0).astype(BF16)

    eye2 = lax.broadcasted_iota(jnp.int32, (1, c2, c2), 1) == lax.broadcasted_iota(jnp.int32, (1, c2, c2), 2)
    t_inv = jnp.where(eye2, 1.0, lab)
    lpow = _bmm(lab_b, lab_b).astype(BF16)
    for _ in range(int(math.log2(c)) - 2):
        both = _bmm(jnp.concatenate([lpow, t_inv.astype(BF16)], axis=1), lpow)
        t_inv = t_inv + both[:, c2:]
        lpow = both[:, :c2].astype(BF16)
    t_inv = t_inv + _bmm(t_inv, lpow)

    lv = _bmm(jnp.concatenate([lak, mrk], axis=1), v_s)
    lakv, mrkv = lv[:, :c2], lv[:, c2:]
    wu = _bmm(t_inv, jnp.concatenate([at_s, lakv.astype(BF16)], axis=2)).astype(BF16)
    qy = _bmm(mrb, wu)
    q = (rt + fold(qy[:, :, :LANES])).astype(BF16)
    yp = fold(qy[:, :, LANES:] + mrkv)
    eye_l = (lax.broadcasted_iota(jnp.int32, (1, LANES, LANES), 1)
             == lax.broadcasted_iota(jnp.int32, (1, LANES, LANES), 2))
    a_t = (jnp.where(eye_l, dec, 0.0) + _bmm_tn(bh_s, wu[:, :, :LANES])).astype(BF16)
    g_t = _bmm_tn(jnp.concatenate([bh_s, kh_s], axis=1), jnp.concatenate([wu[:, :, LANES:], v_s], axis=1))

    per_stream = lambda x: x.reshape(ns, nc, *x.shape[1:])
    q, yp, a_t, g_t = per_stream(q), per_stream(yp), per_stream(a_t), per_stream(g_t)
    ys = []
    for ci in range(nc):
        both = _bmm(jnp.concatenate([q[:, ci], a_t[:, ci]], axis=1), z)
        ys.append(yp[:, ci] + both[:, :c])
        z = both[:, c:] + g_t[:, ci]
    y = jnp.concatenate(ys, axis=1)

    mu = head_sum(y) * (1.0 / n)
    yc = y - mu
    var = head_sum(yc * yc) * (1.0 / n)
    yn = yc * lax.rsqrt(var + GN_EPS) * gn_g + gn_b
    bonus = head_sum(r * km * r_k) * v
    return yn + bonus, z


def _wkv_kernel(r_ref, k_ref, v_ref, a_ref, ld_ref, kk_ref, ka_ref, rk_ref, gg_ref, gb_ref,
                y_ref, z_ref):
    @pl.when(pl.program_id(1) == 0)
    def _():
        z_ref[...] = jnp.zeros_like(z_ref)

    nbatch, npairs = r_ref.shape[0], r_ref.shape[2] // LANES
    streams = [(bi, slice(pi * LANES, (pi + 1) * LANES)) for bi in range(nbatch) for pi in range(npairs)]
    tok = lambda ref: jnp.stack([ref[bi, :, ls] for bi, ls in streams])
    par = lambda ref: jnp.stack([ref[:, ls] for _, ls in streams])
    y, z = _wkv_streams(tok(r_ref), tok(k_ref), tok(v_ref), tok(a_ref), tok(ld_ref), par(kk_ref), par(ka_ref),
                        par(rk_ref), par(gg_ref), par(gb_ref), z_ref[...])
    for si, (bi, ls) in enumerate(streams):
        y_ref[bi, :, ls] = y[si]
    z_ref[...] = z


def _wkv(r, k, v, a, ld, k_k, k_a, r_k, gn_g, gn_b):
    b, lp, d = r.shape
    tb = _pick(lp, (WKV_BLOCK_TOKENS, 128, 64))
    width = WKV_BLOCK_PAIRS * LANES
    tok_spec = pl.BlockSpec((b, tb, width), lambda j, t: (0, t, j))
    par_spec = pl.BlockSpec((1, width), lambda j, t: (0, j))
    return pl.pallas_call(
        _wkv_kernel,
        grid=(d // width, lp // tb),
        in_specs=[tok_spec] * 5 + [par_spec] * 5,
        out_specs=tok_spec,
        out_shape=jax.ShapeDtypeStruct((b, lp, d), F32),
        scratch_shapes=[pltpu.VMEM((b * WKV_BLOCK_PAIRS, LANES, LANES), F32)],
        compiler_params=_params("parallel", "arbitrary"),
        name="wkv7_chunked",
    )(r, k, v, a, ld, k_k, k_a, r_k, gn_g, gn_b)


def _t5_bucket(dist):
    n = jnp.maximum(dist, 0)
    is_small = n < MAX_EXACT
    nf = jnp.maximum(n, 1).astype(F32)
    large = MAX_EXACT + (jnp.log(nf / MAX_EXACT) / math.log(MAX_DISTANCE / MAX_EXACT)
                         * (NUM_BUCKETS - MAX_EXACT)).astype(jnp.int32)
    large = jnp.minimum(large, NUM_BUCKETS - 1)
    return jnp.where(is_small, n, large)


def _bias_tables(rel_bias):
    rb = rel_bias.astype(F32)
    d_band = BLOCK + jnp.arange(BLOCK)[:, None] - jnp.arange(2 * BLOCK)[None, :]
    ok = (d_band >= 0) & (d_band < WINDOW)
    onehot = (_t5_bucket(d_band)[..., None] == jnp.arange(NUM_BUCKETS)).astype(F32)
    looked_up = jnp.einsum("qkn,nh->hqk", onehot, rb, precision=lax.Precision.HIGHEST)
    band = jnp.where(ok[None], looked_up, NEG_INF)
    col = jnp.arange(2 * BLOCK)
    band3 = jnp.stack([jnp.where((col + (n - 1) * BLOCK >= PAD)[None, None, :], band, NEG_INF) for n in range(3)])
    meta = rb[_t5_bucket(jnp.array(WINDOW))]
    mcol = jnp.arange(BLOCK)[None, :] - PAD
    d_meta = lambda n: n * BLOCK + jnp.arange(BLOCK)[:, None] - PAD - mcol
    meta3 = jnp.stack([jnp.where(((mcol >= 0) & (d_meta(n) >= WINDOW))[None], meta[:, None, None], NEG_INF)
                       for n in range(3)])
    return band3, meta3


def kernel(x, meta_tokens, rel_bias, ln_mix_g, ln_mix_b, ln_ffn_g, ln_ffn_b, attn_w_qkv, attn_b_qkv, attn_sinks, attn_w_o, rwkv_mu, rwkv_w0, rwkv_w1, rwkv_w2, rwkv_a0, rwkv_a1, rwkv_a2, rwkv_g1, rwkv_g2, rwkv_k_k, rwkv_k_a, rwkv_r_k, rwkv_w_rkv, rwkv_lnx_g, rwkv_lnx_b, rwkv_w_o, ffn_w_gu, ffn_w_down, moe_router, moe_w_gu, moe_w_down):
    b, seq, d = x.shape
    lp = seq + BLOCK
    m = b * lp
    row = lambda t: t.reshape(1, -1).astype(F32)

    meta = jnp.broadcast_to(meta_tokens.astype(x.dtype)[None], (b, N_META, d))
    h = jnp.concatenate([jnp.zeros((b, PAD, d), x.dtype), meta, x], axis=1).reshape(m, d)

    q_dim = N_HEADS * HEAD_DIM
    e_dim = attn_w_qkv.shape[2]
    col_scale = jnp.where(jnp.arange(e_dim) < q_dim, ATTN_SCALE, 1.0).astype(F32)
    qkv = _qkv_proj(h, attn_w_qkv[0].astype(BF16), row(attn_b_qkv[0]), row(col_scale))
    bias_band, bias_meta = _bias_tables(rel_bias)
    o = _attention(qkv.reshape(b, lp, e_dim), bias_band, bias_meta, attn_sinks[0].astype(F32))
    h = _proj_res_ln(o.reshape(m, q_dim), attn_w_o[0].astype(BF16), h, row(ln_mix_g[0]), row(ln_mix_b[0]))
    h = _ffn(h, ffn_w_gu[0].astype(BF16), ffn_w_down[0].astype(BF16), row(ln_ffn_g[0]), row(ln_ffn_b[0]), lp, b)

    r, k, v, a, ld, g = _rwkv_proj(
        h, rwkv_mu[0], rwkv_w_rkv[0].astype(BF16), row(rwkv_w0[0]), rwkv_w1[0].astype(BF16),
        rwkv_w2[0].astype(BF16), row(rwkv_a0[0]), rwkv_a1[0].astype(BF16), rwkv_a2[0].astype(BF16),
        rwkv_g1[0].astype(BF16), rwkv_g2[0].astype(BF16), lp, b)
    t3 = lambda t: t.reshape(b, lp, d)
    y = _wkv(t3(r), t3(k), t3(v), t3(a), t3(ld), row(rwkv_k_k[0]), row(rwkv_k_a[0]), row(rwkv_r_k[0]),
             row(rwkv_lnx_g[0]), row(rwkv_lnx_b[0]))
    w_router = _router_weights(moe_router[0])
    h, routed = _proj_res_ln_route(y.reshape(m, d), g, rwkv_w_o[0].astype(BF16), h, row(ln_mix_g[1]),
                                   row(ln_mix_b[1]), w_router)
    expert_idx = routed[:, :2].astype(jnp.int32)
    y_slots = _moe_experts(h, expert_idx, moe_w_gu[0].astype(BF16), moe_w_down[0].astype(BF16))
    return _moe_combine(h, y_slots, routed, row(ln_ffn_g[1]), row(ln_ffn_b[1]), b, lp)
```

```python
import functools
import math

import jax
import jax.numpy as jnp
from jax import lax
from jax.experimental import pallas as pl
from jax.experimental.pallas import tpu as pltpu

F32 = jnp.float32
BF16 = jnp.bfloat16

N_META = 16
N_HEADS = 16
N_KV_HEADS = 4
HEAD_DIM = 64
GROUP = N_HEADS // N_KV_HEADS
WINDOW = 128
BLOCK = 128
PAD = BLOCK - N_META
ATTN_SCALE = 1.0 / math.sqrt(HEAD_DIM)
NEG_INF = -1e30
NUM_BUCKETS = 32
MAX_EXACT = NUM_BUCKETS // 2
MAX_DISTANCE = 128
RWKV_HEAD = 64
GN_EPS = 64e-5
N_EXPERTS = 8
DEPTH = 2
ALPHA = (2 * DEPTH) ** 0.25
LN_EPS = 1e-5

WKV_CHUNK = 64
WKV_BLOCK_TOKENS = 384
WKV_BLOCK_PAIRS = 2
LANES = 128
DMA_ISSUE_UNROLL = 8
VMEM_LIMIT_BYTES = 56 * 1024 * 1024


def _params(*sem, flags=None):
    return pltpu.CompilerParams(dimension_semantics=sem, vmem_limit_bytes=VMEM_LIMIT_BYTES, flags=flags)


def _pick(n, candidates):
    for c in candidates:
        if n % c == 0:
            return c
    raise ValueError(f"no tile in {candidates} divides {n}")


def _pad_row_mask(first_row, rows, lp, nbatch):
    r = first_row + lax.broadcasted_iota(jnp.int32, (rows, 1), 0)
    mask = r < PAD
    for bi in range(1, nbatch):
        mask = jnp.logical_or(mask, jnp.logical_and(r >= bi * lp, r < bi * lp + PAD))
    return mask


def _layer_norm(z, g, b):
    mu = jnp.mean(z, axis=-1, keepdims=True)
    zc = z - mu
    var = jnp.mean(zc * zc, axis=-1, keepdims=True)
    return zc * lax.rsqrt(var + LN_EPS) * g + b


def _dot(a, b):
    return jnp.dot(a.astype(BF16), b.astype(BF16), preferred_element_type=F32)


def _dot_nt(a, b):
    return lax.dot_general(a.astype(BF16), b.astype(BF16), (((1,), (1,)), ((), ())),
                           preferred_element_type=F32)


def _qkv_kernel(x_ref, w_ref, b_ref, s_ref, o_ref):
    acc = _dot(x_ref[...], w_ref[...])
    o_ref[...] = ((acc + b_ref[...]) * s_ref[...]).astype(o_ref.dtype)


def _qkv_proj(x, w, b, s):
    m, d = x.shape
    n = w.shape[1]
    tm = _pick(m, (768, 384, 256, 128))
    return pl.pallas_call(
        _qkv_kernel,
        grid=(m // tm,),
        in_specs=[pl.BlockSpec((tm, d), lambda i: (i, 0)),
                  pl.BlockSpec((d, n), lambda i: (0, 0)),
                  pl.BlockSpec((1, n), lambda i: (0, 0)),
                  pl.BlockSpec((1, n), lambda i: (0, 0))],
        out_specs=pl.BlockSpec((tm, n), lambda i: (i, 0)),
        out_shape=jax.ShapeDtypeStruct((m, n), BF16),
        compiler_params=_params("parallel"),
        name="qkv_proj",
    )(x, w, b, s)


def _attn_kernel(sink_ref, cur_ref, prev_ref, meta_ref, bias_ref, mbias_ref, o_ref):
    q_dim = N_HEADS * HEAD_DIM
    kv_dim = N_KV_HEADS * HEAD_DIM
    cur = cur_ref[0]
    prev = prev_ref[0]
    meta = meta_ref[0]

    for h in range(N_KV_HEADS):
        ks = slice(h * HEAD_DIM, (h + 1) * HEAD_DIM)
        vs = slice(kv_dim + h * HEAD_DIM, kv_dim + (h + 1) * HEAD_DIM)
        kb = jnp.concatenate([prev[:, ks], cur[:, q_dim + h * HEAD_DIM:q_dim + (h + 1) * HEAD_DIM]], axis=0)
        vb = jnp.concatenate([prev[:, vs], cur[:, q_dim + kv_dim + h * HEAD_DIM:
                                               q_dim + kv_dim + (h + 1) * HEAD_DIM]], axis=0)
        km = meta[:, ks]
        vm = meta[:, vs]
        q4 = jnp.concatenate(
            [cur[:, (h * GROUP + g) * HEAD_DIM:(h * GROUP + g + 1) * HEAD_DIM] for g in range(GROUP)], axis=0)

        bias4 = bias_ref[0, h * GROUP:(h + 1) * GROUP].reshape(GROUP * BLOCK, 2 * BLOCK)
        mbias4 = mbias_ref[0, h * GROUP:(h + 1) * GROUP].reshape(GROUP * BLOCK, BLOCK)
        s = _dot_nt(q4, kb) + bias4
        sm = _dot_nt(q4, km) + mbias4
        ps, pms, denoms = [], [], []
        for g in range(GROUP):
            rows = slice(g * BLOCK, (g + 1) * BLOCK)
            sink = sink_ref[h * GROUP + g]
            s0, s1, s2 = s[rows, :BLOCK], s[rows, BLOCK:], sm[rows]
            mx = jnp.maximum(jnp.max(jnp.maximum(jnp.maximum(s0, s1), s2), axis=-1, keepdims=True), sink)
            p0, p1, pm = jnp.exp(s0 - mx), jnp.exp(s1 - mx), jnp.exp(s2 - mx)
            denoms.append(jnp.sum(p0 + p1 + pm, axis=-1, keepdims=True) + jnp.exp(sink - mx))
            ps.append(jnp.concatenate([p0, p1], axis=1).astype(BF16))
            pms.append(pm.astype(BF16))
        o4 = _dot(jnp.concatenate(ps, axis=0), vb) + _dot(jnp.concatenate(pms, axis=0), vm)
        for g in range(GROUP):
            hd = h * GROUP + g
            o_ref[0, :, hd * HEAD_DIM:(hd + 1) * HEAD_DIM] = (
                o4[g * BLOCK:(g + 1) * BLOCK] / denoms[g]).astype(o_ref.dtype)


def _attention(qkv, bias_tbl, meta_tbl, sinks):
    b, lp, e = qkv.shape
    nb = lp // BLOCK
    q_dim = N_HEADS * HEAD_DIM
    kv2 = 2 * N_KV_HEADS * HEAD_DIM
    kv_blk = q_dim // kv2
    smem = pl.BlockSpec(memory_space=pltpu.SMEM)
    cls = lambda i, j: (jnp.minimum(j, 2), 0, 0, 0)
    return pl.pallas_call(
        _attn_kernel,
        grid=(b, nb),
        in_specs=[smem,
                  pl.BlockSpec((1, BLOCK, e), lambda i, j: (i, j, 0)),
                  pl.BlockSpec((1, BLOCK, kv2), lambda i, j: (i, jnp.maximum(j - 1, 0), kv_blk)),
                  pl.BlockSpec((1, BLOCK, kv2), lambda i, j: (i, 0, kv_blk)),
                  pl.BlockSpec((1, N_HEADS, BLOCK, 2 * BLOCK), cls),
                  pl.BlockSpec((1, N_HEADS, BLOCK, BLOCK), cls)],
        out_specs=pl.BlockSpec((1, BLOCK, q_dim), lambda i, j: (i, j, 0)),
        out_shape=jax.ShapeDtypeStruct((b, lp, q_dim), BF16),
        compiler_params=_params("parallel", "parallel"),
        name="swa_attention",
    )(sinks, qkv, qkv, qkv, bias_tbl, meta_tbl)


def _proj_ln_kernel(a_ref, w_ref, res_ref, g_ref, b_ref, o_ref):
    y = _dot(a_ref[...], w_ref[...])
    o_ref[...] = _layer_norm(ALPHA * res_ref[...] + y, g_ref[...], b_ref[...])


def _proj_res_ln(a, w, res, g, b):
    m, k = a.shape
    d = w.shape[1]
    tm = _pick(m, (768, 384, 256, 128))
    row = lambda i: (i, 0)
    fixed = lambda i: (0, 0)
    return pl.pallas_call(
        _proj_ln_kernel,
        grid=(m // tm,),
        in_specs=[pl.BlockSpec((tm, k), row), pl.BlockSpec((k, d), fixed), pl.BlockSpec((tm, d), row),
                  pl.BlockSpec((1, d), fixed), pl.BlockSpec((1, d), fixed)],
        out_specs=pl.BlockSpec((tm, d), row),
        out_shape=jax.ShapeDtypeStruct((m, d), F32),
        compiler_params=_params("parallel"),
        name="proj_res_ln",
    )(a, w, res, g, b)


def _split_bf16(x):
    hi = x.astype(BF16)
    return hi, (x - hi.astype(F32)).astype(BF16)


def _router_weights(w):
    d, ne = w.shape
    w_hi, w_lo = _split_bf16(w.astype(F32))
    zeros = lambda n: jnp.zeros((d, n), BF16)
    return jnp.stack([jnp.concatenate([w_hi, w_lo, zeros(LANES - 2 * ne)], axis=1),
                      jnp.concatenate([w_hi, zeros(LANES - ne)], axis=1)])


def _route(h, w_router):
    h_hi, h_lo = _split_bf16(h)
    first = jnp.dot(h_hi, w_router[0], preferred_element_type=F32)
    logits = (first + pltpu.roll(first, LANES - N_EXPERTS, axis=1)
              + jnp.dot(h_lo, w_router[1], preferred_element_type=F32))
    lane = lax.broadcasted_iota(jnp.int32, logits.shape, 1).astype(F32)
    neg = -jnp.inf
    lg = jnp.where(lane < N_EXPERTS, logits, neg)
    m1 = jnp.max(lg, axis=-1, keepdims=True)
    i1 = jnp.min(jnp.where(lg == m1, lane, float(LANES)), axis=-1, keepdims=True)
    lg2 = jnp.where(lane == i1, neg, lg)
    m2 = jnp.max(lg2, axis=-1, keepdims=True)
    i2 = jnp.min(jnp.where(lg2 == m2, lane, float(LANES)), axis=-1, keepdims=True)
    e2 = jnp.exp(m2 - m1)
    den = 1.0 + e2
    return jnp.where(lane == 0.0, i1, jnp.where(lane == 1.0, i2, jnp.where(
        lane == 2.0, 1.0 / den, jnp.where(lane == 3.0, e2 / den, 0.0))))


def _proj_ln_route_kernel(a_ref, m_ref, w_ref, res_ref, g_ref, b_ref, wr_ref, o_ref, ro_ref):
    a = a_ref[...].astype(F32) * m_ref[...].astype(F32)
    h = _layer_norm(ALPHA * res_ref[...] + _dot(a, w_ref[...]), g_ref[...], b_ref[...])
    o_ref[...] = h
    ro_ref[...] = _route(h, wr_ref[...])


def _proj_res_ln_route(a, mult, w, res, g, b, w_router):
    m, k = a.shape
    d = w.shape[1]
    tm = _pick(m, (768, 384, 256, 128))
    row = lambda i: (i, 0)
    fixed = lambda i: (0, 0)
    return pl.pallas_call(
        _proj_ln_route_kernel,
        grid=(m // tm,),
        in_specs=[pl.BlockSpec((tm, k), row), pl.BlockSpec((tm, k), row), pl.BlockSpec((k, d), fixed),
                  pl.BlockSpec((tm, d), row), pl.BlockSpec((1, d), fixed), pl.BlockSpec((1, d), fixed),
                  pl.BlockSpec((2, d, LANES), lambda i: (0, 0, 0))],
        out_specs=[pl.BlockSpec((tm, d), row), pl.BlockSpec((tm, LANES), row)],
        out_shape=[jax.ShapeDtypeStruct((m, d), F32), jax.ShapeDtypeStruct((m, LANES), F32)],
        compiler_params=_params("parallel"),
        name="proj_res_ln_route",
    )(a, mult, w, res, g, b, w_router)


def _silu_mul(gate, up):
    return gate * (1.0 / (1.0 + jnp.exp(-gate))) * up


def _swiglu_partial(xb, wg, wu, wd):
    gate = jnp.dot(xb, wg, preferred_element_type=F32)
    up = jnp.dot(xb, wu, preferred_element_type=F32)
    return jnp.dot(_silu_mul(gate, up).astype(BF16), wd, preferred_element_type=F32)


def _ffn_kernel(x_ref, wg_ref, wu_ref, wd_ref, g_ref, b_ref, o_ref, acc_ref, xb_ref, *, lp, nbatch, tm, nf):
    i = pl.program_id(0)
    f = pl.program_id(1)
    partial = lambda: _swiglu_partial(xb_ref[...], wg_ref[...], wu_ref[...], wd_ref[...])

    @pl.when(f == 0)
    def _():
        xb_ref[...] = x_ref[...].astype(BF16)
        if nf > 1:
            acc_ref[...] = partial()

    if nf > 2:
        @pl.when(jnp.logical_and(f > 0, f < nf - 1))
        def _():
            acc_ref[...] += partial()

    @pl.when(f == nf - 1)
    def _():
        ff = partial() + acc_ref[...] if nf > 1 else partial()
        y = _layer_norm(ALPHA * x_ref[...] + ff, g_ref[...], b_ref[...])
        o_ref[...] = jnp.where(_pad_row_mask(i * tm, tm, lp, nbatch), 0.0, y)


def _ffn(x, w_gu, w_down, g, b, lp, nbatch):
    m, d = x.shape
    ff = w_down.shape[0]
    tm = _pick(m, (768, 384, 256, 128))
    tf = _pick(ff, (1408, 896, 512, 256, 128))
    nf = ff // tf
    row = lambda i, f: (i, 0)
    fixed = lambda i, f: (0, 0)
    fblk = lambda i, f: jnp.where(lax.rem(i, 2) == 1, nf - 1 - f, f)
    return pl.pallas_call(
        functools.partial(_ffn_kernel, lp=lp, nbatch=nbatch, tm=tm, nf=nf),
        grid=(m // tm, nf),
        in_specs=[pl.BlockSpec((tm, d), row),
                  pl.BlockSpec((d, tf), lambda i, f: (0, fblk(i, f))),
                  pl.BlockSpec((d, tf), lambda i, f: (0, fblk(i, f) + nf)),
                  pl.BlockSpec((tf, d), lambda i, f: (fblk(i, f), 0)),
                  pl.BlockSpec((1, d), fixed), pl.BlockSpec((1, d), fixed)],
        out_specs=pl.BlockSpec((tm, d), row),
        out_shape=jax.ShapeDtypeStruct((m, d), F32),
        scratch_shapes=[pltpu.VMEM((tm, d), F32), pltpu.VMEM((tm, d), BF16)],
        compiler_params=_params("parallel", "arbitrary"),
        name="swiglu_res_ln",
    )(x, w_gu, w_gu, w_down, g, b)


def _moe_kernel(tile_e_ref, nact_ref, src_ref, src_next_ref, dst_ref, dst_prev_ref, x_hbm, wg_ref, wu_ref,
                wd_ref, y_hbm, xbuf, xb_ref, acc_ref, ybuf, gsem, ssem, *, tm, nf):
    j = pl.program_id(0)
    f = pl.program_id(1)
    last_tile = j == pl.num_programs(0) - 1
    active = j < nact_ref[0]
    slot = lax.rem(j, 2)
    other = 1 - slot

    def gather_row(idx_ref, s, i):
        t = idx_ref[0, 0, i]
        pltpu.make_async_copy(x_hbm.at[pl.ds(t, 1)], xbuf.at[s, pl.ds(i, 1)], gsem.at[s]).start()

    def scatter_row(idx_ref, s, i):
        t = idx_ref[0, 0, i]
        pltpu.make_async_copy(ybuf.at[s, pl.ds(i, 1)], y_hbm.at[pl.ds(t, 1)], ssem.at[s]).start()

    def gather_wait(s):
        pltpu.make_async_copy(x_hbm.at[pl.ds(0, tm)], xbuf.at[s], gsem.at[s]).wait()

    def scatter_wait(s):
        pltpu.make_async_copy(ybuf.at[s], y_hbm.at[pl.ds(0, tm)], ssem.at[s]).wait()

    @pl.when(f == 0)
    def _():
        @pl.when(j == 0)
        def _():
            ybuf[...] = jnp.zeros_like(ybuf)

            def body(i, carry):
                gather_row(src_ref, 0, i)
                return carry
            lax.fori_loop(0, tm, body, 0, unroll=DMA_ISSUE_UNROLL)

        gather_wait(slot)
        xb_ref[...] = xbuf[slot].astype(BF16)
        for i in range(tm):
            gather_row(src_next_ref, other, i)
            scatter_row(dst_prev_ref, other, i)

    partial = lambda: _swiglu_partial(xb_ref[...], wg_ref[0], wu_ref[0], wd_ref[0])

    if nf > 1:
        @pl.when(jnp.logical_and(active, f == 0))
        def _():
            acc_ref[...] = partial()

    if nf > 2:
        @pl.when(jnp.logical_and(active, jnp.logical_and(f > 0, f < nf - 1)))
        def _():
            acc_ref[...] += partial()

    @pl.when(f == nf - 1)
    def _():
        @pl.when(j >= 1)
        def _():
            scatter_wait(slot)

        @pl.when(active)
        def _():
            ybuf[slot] = partial() + acc_ref[...] if nf > 1 else partial()

        @pl.when(jnp.logical_not(active))
        def _():
            ybuf[slot] = jnp.zeros(ybuf.shape[1:], ybuf.dtype)

        @pl.when(last_tile)
        def _():
            def body(i, carry):
                scatter_row(dst_ref, slot, i)
                return carry
            lax.fori_loop(0, tm, body, 0, unroll=DMA_ISSUE_UNROLL)
            scatter_wait(slot)
            scatter_wait(other)
            gather_wait(other)


def _moe_experts(x, expert_idx, w_gu, w_down):
    m, d = x.shape
    ne, ff = w_down.shape[0], w_down.shape[1]
    nslots = 2 * m
    tm = _pick(nslots, (512, 256, 128))
    tf = _pick(ff, (1792, 896, 512, 256, 128))
    nf = ff // tf
    n_tiles = nslots // tm + ne
    p_rows = n_tiles * tm

    e_flat = expert_idx.reshape(nslots)
    slot_ids = jnp.arange(nslots, dtype=jnp.int32)
    _, order = lax.sort_key_val(e_flat, slot_ids)
    counts = jnp.sum((e_flat[:, None] == jnp.arange(ne, dtype=jnp.int32)[None, :]).astype(jnp.int32), axis=0)
    gsize = ((counts + tm - 1) // tm) * tm
    gend = jnp.cumsum(gsize)
    tile_e = jnp.minimum(jnp.searchsorted(gend, jnp.arange(n_tiles, dtype=jnp.int32) * tm, side="right"),
                         ne - 1).astype(jnp.int32)
    pos = jnp.arange(p_rows, dtype=jnp.int32)
    pos_e = jnp.repeat(tile_e, tm)
    rank = pos - (gend - gsize)[pos_e]
    is_fill = rank >= counts[pos_e]
    slot_sorted = order[jnp.clip((jnp.cumsum(counts) - counts)[pos_e] + rank, 0, nslots - 1)]
    fill_rank = jnp.cumsum(is_fill.astype(jnp.int32)) - 1
    src_sorted = jnp.where(is_fill, 0, slot_sorted // 2)
    dst_sorted = jnp.where(is_fill, nslots + fill_rank, (slot_sorted % 2) * m + slot_sorted // 2)
    nact = (gend[-1:] // tm).astype(jnp.int32)
    src3 = src_sorted.reshape(n_tiles, 1, tm)
    dst3 = dst_sorted.reshape(n_tiles, 1, tm)

    def f_eff(j, f, nact_ref):
        snake = lambda jj, ff_: jnp.where(lax.rem(jj, 2) == 1, nf - 1 - ff_, ff_)
        return jnp.where(j < nact_ref[0], snake(j, f), snake(nact_ref[0] - 1, nf - 1))

    smem_blk = lambda imap: pl.BlockSpec((1, 1, tm), imap, memory_space=pltpu.SMEM)
    grid_spec = pltpu.PrefetchScalarGridSpec(
        num_scalar_prefetch=2,
        grid=(n_tiles, nf),
        in_specs=[smem_blk(lambda j, f, te, na: (j, 0, 0)),
                  smem_blk(lambda j, f, te, na: (jnp.minimum(j + 1, n_tiles - 1), 0, 0)),
                  smem_blk(lambda j, f, te, na: (j, 0, 0)),
                  smem_blk(lambda j, f, te, na: (jnp.where(j == 0, n_tiles - 1, j - 1), 0, 0)),
                  pl.BlockSpec(memory_space=pl.ANY),
                  pl.BlockSpec((1, d, tf), lambda j, f, te, na: (te[j], 0, f_eff(j, f, na))),
                  pl.BlockSpec((1, d, tf), lambda j, f, te, na: (te[j], 0, f_eff(j, f, na) + nf)),
                  pl.BlockSpec((1, tf, d), lambda j, f, te, na: (te[j], f_eff(j, f, na), 0))],
        out_specs=pl.BlockSpec(memory_space=pl.ANY),
        scratch_shapes=[pltpu.VMEM((2, tm, d), F32), pltpu.VMEM((tm, d), BF16), pltpu.VMEM((tm, d), F32),
                        pltpu.VMEM((2, tm, d), F32), pltpu.SemaphoreType.DMA((2,)),
                        pltpu.SemaphoreType.DMA((2,))],
    )
    return pl.pallas_call(
        functools.partial(_moe_kernel, tm=tm, nf=nf),
        grid_spec=grid_spec,
        out_shape=jax.ShapeDtypeStruct((p_rows, d), F32),
        compiler_params=_params("arbitrary", "arbitrary"),
        name="moe_grouped_swiglu",
    )(tile_e, nact, src3, src3, dst3, dst3, x, w_gu, w_gu, w_down)


def _combine_kernel(*refs, sub):
    g_ref, b_ref, o_ref = refs[4 * sub:]
    for s in range(sub):
        x_ref, y1_ref, y2_ref, ro_ref = refs[4 * s:4 * s + 4]
        ro = ro_ref[...]
        lane = lax.broadcasted_iota(jnp.int32, ro.shape, 1)
        w1 = jnp.sum(jnp.where(lane == 2, ro, 0.0), axis=-1, keepdims=True)
        w2 = jnp.sum(jnp.where(lane == 3, ro, 0.0), axis=-1, keepdims=True)
        z = ALPHA * x_ref[...] + w1 * y1_ref[...] + w2 * y2_ref[...]
        o_ref[0, s * BLOCK:(s + 1) * BLOCK] = _layer_norm(z, g_ref[...], b_ref[...])


def _moe_combine(x, y_slots, router_out, g, b, nbatch, lp):
    m, d = x.shape
    nb = lp // BLOCK
    sub = _pick(nb - 1, (4, 2, 1))
    choice2 = m // BLOCK
    fixed = lambda i, j: (0, 0)
    ins, specs = [], []
    for s in range(sub):
        row = lambda i, j, s=s: (i * nb + j * sub + s + 1, 0)
        row2 = lambda i, j, s=s: (choice2 + i * nb + j * sub + s + 1, 0)
        ins += [x, y_slots, y_slots, router_out]
        specs += [pl.BlockSpec((BLOCK, d), row), pl.BlockSpec((BLOCK, d), row), pl.BlockSpec((BLOCK, d), row2),
                  pl.BlockSpec((BLOCK, LANES), row)]
    return pl.pallas_call(
        functools.partial(_combine_kernel, sub=sub),
        grid=(nbatch, (nb - 1) // sub),
        in_specs=specs + [pl.BlockSpec((1, d), fixed), pl.BlockSpec((1, d), fixed)],
        out_specs=pl.BlockSpec((1, sub * BLOCK, d), lambda i, j: (i, j, 0)),
        out_shape=jax.ShapeDtypeStruct((nbatch, lp - BLOCK, d), F32),
        compiler_params=_params("parallel", "parallel"),
        name="moe_combine_ln",
    )(*ins, g, b)


def _rwkv_proj_kernel(x_ref, xp_ref, mu_ref, wr_ref, wk_ref, wv_ref, w0_ref, w1_ref, w2_ref,
                      a0_ref, a1_ref, a2_ref, g1_ref, g2_ref,
                      r_ref, k_ref, v_ref, a_ref, ld_ref, g_ref, *, lp, nbatch):
    x = x_ref[...]
    tm = x.shape[0]
    rolled = pltpu.roll(x, 1, axis=0)
    prev_row = xp_ref[7:8, :]
    first = lax.broadcasted_iota(jnp.int32, (tm, 1), 0) == 0
    xx = jnp.where(first, prev_row, rolled) - x
    xx = jnp.where(_pad_row_mask(pl.program_id(0) * tm, tm, lp, nbatch), 0.0, xx)

    def mix(i):
        return x + xx * mu_ref[i:i + 1, :]

    r_ref[...] = _dot(mix(0), wr_ref[...]).astype(r_ref.dtype)
    k_ref[...] = _dot(mix(2), wk_ref[...]).astype(k_ref.dtype)
    v_ref[...] = _dot(mix(3), wv_ref[...]).astype(v_ref.dtype)
    wl = w0_ref[...] + _dot(jnp.tanh(_dot(mix(1), w1_ref[...])), w2_ref[...])
    z = -wl
    softplus = jnp.maximum(z, 0.0) + jnp.log(1.0 + jnp.exp(-jnp.abs(z)))
    ld_ref[...] = -jnp.exp(-softplus - 0.5)
    al = a0_ref[...] + _dot(_dot(mix(4), a1_ref[...]), a2_ref[...])
    a_ref[...] = (1.0 / (1.0 + jnp.exp(-al))).astype(a_ref.dtype)
    gl = _dot(mix(5), g1_ref[...])
    g_ref[...] = _dot(1.0 / (1.0 + jnp.exp(-gl)), g2_ref[...]).astype(g_ref.dtype)


def _rwkv_proj(x, mu, w_rkv, w0, w1, w2, a0, a1, a2, g1, g2, lp, nbatch):
    m, d = x.shape
    tm = _pick(m, (768, 384, 256, 128))
    row = lambda i: (i, 0)
    fixed = lambda i: (0, 0)
    full = lambda arr: pl.BlockSpec(arr.shape, fixed)
    out = lambda dt: jax.ShapeDtypeStruct((m, d), dt)
    return pl.pallas_call(
        functools.partial(_rwkv_proj_kernel, lp=lp, nbatch=nbatch),
        grid=(m // tm,),
        in_specs=[pl.BlockSpec((tm, d), row),
                  pl.BlockSpec((8, d), lambda i: (jnp.maximum(i * (tm // 8) - 1, 0), 0)),
                  full(mu), full(w_rkv[0]), full(w_rkv[1]), full(w_rkv[2]),
                  full(w0), full(w1), full(w2), full(a0), full(a1), full(a2), full(g1), full(g2)],
        out_specs=[pl.BlockSpec((tm, d), row)] * 6,
        out_shape=[out(BF16), out(BF16), out(BF16), out(BF16), out(F32), out(BF16)],
        compiler_params=_params("parallel"),
        name="rwkv_proj",
    )(x, x, mu, w_rkv[0], w_rkv[1], w_rkv[2], w0, w1, w2, a0, a1, a2, g1, g2)


def _bmm(a, b):
    return jnp.einsum("cij,cjk->cik", a.astype(BF16), b.astype(BF16), preferred_element_type=F32)


def _bmm_nt(a, b):
    return jnp.einsum("cik,cjk->cij", a.astype(BF16), b.astype(BF16), preferred_element_type=F32)


def _bmm_tn(a, b):
    return jnp.einsum("cki,ckj->cij", a.astype(BF16), b.astype(BF16), preferred_element_type=F32)


def _wkv_streams(r, k, v, a, ld, k_k, k_a, r_k, gn_g, gn_b, z):
    c = WKV_CHUNK
    n = RWKV_HEAD
    ns, tb, _ = r.shape
    nc = tb // c
    r, k, v, a = (t.astype(F32) for t in (r, k, v, a))
    head0 = lax.broadcasted_iota(jnp.int32, (1, 1, LANES), 2) < n

    def head_sum(x):
        s0 = jnp.sum(jnp.where(head0, x, 0.0), axis=-1, keepdims=True)
        s1 = jnp.sum(jnp.where(head0, 0.0, x), axis=-1, keepdims=True)
        return jnp.where(head0, s0, s1)

    def stack(x):
        xb = x.astype(BF16)
        zero = jnp.zeros_like(xb)
        return jnp.concatenate([jnp.where(head0, xb, zero), jnp.where(head0, zero, xb)], axis=1)

    def fold(x):
        return x[:, :c] + x[:, c:]

    kkr = k * k_k
    kk = kkr / jnp.maximum(jnp.sqrt(head_sum(kkr * kkr)), 1e-12)
    km = k * (1.0 + (a - 1.0) * k_a)
    bv = kk * a

    tpos = jnp.bitwise_and(lax.broadcasted_iota(jnp.int32, (ns * tb, 1), 0), c - 1)
    cs = ld.reshape(ns * tb, LANES)
    shift = 1
    while shift < c:
        cs = cs + jnp.where(tpos >= shift, pltpu.roll(cs, shift, axis=0), 0.0)
        shift *= 2

    to3 = lambda x: x.reshape(ns * nc, c, LANES)
    cs3, ld3 = to3(cs), to3(ld)
    cs_last = cs3[:, c - 1:c, :]
    e_neg = jnp.exp(-cs3)
    e_tail = jnp.exp(cs_last - cs3)
    at_s = stack(-to3(kk) * jnp.exp(cs3 - ld3))
    rt = to3(r) * jnp.exp(cs3)
    rt_s = stack(rt)
    bt_s = stack(to3(bv) * e_neg)
    kt_s = stack(to3(km) * e_neg)
    bh_s = stack(to3(bv) * e_tail)
    kh_s = stack(to3(km) * e_tail)
    v_s = stack(to3(v))
    dec = jnp.exp(cs_last)

    c2 = 2 * c
    g = _bmm_nt(jnp.concatenate([at_s, rt_s], axis=1), jnp.concatenate([bt_s, kt_s], axis=1))
    ti = jnp.bitwise_and(lax.broadcasted_iota(jnp.int32, (1, c2, c2), 1), c - 1)
    si = jnp.bitwise_and(lax.broadcasted_iota(jnp.int32, (1, c2, c2), 2), c - 1)
    strict = ti > si
    incl = ti >= si
    lab = jnp.where(strict, g[:, :c2, :c2], 0.0)
    lab_b = lab.astype(BF16)
    lak = jnp.where(strict, g[:, :c2, c2:], 0.0).astype(BF16)
    mrb = jnp.where(incl, g[:, c2:, :c2], 0.0).astype(BF16)
    mrk = jnp.where(incl, g[:, c2:, c2:], 0.0).astype(BF16)

    eye2 = lax.broadcasted_iota(jnp.int32, (1, c2, c2), 1) == lax.broadcasted_iota(jnp.int32, (1, c2, c2), 2)
    t_inv = jnp.where(eye2, 1.0, lab)
    lpow = _bmm(lab_b, lab_b).astype(BF16)
    for _ in range(int(math.log2(c)) - 2):
        both = _bmm(jnp.concatenate([lpow, t_inv.astype(BF16)], axis=1), lpow)
        t_inv = t_inv + both[:, c2:]
        lpow = both[:, :c2].astype(BF16)
    t_inv = t_inv + _bmm(t_inv, lpow)

    lv = _bmm(jnp.concatenate([lak, mrk], axis=1), v_s)
    lakv, mrkv = lv[:, :c2], lv[:, c2:]
    wu = _bmm(t_inv, jnp.concatenate([at_s, lakv.astype(BF16)], axis=2)).astype(BF16)
    qy = _bmm(mrb, wu)
    q = (rt + fold(qy[:, :, :LANES])).astype(BF16)
    yp = fold(qy[:, :, LANES:] + mrkv)
    eye_l = (lax.broadcasted_iota(jnp.int32, (1, LANES, LANES), 1)
             == lax.broadcasted_iota(jnp.int32, (1, LANES, LANES), 2))
    a_t = (jnp.where(eye_l, dec, 0.0) + _bmm_tn(bh_s, wu[:, :, :LANES])).astype(BF16)
    g_t = _bmm_tn(jnp.concatenate([bh_s, kh_s], axis=1), jnp.concatenate([wu[:, :, LANES:], v_s], axis=1))

    per_stream = lambda x: x.reshape(ns, nc, *x.shape[1:])
    q, yp, a_t, g_t = per_stream(q), per_stream(yp), per_stream(a_t), per_stream(g_t)
    ys = []
    for ci in range(nc):
        both = _bmm(jnp.concatenate([q[:, ci], a_t[:, ci]], axis=1), z)
        ys.append(yp[:, ci] + both[:, :c])
        z = both[:, c:] + g_t[:, ci]
    y = jnp.concatenate(ys, axis=1)

    mu = head_sum(y) * (1.0 / n)
    yc = y - mu
    var = head_sum(yc * yc) * (1.0 / n)
    yn = yc * lax.rsqrt(var + GN_EPS) * gn_g + gn_b
    bonus = head_sum(r * km * r_k) * v
    return yn + bonus, z


def _wkv_kernel(r_ref, k_ref, v_ref, a_ref, ld_ref, kk_ref, ka_ref, rk_ref, gg_ref, gb_ref,
                y_ref, z_ref):
    @pl.when(pl.program_id(1) == 0)
    def _():
        z_ref[...] = jnp.zeros_like(z_ref)

    nbatch, npairs = r_ref.shape[0], r_ref.shape[2] // LANES
    streams = [(bi, slice(pi * LANES, (pi + 1) * LANES)) for bi in range(nbatch) for pi in range(npairs)]
    tok = lambda ref: jnp.stack([ref[bi, :, ls] for bi, ls in streams])
    par = lambda ref: jnp.stack([ref[:, ls] for _, ls in streams])
    y, z = _wkv_streams(tok(r_ref), tok(k_ref), tok(v_ref), tok(a_ref), tok(ld_ref), par(kk_ref), par(ka_ref),
                        par(rk_ref), par(gg_ref), par(gb_ref), z_ref[...])
    for si, (bi, ls) in enumerate(streams):
        y_ref[bi, :, ls] = y[si]
    z_ref[...] = z


def _wkv(r, k, v, a, ld, k_k, k_a, r_k, gn_g, gn_b):
    b, lp, d = r.shape
    tb = _pick(lp, (WKV_BLOCK_TOKENS, 128, 64))
    width = WKV_BLOCK_PAIRS * LANES
    tok_spec = pl.BlockSpec((b, tb, width), lambda j, t: (0, t, j))
    par_spec = pl.BlockSpec((1, width), lambda j, t: (0, j))
    return pl.pallas_call(
        _wkv_kernel,
        grid=(d // width, lp // tb),
        in_specs=[tok_spec] * 5 + [par_spec] * 5,
        out_specs=tok_spec,
        out_shape=jax.ShapeDtypeStruct((b, lp, d), F32),
        scratch_shapes=[pltpu.VMEM((b * WKV_BLOCK_PAIRS, LANES, LANES), F32)],
        compiler_params=_params("parallel", "arbitrary"),
        name="wkv7_chunked",
    )(r, k, v, a, ld, k_k, k_a, r_k, gn_g, gn_b)


def _t5_bucket(dist):
    n = jnp.maximum(dist, 0)
    is_small = n < MAX_EXACT
    nf = jnp.maximum(n, 1).astype(F32)
    large = MAX_EXACT + (jnp.log(nf / MAX_EXACT) / math.log(MAX_DISTANCE / MAX_EXACT)
                         * (NUM_BUCKETS - MAX_EXACT)).astype(jnp.int32)
    large = jnp.minimum(large, NUM_BUCKETS - 1)
    return jnp.where(is_small, n, large)


def _bias_tables(rel_bias):
    rb = rel_bias.astype(F32)
    d_band = BLOCK + jnp.arange(BLOCK)[:, None] - jnp.arange(2 * BLOCK)[None, :]
    ok = (d_band >= 0) & (d_band < WINDOW)
    onehot = (_t5_bucket(d_band)[..., None] == jnp.arange(NUM_BUCKETS)).astype(F32)
    looked_up = jnp.einsum("qkn,nh->hqk", onehot, rb, precision=lax.Precision.HIGHEST)
    band = jnp.where(ok[None], looked_up, NEG_INF)
    col = jnp.arange(2 * BLOCK)
    band3 = jnp.stack([jnp.where((col + (n - 1) * BLOCK >= PAD)[None, None, :], band, NEG_INF) for n in range(3)])
    meta = rb[_t5_bucket(jnp.array(WINDOW))]
    mcol = jnp.arange(BLOCK)[None, :] - PAD
    d_meta = lambda n: n * BLOCK + jnp.arange(BLOCK)[:, None] - PAD - mcol
    meta3 = jnp.stack([jnp.where(((mcol >= 0) & (d_meta(n) >= WINDOW))[None], meta[:, None, None], NEG_INF)
                       for n in range(3)])
    return band3, meta3


def kernel(x, meta_tokens, rel_bias, ln_mix_g, ln_mix_b, ln_ffn_g, ln_ffn_b, attn_w_qkv, attn_b_qkv, attn_sinks, attn_w_o, rwkv_mu, rwkv_w0, rwkv_w1, rwkv_w2, rwkv_a0, rwkv_a1, rwkv_a2, rwkv_g1, rwkv_g2, rwkv_k_k, rwkv_k_a, rwkv_r_k, rwkv_w_rkv, rwkv_lnx_g, rwkv_lnx_b, rwkv_w_o, ffn_w_gu, ffn_w_down, moe_router, moe_w_gu, moe_w_down):
    b, seq, d = x.shape
    lp = seq + BLOCK
    m = b * lp
    row = lambda t: t.reshape(1, -1).astype(F32)

    meta = jnp.broadcast_to(meta_tokens.astype(x.dtype)[None], (b, N_META, d))
    h = jnp.concatenate([jnp.zeros((b, PAD, d), x.dtype), meta, x], axis=1).reshape(m, d)

    q_dim = N_HEADS * HEAD_DIM
    e_dim = attn_w_qkv.shape[2]
    col_scale = jnp.where(jnp.arange(e_dim) < q_dim, ATTN_SCALE, 1.0).astype(F32)
    qkv = _qkv_proj(h, attn_w_qkv[0].astype(BF16), row(attn_b_qkv[0]), row(col_scale))
    bias_band, bias_meta = _bias_tables(rel_bias)
    o = _attention(qkv.reshape(b, lp, e_dim), bias_band, bias_meta, attn_sinks[0].astype(F32))
    h = _proj_res_ln(o.reshape(m, q_dim), attn_w_o[0].astype(BF16), h, row(ln_mix_g[0]), row(ln_mix_b[0]))
    h = _ffn(h, ffn_w_gu[0].astype(BF16), ffn_w_down[0].astype(BF16), row(ln_ffn_g[0]), row(ln_ffn_b[0]), lp, b)

    r, k, v, a, ld, g = _rwkv_proj(
        h, rwkv_mu[0], rwkv_w_rkv[0].astype(BF16), row(rwkv_w0[0]), rwkv_w1[0].astype(BF16),
        rwkv_w2[0].astype(BF16), row(rwkv_a0[0]), rwkv_a1[0].astype(BF16), rwkv_a2[0].astype(BF16),
        rwkv_g1[0].astype(BF16), rwkv_g2[0].astype(BF16), lp, b)
    t3 = lambda t: t.reshape(b, lp, d)
    y = _wkv(t3(r), t3(k), t3(v), t3(a), t3(ld), row(rwkv_k_k[0]), row(rwkv_k_a[0]), row(rwkv_r_k[0]),
             row(rwkv_lnx_g[0]), row(rwkv_lnx_b[0]))
    w_router = _router_weights(moe_router[0])
    h, routed = _proj_res_ln_route(y.reshape(m, d), g, rwkv_w_o[0].astype(BF16), h, row(ln_mix_g[1]),
                                   row(ln_mix_b[1]), w_router)
    expert_idx = routed[:, :2].astype(jnp.int32)
    y_slots = _moe_experts(h, expert_idx, moe_w_gu[0].astype(BF16), moe_w_down[0].astype(BF16))
    return _moe_combine(h, y_slots, routed, row(ln_ffn_g[1]), row(ln_ffn_b[1]), b, lp)
```

```python
import functools
import math

import jax
import jax.numpy as jnp
from jax import lax
from jax.experimental import pallas as pl
from jax.experimental.pallas import tpu as pltpu

F32 = jnp.float32
BF16 = jnp.bfloat16

N_META = 16
N_HEADS = 16
N_KV_HEADS = 4
HEAD_DIM = 64
GROUP = N_HEADS // N_KV_HEADS
WINDOW = 128
BLOCK = 128
PAD = BLOCK - N_META
ATTN_SCALE = 1.0 / math.sqrt(HEAD_DIM)
NEG_INF = -1e30
NUM_BUCKETS = 32
MAX_EXACT = NUM_BUCKETS // 2
MAX_DISTANCE = 128
RWKV_HEAD = 64
GN_EPS = 64e-5
N_EXPERTS = 8
DEPTH = 2
ALPHA = (2 * DEPTH) ** 0.25
LN_EPS = 1e-5

WKV_CHUNK = 64
WKV_BLOCK_TOKENS = 384
WKV_BLOCK_PAIRS = 2
LANES = 128
DMA_ISSUE_UNROLL = 8
VMEM_LIMIT_BYTES = 56 * 1024 * 1024


def _params(*sem, flags=None):
    return pltpu.CompilerParams(dimension_semantics=sem, vmem_limit_bytes=VMEM_LIMIT_BYTES, flags=flags)


def _pick(n, candidates):
    for c in candidates:
        if n % c == 0:
            return c
    raise ValueError(f"no tile in {candidates} divides {n}")


def _pad_row_mask(first_row, rows, lp, nbatch):
    r = first_row + lax.broadcasted_iota(jnp.int32, (rows, 1), 0)
    mask = r < PAD
    for bi in range(1, nbatch):
        mask = jnp.logical_or(mask, jnp.logical_and(r >= bi * lp, r < bi * lp + PAD))
    return mask


def _layer_norm(z, g, b):
    mu = jnp.mean(z, axis=-1, keepdims=True)
    zc = z - mu
    var = jnp.mean(zc * zc, axis=-1, keepdims=True)
    return zc * lax.rsqrt(var + LN_EPS) * g + b


def _dot(a, b):
    return jnp.dot(a.astype(BF16), b.astype(BF16), preferred_element_type=F32)


def _dot_nt(a, b):
    return lax.dot_general(a.astype(BF16), b.astype(BF16), (((1,), (1,)), ((), ())),
                           preferred_element_type=F32)


def _qkv_kernel(x_ref, w_ref, b_ref, s_ref, o_ref):
    acc = _dot(x_ref[...], w_ref[...])
    o_ref[...] = ((acc + b_ref[...]) * s_ref[...]).astype(o_ref.dtype)


def _qkv_proj(x, w, b, s):
    m, d = x.shape
    n = w.shape[1]
    tm = _pick(m, (768, 384, 256, 128))
    return pl.pallas_call(
        _qkv_kernel,
        grid=(m // tm,),
        in_specs=[pl.BlockSpec((tm, d), lambda i: (i, 0)),
                  pl.BlockSpec((d, n), lambda i: (0, 0)),
                  pl.BlockSpec((1, n), lambda i: (0, 0)),
                  pl.BlockSpec((1, n), lambda i: (0, 0))],
        out_specs=pl.BlockSpec((tm, n), lambda i: (i, 0)),
        out_shape=jax.ShapeDtypeStruct((m, n), BF16),
        compiler_params=_params("parallel"),
        name="qkv_proj",
    )(x, w, b, s)


def _attn_kernel(sink_ref, cur_ref, prev_ref, meta_ref, bias_ref, mbias_ref, o_ref):
    q_dim = N_HEADS * HEAD_DIM
    kv_dim = N_KV_HEADS * HEAD_DIM
    cur = cur_ref[0]
    prev = prev_ref[0]
    meta = meta_ref[0]

    for h in range(N_KV_HEADS):
        ks = slice(h * HEAD_DIM, (h + 1) * HEAD_DIM)
        vs = slice(kv_dim + h * HEAD_DIM, kv_dim + (h + 1) * HEAD_DIM)
        kb = jnp.concatenate([prev[:, ks], cur[:, q_dim + h * HEAD_DIM:q_dim + (h + 1) * HEAD_DIM]], axis=0)
        vb = jnp.concatenate([prev[:, vs], cur[:, q_dim + kv_dim + h * HEAD_DIM:
                                               q_dim + kv_dim + (h + 1) * HEAD_DIM]], axis=0)
        km = meta[:, ks]
        vm = meta[:, vs]
        q4 = jnp.concatenate(
            [cur[:, (h * GROUP + g) * HEAD_DIM:(h * GROUP + g + 1) * HEAD_DIM] for g in range(GROUP)], axis=0)

        bias4 = bias_ref[0, h * GROUP:(h + 1) * GROUP].reshape(GROUP * BLOCK, 2 * BLOCK)
        mbias4 = mbias_ref[0, h * GROUP:(h + 1) * GROUP].reshape(GROUP * BLOCK, BLOCK)
        s = _dot_nt(q4, kb) + bias4
        sm = _dot_nt(q4, km) + mbias4
        ps, pms, denoms = [], [], []
        for g in range(GROUP):
            rows = slice(g * BLOCK, (g + 1) * BLOCK)
            sink = sink_ref[h * GROUP + g]
            s0, s1, s2 = s[rows, :BLOCK], s[rows, BLOCK:], sm[rows]
            mx = jnp.maximum(jnp.max(jnp.maximum(jnp.maximum(s0, s1), s2), axis=-1, keepdims=True), sink)
            p0, p1, pm = jnp.exp(s0 - mx), jnp.exp(s1 - mx), jnp.exp(s2 - mx)
            denoms.append(jnp.sum(p0 + p1 + pm, axis=-1, keepdims=True) + jnp.exp(sink - mx))
            ps.append(jnp.concatenate([p0, p1], axis=1).astype(BF16))
            pms.append(pm.astype(BF16))
        o4 = _dot(jnp.concatenate(ps, axis=0), vb) + _dot(jnp.concatenate(pms, axis=0), vm)
        for g in range(GROUP):
            hd = h * GROUP + g
            o_ref[0, :, hd * HEAD_DIM:(hd + 1) * HEAD_DIM] = (
                o4[g * BLOCK:(g + 1) * BLOCK] / denoms[g]).astype(o_ref.dtype)


def _attention(qkv, bias_tbl, meta_tbl, sinks):
    b, lp, e = qkv.shape
    nb = lp // BLOCK
    q_dim = N_HEADS * HEAD_DIM
    kv2 = 2 * N_KV_HEADS * HEAD_DIM
    kv_blk = q_dim // kv2
    smem = pl.BlockSpec(memory_space=pltpu.SMEM)
    cls = lambda i, j: (jnp.minimum(j, 2), 0, 0, 0)
    return pl.pallas_call(
        _attn_kernel,
        grid=(b, nb),
        in_specs=[smem,
                  pl.BlockSpec((1, BLOCK, e), lambda i, j: (i, j, 0)),
                  pl.BlockSpec((1, BLOCK, kv2), lambda i, j: (i, jnp.maximum(j - 1, 0), kv_blk)),
                  pl.BlockSpec((1, BLOCK, kv2), lambda i, j: (i, 0, kv_blk)),
                  pl.BlockSpec((1, N_HEADS, BLOCK, 2 * BLOCK), cls),
                  pl.BlockSpec((1, N_HEADS, BLOCK, BLOCK), cls)],
        out_specs=pl.BlockSpec((1, BLOCK, q_dim), lambda i, j: (i, j, 0)),
        out_shape=jax.ShapeDtypeStruct((b, lp, q_dim), BF16),
        compiler_params=_params("parallel", "parallel"),
        name="swa_attention",
    )(sinks, qkv, qkv, qkv, bias_tbl, meta_tbl)


def _proj_ln_kernel(a_ref, w_ref, res_ref, g_ref, b_ref, o_ref):
    y = _dot(a_ref[...], w_ref[...])
    o_ref[...] = _layer_norm(ALPHA * res_ref[...] + y, g_ref[...], b_ref[...])


def _proj_res_ln(a, w, res, g, b):
    m, k = a.shape
    d = w.shape[1]
    tm = _pick(m, (768, 384, 256, 128))
    row = lambda i: (i, 0)
    fixed = lambda i: (0, 0)
    return pl.pallas_call(
        _proj_ln_kernel,
        grid=(m // tm,),
        in_specs=[pl.BlockSpec((tm, k), row), pl.BlockSpec((k, d), fixed), pl.BlockSpec((tm, d), row),
                  pl.BlockSpec((1, d), fixed), pl.BlockSpec((1, d), fixed)],
        out_specs=pl.BlockSpec((tm, d), row),
        out_shape=jax.ShapeDtypeStruct((m, d), F32),
        compiler_params=_params("parallel"),
        name="proj_res_ln",
    )(a, w, res, g, b)


def _split_bf16(x):
    hi = x.astype(BF16)
    return hi, (x - hi.astype(F32)).astype(BF16)


def _router_weights(w):
    d, ne = w.shape
    w_hi, w_lo = _split_bf16(w.astype(F32))
    zeros = lambda n: jnp.zeros((d, n), BF16)
    return jnp.stack([jnp.concatenate([w_hi, w_lo, zeros(LANES - 2 * ne)], axis=1),
                      jnp.concatenate([w_hi, zeros(LANES - ne)], axis=1)])


def _route(h, w_router):
    h_hi, h_lo = _split_bf16(h)
    first = jnp.dot(h_hi, w_router[0], preferred_element_type=F32)
    logits = (first + pltpu.roll(first, LANES - N_EXPERTS, axis=1)
              + jnp.dot(h_lo, w_router[1], preferred_element_type=F32))
    lane = lax.broadcasted_iota(jnp.int32, logits.shape, 1).astype(F32)
    neg = -jnp.inf
    lg = jnp.where(lane < N_EXPERTS, logits, neg)
    m1 = jnp.max(lg, axis=-1, keepdims=True)
    i1 = jnp.min(jnp.where(lg == m1, lane, float(LANES)), axis=-1, keepdims=True)
    lg2 = jnp.where(lane == i1, neg, lg)
    m2 = jnp.max(lg2, axis=-1, keepdims=True)
    i2 = jnp.min(jnp.where(lg2 == m2, lane, float(LANES)), axis=-1, keepdims=True)
    e2 = jnp.exp(m2 - m1)
    den = 1.0 + e2
    return jnp.where(lane == 0.0, i1, jnp.where(lane == 1.0, i2, jnp.where(
        lane == 2.0, 1.0 / den, jnp.where(lane == 3.0, e2 / den, 0.0))))


def _proj_ln_route_kernel(a_ref, m_ref, w_ref, res_ref, g_ref, b_ref, wr_ref, o_ref, ro_ref):
    a = a_ref[...].astype(F32) * m_ref[...].astype(F32)
    h = _layer_norm(ALPHA * res_ref[...] + _dot(a, w_ref[...]), g_ref[...], b_ref[...])
    o_ref[...] = h
    ro_ref[...] = _route(h, wr_ref[...])


def _proj_res_ln_route(a, mult, w, res, g, b, w_router):
    m, k = a.shape
    d = w.shape[1]
    tm = _pick(m, (768, 384, 256, 128))
    row = lambda i: (i, 0)
    fixed = lambda i: (0, 0)
    return pl.pallas_call(
        _proj_ln_route_kernel,
        grid=(m // tm,),
        in_specs=[pl.BlockSpec((tm, k), row), pl.BlockSpec((tm, k), row), pl.BlockSpec((k, d), fixed),
                  pl.BlockSpec((tm, d), row), pl.BlockSpec((1, d), fixed), pl.BlockSpec((1, d), fixed),
                  pl.BlockSpec((2, d, LANES), lambda i: (0, 0, 0))],
        out_specs=[pl.BlockSpec((tm, d), row), pl.BlockSpec((tm, LANES), row)],
        out_shape=[jax.ShapeDtypeStruct((m, d), F32), jax.ShapeDtypeStruct((m, LANES), F32)],
        compiler_params=_params("parallel"),
        name="proj_res_ln_route",
    )(a, mult, w, res, g, b, w_router)


def _silu_mul(gate, up):
    return gate * (1.0 / (1.0 + jnp.exp(-gate))) * up


def _swiglu_partial(xb, wg, wu, wd):
    gate = jnp.dot(xb, wg, preferred_element_type=F32)
    up = jnp.dot(xb, wu, preferred_element_type=F32)
    return jnp.dot(_silu_mul(gate, up).astype(BF16), wd, preferred_element_type=F32)


def _ffn_kernel(x_ref, wg_ref, wu_ref, wd_ref, g_ref, b_ref, o_ref, acc_ref, xb_ref, *, lp, nbatch, tm, nf):
    i = pl.program_id(0)
    f = pl.program_id(1)
    partial = lambda: _swiglu_partial(xb_ref[...], wg_ref[...], wu_ref[...], wd_ref[...])

    @pl.when(f == 0)
    def _():
        xb_ref[...] = x_ref[...].astype(BF16)
        if nf > 1:
            acc_ref[...] = partial()

    if nf > 2:
        @pl.when(jnp.logical_and(f > 0, f < nf - 1))
        def _():
            acc_ref[...] += partial()

    @pl.when(f == nf - 1)
    def _():
        ff = partial() + acc_ref[...] if nf > 1 else partial()
        y = _layer_norm(ALPHA * x_ref[...] + ff, g_ref[...], b_ref[...])
        o_ref[...] = jnp.where(_pad_row_mask(i * tm, tm, lp, nbatch), 0.0, y)


def _ffn(x, w_gu, w_down, g, b, lp, nbatch):
    m, d = x.shape
    ff = w_down.shape[0]
    tm = _pick(m, (768, 384, 256, 128))
    tf = _pick(ff, (1408, 896, 512, 256, 128))
    nf = ff // tf
    row = lambda i, f: (i, 0)
    fixed = lambda i, f: (0, 0)
    fblk = lambda i, f: jnp.where(lax.rem(i, 2) == 1, nf - 1 - f, f)
    return pl.pallas_call(
        functools.partial(_ffn_kernel, lp=lp, nbatch=nbatch, tm=tm, nf=nf),
        grid=(m // tm, nf),
        in_specs=[pl.BlockSpec((tm, d), row),
                  pl.BlockSpec((d, tf), lambda i, f: (0, fblk(i, f))),
                  pl.BlockSpec((d, tf), lambda i, f: (0, fblk(i, f) + nf)),
                  pl.BlockSpec((tf, d), lambda i, f: (fblk(i, f), 0)),
                  pl.BlockSpec((1, d), fixed), pl.BlockSpec((1, d), fixed)],
        out_specs=pl.BlockSpec((tm, d), row),
        out_shape=jax.ShapeDtypeStruct((m, d), F32),
        scratch_shapes=[pltpu.VMEM((tm, d), F32), pltpu.VMEM((tm, d), BF16)],
        compiler_params=_params("parallel", "arbitrary"),
        name="swiglu_res_ln",
    )(x, w_gu, w_gu, w_down, g, b)


def _moe_kernel(tile_e_ref, nact_ref, src_ref, src_next_ref, dst_ref, dst_prev_ref, x_hbm, wg_ref, wu_ref,
                wd_ref, y_hbm, xbuf, xb_ref, acc_ref, ybuf, gsem, ssem, *, tm, nf):
    j = pl.program_id(0)
    f = pl.program_id(1)
    last_tile = j == pl.num_programs(0) - 1
    active = j < nact_ref[0]
    slot = lax.rem(j, 2)
    other = 1 - slot

    def gather_row(idx_ref, s, i):
        t = idx_ref[0, 0, i]
        pltpu.make_async_copy(x_hbm.at[pl.ds(t, 1)], xbuf.at[s, pl.ds(i, 1)], gsem.at[s]).start()

    def scatter_row(idx_ref, s, i):
        t = idx_ref[0, 0, i]
        pltpu.make_async_copy(ybuf.at[s, pl.ds(i, 1)], y_hbm.at[pl.ds(t, 1)], ssem.at[s]).start()

    def gather_wait(s):
        pltpu.make_async_copy(x_hbm.at[pl.ds(0, tm)], xbuf.at[s], gsem.at[s]).wait()

    def scatter_wait(s):
        pltpu.make_async_copy(ybuf.at[s], y_hbm.at[pl.ds(0, tm)], ssem.at[s]).wait()

    @pl.when(f == 0)
    def _():
        @pl.when(j == 0)
        def _():
            ybuf[...] = jnp.zeros_like(ybuf)

            def body(i, carry):
                gather_row(src_ref, 0, i)
                return carry
            lax.fori_loop(0, tm, body, 0, unroll=DMA_ISSUE_UNROLL)

        gather_wait(slot)
        xb_ref[...] = xbuf[slot].astype(BF16)
        for i in range(tm):
            gather_row(src_next_ref, other, i)
            scatter_row(dst_prev_ref, other, i)

    partial = lambda: _swiglu_partial(xb_ref[...], wg_ref[0], wu_ref[0], wd_ref[0])

    if nf > 1:
        @pl.when(jnp.logical_and(active, f == 0))
        def _():
            acc_ref[...] = partial()

    if nf > 2:
        @pl.when(jnp.logical_and(active, jnp.logical_and(f > 0, f < nf - 1)))
        def _():
            acc_ref[...] += partial()

    @pl.when(f == nf - 1)
    def _():
        @pl.when(j >= 1)
        def _():
            scatter_wait(slot)

        @pl.when(active)
        def _():
            ybuf[slot] = partial() + acc_ref[...] if nf > 1 else partial()

        @pl.when(jnp.logical_not(active))
        def _():
            ybuf[slot] = jnp.zeros(ybuf.shape[1:], ybuf.dtype)

        @pl.when(last_tile)
        def _():
            def body(i, carry):
                scatter_row(dst_ref, slot, i)
                return carry
            lax.fori_loop(0, tm, body, 0, unroll=DMA_ISSUE_UNROLL)
            scatter_wait(slot)
            scatter_wait(other)
            gather_wait(other)


def _moe_experts(x, expert_idx, w_gu, w_down):
    m, d = x.shape
    ne, ff = w_down.shape[0], w_down.shape[1]
    nslots = 2 * m
    tm = _pick(nslots, (512, 256, 128))
    tf = _pick(ff, (1792, 896, 512, 256, 128))
    nf = ff // tf
    n_tiles = nslots // tm + ne
    p_rows = n_tiles * tm

    e_flat = expert_idx.reshape(nslots)
    slot_bits = max(1, (nslots - 1).bit_length())
    assert ne << slot_bits < 2 ** 31
    order = jnp.bitwise_and(jnp.sort(jnp.left_shift(e_flat, slot_bits) + jnp.arange(nslots, dtype=jnp.int32)),
                            (1 << slot_bits) - 1)
    counts = jnp.sum((e_flat[:, None] == jnp.arange(ne, dtype=jnp.int32)[None, :]).astype(jnp.int32), axis=0)
    gsize = ((counts + tm - 1) // tm) * tm
    gend = jnp.cumsum(gsize)
    tile_start = jnp.arange(n_tiles, dtype=jnp.int32) * tm
    tile_e = jnp.minimum(jnp.sum((gend[None, :] <= tile_start[:, None]).astype(jnp.int32), axis=1), ne - 1)
    pos = jnp.arange(p_rows, dtype=jnp.int32)
    pos_e = jnp.repeat(tile_e, tm)
    rank = pos - (gend - gsize)[pos_e]
    is_fill = rank >= counts[pos_e]
    slot_sorted = order[jnp.clip((jnp.cumsum(counts) - counts)[pos_e] + rank, 0, nslots - 1)]
    fill_rank = jnp.cumsum(is_fill.astype(jnp.int32)) - 1
    src_sorted = jnp.where(is_fill, 0, slot_sorted // 2)
    dst_sorted = jnp.where(is_fill, nslots + fill_rank, (slot_sorted % 2) * m + slot_sorted // 2)
    nact = (gend[-1:] // tm).astype(jnp.int32)
    src3 = src_sorted.reshape(n_tiles, 1, tm)
    dst3 = dst_sorted.reshape(n_tiles, 1, tm)

    def f_eff(j, f, nact_ref):
        snake = lambda jj, ff_: jnp.where(lax.rem(jj, 2) == 1, nf - 1 - ff_, ff_)
        return jnp.where(j < nact_ref[0], snake(j, f), snake(nact_ref[0] - 1, nf - 1))

    smem_blk = lambda imap: pl.BlockSpec((1, 1, tm), imap, memory_space=pltpu.SMEM)
    grid_spec = pltpu.PrefetchScalarGridSpec(
        num_scalar_prefetch=2,
        grid=(n_tiles, nf),
        in_specs=[smem_blk(lambda j, f, te, na: (j, 0, 0)),
                  smem_blk(lambda j, f, te, na: (jnp.minimum(j + 1, n_tiles - 1), 0, 0)),
                  smem_blk(lambda j, f, te, na: (j, 0, 0)),
                  smem_blk(lambda j, f, te, na: (jnp.where(j == 0, n_tiles - 1, j - 1), 0, 0)),
                  pl.BlockSpec(memory_space=pl.ANY),
                  pl.BlockSpec((1, d, tf), lambda j, f, te, na: (te[j], 0, f_eff(j, f, na))),
                  pl.BlockSpec((1, d, tf), lambda j, f, te, na: (te[j], 0, f_eff(j, f, na) + nf)),
                  pl.BlockSpec((1, tf, d), lambda j, f, te, na: (te[j], f_eff(j, f, na), 0))],
        out_specs=pl.BlockSpec(memory_space=pl.ANY),
        scratch_shapes=[pltpu.VMEM((2, tm, d), F32), pltpu.VMEM((tm, d), BF16), pltpu.VMEM((tm, d), F32),
                        pltpu.VMEM((2, tm, d), F32), pltpu.SemaphoreType.DMA((2,)),
                        pltpu.SemaphoreType.DMA((2,))],
    )
    return pl.pallas_call(
        functools.partial(_moe_kernel, tm=tm, nf=nf),
        grid_spec=grid_spec,
        out_shape=jax.ShapeDtypeStruct((p_rows, d), F32),
        compiler_params=_params("arbitrary", "arbitrary"),
        name="moe_grouped_swiglu",
    )(tile_e, nact, src3, src3, dst3, dst3, x, w_gu, w_gu, w_down)


def _combine_kernel(*refs, sub):
    g_ref, b_ref, o_ref = refs[4 * sub:]
    for s in range(sub):
        x_ref, y1_ref, y2_ref, ro_ref = refs[4 * s:4 * s + 4]
        ro = ro_ref[...]
        lane = lax.broadcasted_iota(jnp.int32, ro.shape, 1)
        w1 = jnp.sum(jnp.where(lane == 2, ro, 0.0), axis=-1, keepdims=True)
        w2 = jnp.sum(jnp.where(lane == 3, ro, 0.0), axis=-1, keepdims=True)
        z = ALPHA * x_ref[...] + w1 * y1_ref[...] + w2 * y2_ref[...]
        o_ref[0, s * BLOCK:(s + 1) * BLOCK] = _layer_norm(z, g_ref[...], b_ref[...])


def _moe_combine(x, y_slots, router_out, g, b, nbatch, lp):
    m, d = x.shape
    nb = lp // BLOCK
    sub = _pick(nb - 1, (4, 2, 1))
    choice2 = m // BLOCK
    fixed = lambda i, j: (0, 0)
    ins, specs = [], []
    for s in range(sub):
        row = lambda i, j, s=s: (i * nb + j * sub + s + 1, 0)
        row2 = lambda i, j, s=s: (choice2 + i * nb + j * sub + s + 1, 0)
        ins += [x, y_slots, y_slots, router_out]
        specs += [pl.BlockSpec((BLOCK, d), row), pl.BlockSpec((BLOCK, d), row), pl.BlockSpec((BLOCK, d), row2),
                  pl.BlockSpec((BLOCK, LANES), row)]
    return pl.pallas_call(
        functools.partial(_combine_kernel, sub=sub),
        grid=(nbatch, (nb - 1) // sub),
        in_specs=specs + [pl.BlockSpec((1, d), fixed), pl.BlockSpec((1, d), fixed)],
        out_specs=pl.BlockSpec((1, sub * BLOCK, d), lambda i, j: (i, j, 0)),
        out_shape=jax.ShapeDtypeStruct((nbatch, lp - BLOCK, d), F32),
        compiler_params=_params("parallel", "parallel"),
        name="moe_combine_ln",
    )(*ins, g, b)


def _rwkv_proj_kernel(x_ref, xp_ref, mu_ref, wr_ref, wk_ref, wv_ref, w0_ref, w1_ref, w2_ref,
                      a0_ref, a1_ref, a2_ref, g1_ref, g2_ref,
                      r_ref, k_ref, v_ref, a_ref, ld_ref, g_ref, *, lp, nbatch):
    x = x_ref[...]
    tm = x.shape[0]
    rolled = pltpu.roll(x, 1, axis=0)
    prev_row = xp_ref[7:8, :]
    first = lax.broadcasted_iota(jnp.int32, (tm, 1), 0) == 0
    xx = jnp.where(first, prev_row, rolled) - x
    xx = jnp.where(_pad_row_mask(pl.program_id(0) * tm, tm, lp, nbatch), 0.0, xx)

    def mix(i):
        return x + xx * mu_ref[i:i + 1, :]

    r_ref[...] = _dot(mix(0), wr_ref[...]).astype(r_ref.dtype)
    k_ref[...] = _dot(mix(2), wk_ref[...]).astype(k_ref.dtype)
    v_ref[...] = _dot(mix(3), wv_ref[...]).astype(v_ref.dtype)
    wl = w0_ref[...] + _dot(jnp.tanh(_dot(mix(1), w1_ref[...])), w2_ref[...])
    z = -wl
    softplus = jnp.maximum(z, 0.0) + jnp.log(1.0 + jnp.exp(-jnp.abs(z)))
    ld_ref[...] = -jnp.exp(-softplus - 0.5)
    al = a0_ref[...] + _dot(_dot(mix(4), a1_ref[...]), a2_ref[...])
    a_ref[...] = (1.0 / (1.0 + jnp.exp(-al))).astype(a_ref.dtype)
    gl = _dot(mix(5), g1_ref[...])
    g_ref[...] = _dot(1.0 / (1.0 + jnp.exp(-gl)), g2_ref[...]).astype(g_ref.dtype)


def _rwkv_proj(x, mu, w_rkv, w0, w1, w2, a0, a1, a2, g1, g2, lp, nbatch):
    m, d = x.shape
    tm = _pick(m, (768, 384, 256, 128))
    row = lambda i: (i, 0)
    fixed = lambda i: (0, 0)
    full = lambda arr: pl.BlockSpec(arr.shape, fixed)
    out = lambda dt: jax.ShapeDtypeStruct((m, d), dt)
    return pl.pallas_call(
        functools.partial(_rwkv_proj_kernel, lp=lp, nbatch=nbatch),
        grid=(m // tm,),
        in_specs=[pl.BlockSpec((tm, d), row),
                  pl.BlockSpec((8, d), lambda i: (jnp.maximum(i * (tm // 8) - 1, 0), 0)),
                  full(mu), full(w_rkv[0]), full(w_rkv[1]), full(w_rkv[2]),
                  full(w0), full(w1), full(w2), full(a0), full(a1), full(a2), full(g1), full(g2)],
        out_specs=[pl.BlockSpec((tm, d), row)] * 6,
        out_shape=[out(BF16), out(BF16), out(BF16), out(BF16), out(F32), out(BF16)],
        compiler_params=_params("parallel"),
        name="rwkv_proj",
    )(x, x, mu, w_rkv[0], w_rkv[1], w_rkv[2], w0, w1, w2, a0, a1, a2, g1, g2)


def _bmm(a, b):
    return jnp.einsum("cij,cjk->cik", a.astype(BF16), b.astype(BF16), preferred_element_type=F32)


def _bmm_nt(a, b):
    return jnp.einsum("cik,cjk->cij", a.astype(BF16), b.astype(BF16), preferred_element_type=F32)


def _bmm_tn(a, b):
    return jnp.einsum("cki,ckj->cij", a.astype(BF16), b.astype(BF16), preferred_element_type=F32)


def _wkv_streams(r, k, v, a, ld, k_k, k_a, r_k, gn_g, gn_b, z):
    c = WKV_CHUNK
    n = RWKV_HEAD
    ns, tb, _ = r.shape
    nc = tb // c
    r, k, v, a = (t.astype(F32) for t in (r, k, v, a))
    head0 = lax.broadcasted_iota(jnp.int32, (1, 1, LANES), 2) < n

    def head_sum(x):
        s0 = jnp.sum(jnp.where(head0, x, 0.0), axis=-1, keepdims=True)
        s1 = jnp.sum(jnp.where(head0, 0.0, x), axis=-1, keepdims=True)
        return jnp.where(head0, s0, s1)

    def stack(x):
        xb = x.astype(BF16)
        zero = jnp.zeros_like(xb)
        return jnp.concatenate([jnp.where(head0, xb, zero), jnp.where(head0, zero, xb)], axis=1)

    def fold(x):
        return x[:, :c] + x[:, c:]

    kkr = k * k_k
    kk = kkr / jnp.maximum(jnp.sqrt(head_sum(kkr * kkr)), 1e-12)
    km = k * (1.0 + (a - 1.0) * k_a)
    bv = kk * a

    tpos = jnp.bitwise_and(lax.broadcasted_iota(jnp.int32, (ns * tb, 1), 0), c - 1)
    cs = ld.reshape(ns * tb, LANES)
    shift = 1
    while shift < c:
        cs = cs + jnp.where(tpos >= shift, pltpu.roll(cs, shift, axis=0), 0.0)
        shift *= 2

    to3 = lambda x: x.reshape(ns * nc, c, LANES)
    cs3, ld3 = to3(cs), to3(ld)
    cs_last = cs3[:, c - 1:c, :]
    e_neg = jnp.exp(-cs3)
    e_tail = jnp.exp(cs_last - cs3)
    at_s = stack(-to3(kk) * jnp.exp(cs3 - ld3))
    rt = to3(r) * jnp.exp(cs3)
    rt_s = stack(rt)
    bt_s = stack(to3(bv) * e_neg)
    kt_s = stack(to3(km) * e_neg)
    bh_s = stack(to3(bv) * e_tail)
    kh_s = stack(to3(km) * e_tail)
    v_s = stack(to3(v))
    dec = jnp.exp(cs_last)

    c2 = 2 * c
    g = _bmm_nt(jnp.concatenate([at_s, rt_s], axis=1), jnp.concatenate([bt_s, kt_s], axis=1))
    ti = jnp.bitwise_and(lax.broadcasted_iota(jnp.int32, (1, c2, c2), 1), c - 1)
    si = jnp.bitwise_and(lax.broadcasted_iota(jnp.int32, (1, c2, c2), 2), c - 1)
    strict = ti > si
    incl = ti >= si
    lab = jnp.where(strict, g[:, :c2, :c2], 0.0)
    lab_b = lab.astype(BF16)
    lak = jnp.where(strict, g[:, :c2, c2:], 0.0).astype(BF16)
    mrb = jnp.where(incl, g[:, c2:, :c2], 0.0).astype(BF16)
    mrk = jnp.where(incl, g[:, c2:, c2:], 0.0).astype(BF16)

    eye2 = lax.broadcasted_iota(jnp.int32, (1, c2, c2), 1) == lax.broadcasted_iota(jnp.int32, (1, c2, c2), 2)
    t_inv = jnp.where(eye2, 1.0, lab)
    lpow = _bmm(lab_b, lab_b).astype(BF16)
    for _ in range(int(math.log2(c)) - 2):
        both = _bmm(jnp.concatenate([lpow, t_inv.astype(BF16)], axis=1), lpow)
        t_inv = t_inv + both[:, c2:]
        lpow = both[:, :c2].astype(BF16)
    t_inv = t_inv + _bmm(t_inv, lpow)

    lv = _bmm(jnp.concatenate([lak, mrk], axis=1), v_s)
    lakv, mrkv = lv[:, :c2], lv[:, c2:]
    wu = _bmm(t_inv, jnp.concatenate([at_s, lakv.astype(BF16)], axis=2)).astype(BF16)
    qy = _bmm(mrb, wu)
    q = (rt + fold(qy[:, :, :LANES])).astype(BF16)
    yp = fold(qy[:, :, LANES:] + mrkv)
    eye_l = (lax.broadcasted_iota(jnp.int32, (1, LANES, LANES), 1)
             == lax.broadcasted_iota(jnp.int32, (1, LANES, LANES), 2))
    a_t = (jnp.where(eye_l, dec, 0.0) + _bmm_tn(bh_s, wu[:, :, :LANES])).astype(BF16)
    g_t = _bmm_tn(jnp.concatenate([bh_s, kh_s], axis=1), jnp.concatenate([wu[:, :, LANES:], v_s], axis=1))

    per_stream = lambda x: x.reshape(ns, nc, *x.shape[1:])
    q, yp, a_t, g_t = per_stream(q), per_stream(yp), per_stream(a_t), per_stream(g_t)
    ys = []
    for ci in range(nc):
        both = _bmm(jnp.concatenate([q[:, ci], a_t[:, ci]], axis=1), z)
        ys.append(yp[:, ci] + both[:, :c])
        z = both[:, c:] + g_t[:, ci]
    y = jnp.concatenate(ys, axis=1)

    mu = head_sum(y) * (1.0 / n)
    yc = y - mu
    var = head_sum(yc * yc) * (1.0 / n)
    yn = yc * lax.rsqrt(var + GN_EPS) * gn_g + gn_b
    bonus = head_sum(r * km * r_k) * v
    return yn + bonus, z


def _wkv_kernel(r_ref, k_ref, v_ref, a_ref, ld_ref, kk_ref, ka_ref, rk_ref, gg_ref, gb_ref,
                y_ref, z_ref):
    @pl.when(pl.program_id(1) == 0)
    def _():
        z_ref[...] = jnp.zeros_like(z_ref)

    nbatch, npairs = r_ref.shape[0], r_ref.shape[2] // LANES
    streams = [(bi, slice(pi * LANES, (pi + 1) * LANES)) for bi in range(nbatch) for pi in range(npairs)]
    tok = lambda ref: jnp.stack([ref[bi, :, ls] for bi, ls in streams])
    par = lambda ref: jnp.stack([ref[:, ls] for _, ls in streams])
    y, z = _wkv_streams(tok(r_ref), tok(k_ref), tok(v_ref), tok(a_ref), tok(ld_ref), par(kk_ref), par(ka_ref),
                        par(rk_ref), par(gg_ref), par(gb_ref), z_ref[...])
    for si, (bi, ls) in enumerate(streams):
        y_ref[bi, :, ls] = y[si]
    z_ref[...] = z


def _wkv(r, k, v, a, ld, k_k, k_a, r_k, gn_g, gn_b):
    b, lp, d = r.shape
    tb = _pick(lp, (WKV_BLOCK_TOKENS, 128, 64))
    width = WKV_BLOCK_PAIRS * LANES
    tok_spec = pl.BlockSpec((b, tb, width), lambda j, t: (0, t, j))
    par_spec = pl.BlockSpec((1, width), lambda j, t: (0, j))
    return pl.pallas_call(
        _wkv_kernel,
        grid=(d // width, lp // tb),
        in_specs=[tok_spec] * 5 + [par_spec] * 5,
        out_specs=tok_spec,
        out_shape=jax.ShapeDtypeStruct((b, lp, d), F32),
        scratch_shapes=[pltpu.VMEM((b * WKV_BLOCK_PAIRS, LANES, LANES), F32)],
        compiler_params=_params("parallel", "arbitrary"),
        name="wkv7_chunked",
    )(r, k, v, a, ld, k_k, k_a, r_k, gn_g, gn_b)


def _t5_bucket(dist):
    n = jnp.maximum(dist, 0)
    is_small = n < MAX_EXACT
    nf = jnp.maximum(n, 1).astype(F32)
    large = MAX_EXACT + (jnp.log(nf / MAX_EXACT) / math.log(MAX_DISTANCE / MAX_EXACT)
                         * (NUM_BUCKETS - MAX_EXACT)).astype(jnp.int32)
    large = jnp.minimum(large, NUM_BUCKETS - 1)
    return jnp.where(is_small, n, large)


def _bias_tables(rel_bias):
    rb = rel_bias.astype(F32)
    d_band = BLOCK + jnp.arange(BLOCK)[:, None] - jnp.arange(2 * BLOCK)[None, :]
    ok = (d_band >= 0) & (d_band < WINDOW)
    onehot = (_t5_bucket(d_band)[..., None] == jnp.arange(NUM_BUCKETS)).astype(F32)
    looked_up = jnp.einsum("qkn,nh->hqk", onehot, rb, precision=lax.Precision.HIGHEST)
    band = jnp.where(ok[None], looked_up, NEG_INF)
    col = jnp.arange(2 * BLOCK)
    band3 = jnp.stack([jnp.where((col + (n - 1) * BLOCK >= PAD)[None, None, :], band, NEG_INF) for n in range(3)])
    meta = rb[_t5_bucket(jnp.array(WINDOW))]
    mcol = jnp.arange(BLOCK)[None, :] - PAD
    d_meta = lambda n: n * BLOCK + jnp.arange(BLOCK)[:, None] - PAD - mcol
    meta3 = jnp.stack([jnp.where(((mcol >= 0) & (d_meta(n) >= WINDOW))[None], meta[:, None, None], NEG_INF)
                       for n in range(3)])
    return band3, meta3


def kernel(x, meta_tokens, rel_bias, ln_mix_g, ln_mix_b, ln_ffn_g, ln_ffn_b, attn_w_qkv, attn_b_qkv, attn_sinks, attn_w_o, rwkv_mu, rwkv_w0, rwkv_w1, rwkv_w2, rwkv_a0, rwkv_a1, rwkv_a2, rwkv_g1, rwkv_g2, rwkv_k_k, rwkv_k_a, rwkv_r_k, rwkv_w_rkv, rwkv_lnx_g, rwkv_lnx_b, rwkv_w_o, ffn_w_gu, ffn_w_down, moe_router, moe_w_gu, moe_w_down):
    b, seq, d = x.shape
    lp = seq + BLOCK
    m = b * lp
    row = lambda t: t.reshape(1, -1).astype(F32)

    meta = jnp.broadcast_to(meta_tokens.astype(x.dtype)[None], (b, N_META, d))
    h = jnp.concatenate([jnp.zeros((b, PAD, d), x.dtype), meta, x], axis=1).reshape(m, d)

    q_dim = N_HEADS * HEAD_DIM
    e_dim = attn_w_qkv.shape[2]
    col_scale = jnp.where(jnp.arange(e_dim) < q_dim, ATTN_SCALE, 1.0).astype(F32)
    qkv = _qkv_proj(h, attn_w_qkv[0].astype(BF16), row(attn_b_qkv[0]), row(col_scale))
    bias_band, bias_meta = _bias_tables(rel_bias)
    o = _attention(qkv.reshape(b, lp, e_dim), bias_band, bias_meta, attn_sinks[0].astype(F32))
    h = _proj_res_ln(o.reshape(m, q_dim), attn_w_o[0].astype(BF16), h, row(ln_mix_g[0]), row(ln_mix_b[0]))
    h = _ffn(h, ffn_w_gu[0].astype(BF16), ffn_w_down[0].astype(BF16), row(ln_ffn_g[0]), row(ln_ffn_b[0]), lp, b)

    r, k, v, a, ld, g = _rwkv_proj(
        h, rwkv_mu[0], rwkv_w_rkv[0].astype(BF16), row(rwkv_w0[0]), rwkv_w1[0].astype(BF16),
        rwkv_w2[0].astype(BF16), row(rwkv_a0[0]), rwkv_a1[0].astype(BF16), rwkv_a2[0].astype(BF16),
        rwkv_g1[0].astype(BF16), rwkv_g2[0].astype(BF16), lp, b)
    t3 = lambda t: t.reshape(b, lp, d)
    y = _wkv(t3(r), t3(k), t3(v), t3(a), t3(ld), row(rwkv_k_k[0]), row(rwkv_k_a[0]), row(rwkv_r_k[0]),
             row(rwkv_lnx_g[0]), row(rwkv_lnx_b[0]))
    w_router = _router_weights(moe_router[0])
    h, routed = _proj_res_ln_route(y.reshape(m, d), g, rwkv_w_o[0].astype(BF16), h, row(ln_mix_g[1]),
                                   row(ln_mix_b[1]), w_router)
    expert_idx = routed[:, :2].astype(jnp.int32)
    y_slots = _moe_experts(h, expert_idx, moe_w_gu[0].astype(BF16), moe_w_down[0].astype(BF16))
    return _moe_combine(h, y_slots, routed, row(ln_ffn_g[1]), row(ln_ffn_b[1]), b, lp)
```

```python
import functools
import math

import jax
import jax.numpy as jnp
from jax import lax
from jax.experimental import pallas as pl
from jax.experimental.pallas import tpu as pltpu

F32 = jnp.float32
BF16 = jnp.bfloat16

N_META = 16
N_HEADS = 16
N_KV_HEADS = 4
HEAD_DIM = 64
GROUP = N_HEADS // N_KV_HEADS
WINDOW = 128
BLOCK = 128
PAD = BLOCK - N_META
ATTN_SCALE = 1.0 / math.sqrt(HEAD_DIM)
NEG_INF = -1e30
NUM_BUCKETS = 32
MAX_EXACT = NUM_BUCKETS // 2
MAX_DISTANCE = 128
RWKV_HEAD = 64
GN_EPS = 64e-5
N_EXPERTS = 8
DEPTH = 2
ALPHA = (2 * DEPTH) ** 0.25
LN_EPS = 1e-5

WKV_CHUNK = 64
WKV_BLOCK_TOKENS = 384
WKV_BLOCK_PAIRS = 2
LANES = 128
DMA_ISSUE_UNROLL = 8
VMEM_LIMIT_BYTES = 56 * 1024 * 1024


def _params(*sem, flags=None):
    return pltpu.CompilerParams(dimension_semantics=sem, vmem_limit_bytes=VMEM_LIMIT_BYTES, flags=flags)


def _pick(n, candidates):
    for c in candidates:
        if n % c == 0:
            return c
    raise ValueError(f"no tile in {candidates} divides {n}")


def _pad_row_mask(first_row, rows, lp, nbatch):
    r = first_row + lax.broadcasted_iota(jnp.int32, (rows, 1), 0)
    mask = r < PAD
    for bi in range(1, nbatch):
        mask = jnp.logical_or(mask, jnp.logical_and(r >= bi * lp, r < bi * lp + PAD))
    return mask


def _layer_norm(z, g, b):
    mu = jnp.mean(z, axis=-1, keepdims=True)
    zc = z - mu
    var = jnp.mean(zc * zc, axis=-1, keepdims=True)
    return zc * lax.rsqrt(var + LN_EPS) * g + b


def _dot(a, b):
    return jnp.dot(a.astype(BF16), b.astype(BF16), preferred_element_type=F32)


def _dot_nt(a, b):
    return lax.dot_general(a.astype(BF16), b.astype(BF16), (((1,), (1,)), ((), ())),
                           preferred_element_type=F32)


def _qkv_kernel(x_ref, w_ref, b_ref, s_ref, o_ref):
    acc = _dot(x_ref[...], w_ref[...])
    o_ref[...] = ((acc + b_ref[...]) * s_ref[...]).astype(o_ref.dtype)


def _qkv_proj(x, w, b, s):
    m, d = x.shape
    n = w.shape[1]
    tm = _pick(m, (768, 384, 256, 128))
    return pl.pallas_call(
        _qkv_kernel,
        grid=(m // tm,),
        in_specs=[pl.BlockSpec((tm, d), lambda i: (i, 0)),
                  pl.BlockSpec((d, n), lambda i: (0, 0)),
                  pl.BlockSpec((1, n), lambda i: (0, 0)),
                  pl.BlockSpec((1, n), lambda i: (0, 0))],
        out_specs=pl.BlockSpec((tm, n), lambda i: (i, 0)),
        out_shape=jax.ShapeDtypeStruct((m, n), BF16),
        compiler_params=_params("parallel"),
        name="qkv_proj",
    )(x, w, b, s)


def _attn_kernel(sink_ref, cur_ref, prev_ref, meta_ref, bias_ref, mbias_ref, o_ref):
    q_dim = N_HEADS * HEAD_DIM
    kv_dim = N_KV_HEADS * HEAD_DIM
    cur = cur_ref[0]
    prev = prev_ref[0]
    meta = meta_ref[0]

    for h in range(N_KV_HEADS):
        ks = slice(h * HEAD_DIM, (h + 1) * HEAD_DIM)
        vs = slice(kv_dim + h * HEAD_DIM, kv_dim + (h + 1) * HEAD_DIM)
        kb = jnp.concatenate([prev[:, ks], cur[:, q_dim + h * HEAD_DIM:q_dim + (h + 1) * HEAD_DIM]], axis=0)
        vb = jnp.concatenate([prev[:, vs], cur[:, q_dim + kv_dim + h * HEAD_DIM:
                                               q_dim + kv_dim + (h + 1) * HEAD_DIM]], axis=0)
        km = meta[:, ks]
        vm = meta[:, vs]
        q4 = jnp.concatenate(
            [cur[:, (h * GROUP + g) * HEAD_DIM:(h * GROUP + g + 1) * HEAD_DIM] for g in range(GROUP)], axis=0)

        bias4 = bias_ref[0, h * GROUP:(h + 1) * GROUP].reshape(GROUP * BLOCK, 2 * BLOCK)
        mbias4 = mbias_ref[0, h * GROUP:(h + 1) * GROUP].reshape(GROUP * BLOCK, BLOCK)
        s = _dot_nt(q4, kb) + bias4
        sm = _dot_nt(q4, km) + mbias4
        ps, pms, denoms = [], [], []
        for g in range(GROUP):
            rows = slice(g * BLOCK, (g + 1) * BLOCK)
            sink = sink_ref[h * GROUP + g]
            s0, s1, s2 = s[rows, :BLOCK], s[rows, BLOCK:], sm[rows]
            mx = jnp.maximum(jnp.max(jnp.maximum(jnp.maximum(s0, s1), s2), axis=-1, keepdims=True), sink)
            p0, p1, pm = jnp.exp(s0 - mx), jnp.exp(s1 - mx), jnp.exp(s2 - mx)
            denoms.append(jnp.sum(p0 + p1 + pm, axis=-1, keepdims=True) + jnp.exp(sink - mx))
            ps.append(jnp.concatenate([p0, p1], axis=1).astype(BF16))
            pms.append(pm.astype(BF16))
        o4 = _dot(jnp.concatenate(ps, axis=0), vb) + _dot(jnp.concatenate(pms, axis=0), vm)
        for g in range(GROUP):
            hd = h * GROUP + g
            o_ref[0, :, hd * HEAD_DIM:(hd + 1) * HEAD_DIM] = (
                o4[g * BLOCK:(g + 1) * BLOCK] / denoms[g]).astype(o_ref.dtype)


def _attention(qkv, bias_tbl, meta_tbl, sinks):
    b, lp, e = qkv.shape
    nb = lp // BLOCK
    q_dim = N_HEADS * HEAD_DIM
    kv2 = 2 * N_KV_HEADS * HEAD_DIM
    kv_blk = q_dim // kv2
    smem = pl.BlockSpec(memory_space=pltpu.SMEM)
    cls = lambda i, j: (jnp.minimum(j, 2), 0, 0, 0)
    return pl.pallas_call(
        _attn_kernel,
        grid=(b, nb),
        in_specs=[smem,
                  pl.BlockSpec((1, BLOCK, e), lambda i, j: (i, j, 0)),
                  pl.BlockSpec((1, BLOCK, kv2), lambda i, j: (i, jnp.maximum(j - 1, 0), kv_blk)),
                  pl.BlockSpec((1, BLOCK, kv2), lambda i, j: (i, 0, kv_blk)),
                  pl.BlockSpec((1, N_HEADS, BLOCK, 2 * BLOCK), cls),
                  pl.BlockSpec((1, N_HEADS, BLOCK, BLOCK), cls)],
        out_specs=pl.BlockSpec((1, BLOCK, q_dim), lambda i, j: (i, j, 0)),
        out_shape=jax.ShapeDtypeStruct((b, lp, q_dim), BF16),
        compiler_params=_params("parallel", "parallel"),
        name="swa_attention",
    )(sinks, qkv, qkv, qkv, bias_tbl, meta_tbl)


def _proj_ln_kernel(a_ref, w_ref, res_ref, g_ref, b_ref, o_ref):
    y = _dot(a_ref[...], w_ref[...])
    o_ref[...] = _layer_norm(ALPHA * res_ref[...] + y, g_ref[...], b_ref[...])


def _proj_res_ln(a, w, res, g, b):
    m, k = a.shape
    d = w.shape[1]
    tm = _pick(m, (768, 384, 256, 128))
    row = lambda i: (i, 0)
    fixed = lambda i: (0, 0)
    return pl.pallas_call(
        _proj_ln_kernel,
        grid=(m // tm,),
        in_specs=[pl.BlockSpec((tm, k), row), pl.BlockSpec((k, d), fixed), pl.BlockSpec((tm, d), row),
                  pl.BlockSpec((1, d), fixed), pl.BlockSpec((1, d), fixed)],
        out_specs=pl.BlockSpec((tm, d), row),
        out_shape=jax.ShapeDtypeStruct((m, d), F32),
        compiler_params=_params("parallel"),
        name="proj_res_ln",
    )(a, w, res, g, b)


def _split_bf16(x):
    hi = x.astype(BF16)
    return hi, (x - hi.astype(F32)).astype(BF16)


def _router_weights(w):
    d, ne = w.shape
    w_hi, w_lo = _split_bf16(w.astype(F32))
    zeros = lambda n: jnp.zeros((d, n), BF16)
    return jnp.stack([jnp.concatenate([w_hi, w_lo, zeros(LANES - 2 * ne)], axis=1),
                      jnp.concatenate([w_hi, zeros(LANES - ne)], axis=1)])


def _route(h, w_router):
    h_hi, h_lo = _split_bf16(h)
    first = jnp.dot(h_hi, w_router[0], preferred_element_type=F32)
    logits = (first + pltpu.roll(first, LANES - N_EXPERTS, axis=1)
              + jnp.dot(h_lo, w_router[1], preferred_element_type=F32))
    lane = lax.broadcasted_iota(jnp.int32, logits.shape, 1).astype(F32)
    neg = -jnp.inf
    lg = jnp.where(lane < N_EXPERTS, logits, neg)
    m1 = jnp.max(lg, axis=-1, keepdims=True)
    i1 = jnp.min(jnp.where(lg == m1, lane, float(LANES)), axis=-1, keepdims=True)
    lg2 = jnp.where(lane == i1, neg, lg)
    m2 = jnp.max(lg2, axis=-1, keepdims=True)
    i2 = jnp.min(jnp.where(lg2 == m2, lane, float(LANES)), axis=-1, keepdims=True)
    e2 = jnp.exp(m2 - m1)
    den = 1.0 + e2
    return jnp.where(lane == 0.0, i1, jnp.where(lane == 1.0, i2, jnp.where(
        lane == 2.0, 1.0 / den, jnp.where(lane == 3.0, e2 / den, 0.0))))


def _proj_ln_route_kernel(a_ref, m_ref, w_ref, res_ref, g_ref, b_ref, wr_ref, o_ref, ro_ref):
    a = a_ref[...].astype(F32) * m_ref[...].astype(F32)
    h = _layer_norm(ALPHA * res_ref[...] + _dot(a, w_ref[...]), g_ref[...], b_ref[...])
    o_ref[...] = h
    ro_ref[...] = _route(h, wr_ref[...])


def _proj_res_ln_route(a, mult, w, res, g, b, w_router):
    m, k = a.shape
    d = w.shape[1]
    tm = _pick(m, (768, 384, 256, 128))
    row = lambda i: (i, 0)
    fixed = lambda i: (0, 0)
    return pl.pallas_call(
        _proj_ln_route_kernel,
        grid=(m // tm,),
        in_specs=[pl.BlockSpec((tm, k), row), pl.BlockSpec((tm, k), row), pl.BlockSpec((k, d), fixed),
                  pl.BlockSpec((tm, d), row), pl.BlockSpec((1, d), fixed), pl.BlockSpec((1, d), fixed),
                  pl.BlockSpec((2, d, LANES), lambda i: (0, 0, 0))],
        out_specs=[pl.BlockSpec((tm, d), row), pl.BlockSpec((tm, LANES), row)],
        out_shape=[jax.ShapeDtypeStruct((m, d), F32), jax.ShapeDtypeStruct((m, LANES), F32)],
        compiler_params=_params("parallel"),
        name="proj_res_ln_route",
    )(a, mult, w, res, g, b, w_router)


def _silu_mul(gate, up):
    return gate * (1.0 / (1.0 + jnp.exp(-gate))) * up


def _swiglu_partial(xb, wg, wu, wd):
    gate = jnp.dot(xb, wg, preferred_element_type=F32)
    up = jnp.dot(xb, wu, preferred_element_type=F32)
    return jnp.dot(_silu_mul(gate, up).astype(BF16), wd, preferred_element_type=F32)


def _ffn_kernel(x_ref, wg_ref, wu_ref, wd_ref, g_ref, b_ref, o_ref, acc_ref, xb_ref, *, lp, nbatch, tm, nf):
    i = pl.program_id(0)
    f = pl.program_id(1)
    partial = lambda: _swiglu_partial(xb_ref[...], wg_ref[...], wu_ref[...], wd_ref[...])

    @pl.when(f == 0)
    def _():
        xb_ref[...] = x_ref[...].astype(BF16)
        if nf > 1:
            acc_ref[...] = partial()

    if nf > 2:
        @pl.when(jnp.logical_and(f > 0, f < nf - 1))
        def _():
            acc_ref[...] += partial()

    @pl.when(f == nf - 1)
    def _():
        ff = partial() + acc_ref[...] if nf > 1 else partial()
        y = _layer_norm(ALPHA * x_ref[...] + ff, g_ref[...], b_ref[...])
        o_ref[...] = jnp.where(_pad_row_mask(i * tm, tm, lp, nbatch), 0.0, y)


def _ffn(x, w_gu, w_down, g, b, lp, nbatch):
    m, d = x.shape
    ff = w_down.shape[0]
    tm = _pick(m, (768, 384, 256, 128))
    tf = _pick(ff, (1408, 896, 512, 256, 128))
    nf = ff // tf
    row = lambda i, f: (i, 0)
    fixed = lambda i, f: (0, 0)
    fblk = lambda i, f: jnp.where(lax.rem(i, 2) == 1, nf - 1 - f, f)
    return pl.pallas_call(
        functools.partial(_ffn_kernel, lp=lp, nbatch=nbatch, tm=tm, nf=nf),
        grid=(m // tm, nf),
        in_specs=[pl.BlockSpec((tm, d), row),
                  pl.BlockSpec((d, tf), lambda i, f: (0, fblk(i, f))),
                  pl.BlockSpec((d, tf), lambda i, f: (0, fblk(i, f) + nf)),
                  pl.BlockSpec((tf, d), lambda i, f: (fblk(i, f), 0)),
                  pl.BlockSpec((1, d), fixed), pl.BlockSpec((1, d), fixed)],
        out_specs=pl.BlockSpec((tm, d), row),
        out_shape=jax.ShapeDtypeStruct((m, d), F32),
        scratch_shapes=[pltpu.VMEM((tm, d), F32), pltpu.VMEM((tm, d), BF16)],
        compiler_params=_params("parallel", "arbitrary"),
        name="swiglu_res_ln",
    )(x, w_gu, w_gu, w_down, g, b)


def _moe_kernel(tile_e_ref, nact_ref, src_ref, src_next_ref, dst_ref, dst_prev_ref, x_hbm, wg_ref, wu_ref,
                wd_ref, y_hbm, xbuf, xb_ref, acc_ref, ybuf, gsem, ssem, *, tm, nf):
    j = pl.program_id(0)
    f = pl.program_id(1)
    last_tile = j == pl.num_programs(0) - 1
    active = j < nact_ref[0]
    slot = lax.rem(j, 2)
    other = 1 - slot

    def gather_row(idx_ref, s, i):
        t = idx_ref[0, 0, i]
        pltpu.make_async_copy(x_hbm.at[pl.ds(t, 1)], xbuf.at[s, pl.ds(i, 1)], gsem.at[s]).start()

    def scatter_row(idx_ref, s, i, priority=0):
        t = idx_ref[0, 0, i]
        pltpu.make_async_copy(ybuf.at[s, pl.ds(i, 1)], y_hbm.at[pl.ds(t, 1)], ssem.at[s]).start(priority=priority)

    def gather_wait(s):
        pltpu.make_async_copy(x_hbm.at[pl.ds(0, tm)], xbuf.at[s], gsem.at[s]).wait()

    def scatter_wait(s):
        pltpu.make_async_copy(ybuf.at[s], y_hbm.at[pl.ds(0, tm)], ssem.at[s]).wait()

    @pl.when(f == 0)
    def _():
        @pl.when(j == 0)
        def _():
            ybuf[...] = jnp.zeros_like(ybuf)

            def body(i, carry):
                gather_row(src_ref, 0, i)
                return carry
            lax.fori_loop(0, tm, body, 0, unroll=DMA_ISSUE_UNROLL)

        gather_wait(slot)
        xb_ref[...] = xbuf[slot].astype(BF16)
        for i in range(tm):
            gather_row(src_next_ref, other, i)
            scatter_row(dst_prev_ref, other, i, priority=i % 2)

    partial = lambda: _swiglu_partial(xb_ref[...], wg_ref[0], wu_ref[0], wd_ref[0])

    if nf > 1:
        @pl.when(jnp.logical_and(active, f == 0))
        def _():
            acc_ref[...] = partial()

    if nf > 2:
        @pl.when(jnp.logical_and(active, jnp.logical_and(f > 0, f < nf - 1)))
        def _():
            acc_ref[...] += partial()

    @pl.when(f == nf - 1)
    def _():
        @pl.when(j >= 1)
        def _():
            scatter_wait(slot)

        @pl.when(active)
        def _():
            ybuf[slot] = partial() + acc_ref[...] if nf > 1 else partial()

        @pl.when(jnp.logical_not(active))
        def _():
            ybuf[slot] = jnp.zeros(ybuf.shape[1:], ybuf.dtype)

        @pl.when(last_tile)
        def _():
            def body(i, carry):
                scatter_row(dst_ref, slot, i)
                return carry
            lax.fori_loop(0, tm, body, 0, unroll=DMA_ISSUE_UNROLL)
            scatter_wait(slot)
            scatter_wait(other)
            gather_wait(other)


def _moe_experts(x, expert_idx, w_gu, w_down):
    m, d = x.shape
    ne, ff = w_down.shape[0], w_down.shape[1]
    nslots = 2 * m
    tm = _pick(nslots, (512, 256, 128))
    tf = _pick(ff, (1792, 896, 512, 256, 128))
    nf = ff // tf
    n_tiles = nslots // tm + ne
    p_rows = n_tiles * tm

    e_flat = expert_idx.reshape(nslots)
    slot_bits = max(1, (nslots - 1).bit_length())
    assert ne << slot_bits < 2 ** 31
    order = jnp.bitwise_and(jnp.sort(jnp.left_shift(e_flat, slot_bits) + jnp.arange(nslots, dtype=jnp.int32)),
                            (1 << slot_bits) - 1)
    counts = jnp.sum((e_flat[:, None] == jnp.arange(ne, dtype=jnp.int32)[None, :]).astype(jnp.int32), axis=0)
    gsize = ((counts + tm - 1) // tm) * tm
    gend = jnp.cumsum(gsize)
    tile_start = jnp.arange(n_tiles, dtype=jnp.int32) * tm
    tile_e = jnp.minimum(jnp.sum((gend[None, :] <= tile_start[:, None]).astype(jnp.int32), axis=1), ne - 1)
    pos = jnp.arange(p_rows, dtype=jnp.int32)
    pos_e = jnp.repeat(tile_e, tm)
    rank = pos - (gend - gsize)[pos_e]
    is_fill = rank >= counts[pos_e]
    slot_sorted = order[jnp.clip((jnp.cumsum(counts) - counts)[pos_e] + rank, 0, nslots - 1)]
    fill_rank = jnp.cumsum(is_fill.astype(jnp.int32)) - 1
    src_sorted = jnp.where(is_fill, 0, slot_sorted // 2)
    dst_sorted = jnp.where(is_fill, nslots + fill_rank, (slot_sorted % 2) * m + slot_sorted // 2)
    nact = (gend[-1:] // tm).astype(jnp.int32)
    src3 = src_sorted.reshape(n_tiles, 1, tm)
    dst3 = dst_sorted.reshape(n_tiles, 1, tm)

    def f_eff(j, f, nact_ref):
        snake = lambda jj, ff_: jnp.where(lax.rem(jj, 2) == 1, nf - 1 - ff_, ff_)
        return jnp.where(j < nact_ref[0], snake(j, f), snake(nact_ref[0] - 1, nf - 1))

    smem_blk = lambda imap: pl.BlockSpec((1, 1, tm), imap, memory_space=pltpu.SMEM)
    grid_spec = pltpu.PrefetchScalarGridSpec(
        num_scalar_prefetch=2,
        grid=(n_tiles, nf),
        in_specs=[smem_blk(lambda j, f, te, na: (j, 0, 0)),
                  smem_blk(lambda j, f, te, na: (jnp.minimum(j + 1, n_tiles - 1), 0, 0)),
                  smem_blk(lambda j, f, te, na: (j, 0, 0)),
                  smem_blk(lambda j, f, te, na: (jnp.where(j == 0, n_tiles - 1, j - 1), 0, 0)),
                  pl.BlockSpec(memory_space=pl.ANY),
                  pl.BlockSpec((1, d, tf), lambda j, f, te, na: (te[j], 0, f_eff(j, f, na))),
                  pl.BlockSpec((1, d, tf), lambda j, f, te, na: (te[j], 0, f_eff(j, f, na) + nf)),
                  pl.BlockSpec((1, tf, d), lambda j, f, te, na: (te[j], f_eff(j, f, na), 0))],
        out_specs=pl.BlockSpec(memory_space=pl.ANY),
        scratch_shapes=[pltpu.VMEM((2, tm, d), F32), pltpu.VMEM((tm, d), BF16), pltpu.VMEM((tm, d), F32),
                        pltpu.VMEM((2, tm, d), F32), pltpu.SemaphoreType.DMA((2,)),
                        pltpu.SemaphoreType.DMA((2,))],
    )
    return pl.pallas_call(
        functools.partial(_moe_kernel, tm=tm, nf=nf),
        grid_spec=grid_spec,
        out_shape=jax.ShapeDtypeStruct((p_rows, d), F32),
        compiler_params=_params("arbitrary", "arbitrary"),
        name="moe_grouped_swiglu",
    )(tile_e, nact, src3, src3, dst3, dst3, x, w_gu, w_gu, w_down)


def _combine_kernel(*refs, sub):
    g_ref, b_ref, o_ref = refs[4 * sub:]
    for s in range(sub):
        x_ref, y1_ref, y2_ref, ro_ref = refs[4 * s:4 * s + 4]
        ro = ro_ref[...]
        lane = lax.broadcasted_iota(jnp.int32, ro.shape, 1)
        w1 = jnp.sum(jnp.where(lane == 2, ro, 0.0), axis=-1, keepdims=True)
        w2 = jnp.sum(jnp.where(lane == 3, ro, 0.0), axis=-1, keepdims=True)
        z = ALPHA * x_ref[...] + w1 * y1_ref[...] + w2 * y2_ref[...]
        o_ref[0, s * BLOCK:(s + 1) * BLOCK] = _layer_norm(z, g_ref[...], b_ref[...])


def _moe_combine(x, y_slots, router_out, g, b, nbatch, lp):
    m, d = x.shape
    nb = lp // BLOCK
    sub = _pick(nb - 1, (4, 2, 1))
    choice2 = m // BLOCK
    fixed = lambda i, j: (0, 0)
    ins, specs = [], []
    for s in range(sub):
        row = lambda i, j, s=s: (i * nb + j * sub + s + 1, 0)
        row2 = lambda i, j, s=s: (choice2 + i * nb + j * sub + s + 1, 0)
        ins += [x, y_slots, y_slots, router_out]
        specs += [pl.BlockSpec((BLOCK, d), row), pl.BlockSpec((BLOCK, d), row), pl.BlockSpec((BLOCK, d), row2),
                  pl.BlockSpec((BLOCK, LANES), row)]
    return pl.pallas_call(
        functools.partial(_combine_kernel, sub=sub),
        grid=(nbatch, (nb - 1) // sub),
        in_specs=specs + [pl.BlockSpec((1, d), fixed), pl.BlockSpec((1, d), fixed)],
        out_specs=pl.BlockSpec((1, sub * BLOCK, d), lambda i, j: (i, j, 0)),
        out_shape=jax.ShapeDtypeStruct((nbatch, lp - BLOCK, d), F32),
        compiler_params=_params("parallel", "parallel"),
        name="moe_combine_ln",
    )(*ins, g, b)


def _rwkv_proj_kernel(x_ref, xp_ref, mu_ref, wr_ref, wk_ref, wv_ref, w0_ref, w1_ref, w2_ref,
                      a0_ref, a1_ref, a2_ref, g1_ref, g2_ref,
                      r_ref, k_ref, v_ref, a_ref, ld_ref, g_ref, *, lp, nbatch):
    x = x_ref[...]
    tm = x.shape[0]
    rolled = pltpu.roll(x, 1, axis=0)
    prev_row = xp_ref[7:8, :]
    first = lax.broadcasted_iota(jnp.int32, (tm, 1), 0) == 0
    xx = jnp.where(first, prev_row, rolled) - x
    xx = jnp.where(_pad_row_mask(pl.program_id(0) * tm, tm, lp, nbatch), 0.0, xx)

    def mix(i):
        return x + xx * mu_ref[i:i + 1, :]

    r_ref[...] = _dot(mix(0), wr_ref[...]).astype(r_ref.dtype)
    k_ref[...] = _dot(mix(2), wk_ref[...]).astype(k_ref.dtype)
    v_ref[...] = _dot(mix(3), wv_ref[...]).astype(v_ref.dtype)
    wl = w0_ref[...] + _dot(jnp.tanh(_dot(mix(1), w1_ref[...])), w2_ref[...])
    z = -wl
    softplus = jnp.maximum(z, 0.0) + jnp.log(1.0 + jnp.exp(-jnp.abs(z)))
    ld_ref[...] = -jnp.exp(-softplus - 0.5)
    al = a0_ref[...] + _dot(_dot(mix(4), a1_ref[...]), a2_ref[...])
    a_ref[...] = (1.0 / (1.0 + jnp.exp(-al))).astype(a_ref.dtype)
    gl = _dot(mix(5), g1_ref[...])
    g_ref[...] = _dot(1.0 / (1.0 + jnp.exp(-gl)), g2_ref[...]).astype(g_ref.dtype)


def _rwkv_proj(x, mu, w_rkv, w0, w1, w2, a0, a1, a2, g1, g2, lp, nbatch):
    m, d = x.shape
    tm = _pick(m, (768, 384, 256, 128))
    row = lambda i: (i, 0)
    fixed = lambda i: (0, 0)
    full = lambda arr: pl.BlockSpec(arr.shape, fixed)
    out = lambda dt: jax.ShapeDtypeStruct((m, d), dt)
    return pl.pallas_call(
        functools.partial(_rwkv_proj_kernel, lp=lp, nbatch=nbatch),
        grid=(m // tm,),
        in_specs=[pl.BlockSpec((tm, d), row),
                  pl.BlockSpec((8, d), lambda i: (jnp.maximum(i * (tm // 8) - 1, 0), 0)),
                  full(mu), full(w_rkv[0]), full(w_rkv[1]), full(w_rkv[2]),
                  full(w0), full(w1), full(w2), full(a0), full(a1), full(a2), full(g1), full(g2)],
        out_specs=[pl.BlockSpec((tm, d), row)] * 6,
        out_shape=[out(BF16), out(BF16), out(BF16), out(BF16), out(F32), out(BF16)],
        compiler_params=_params("parallel"),
        name="rwkv_proj",
    )(x, x, mu, w_rkv[0], w_rkv[1], w_rkv[2], w0, w1, w2, a0, a1, a2, g1, g2)


def _bmm(a, b):
    return jnp.einsum("cij,cjk->cik", a.astype(BF16), b.astype(BF16), preferred_element_type=F32)


def _bmm_nt(a, b):
    return jnp.einsum("cik,cjk->cij", a.astype(BF16), b.astype(BF16), preferred_element_type=F32)


def _bmm_tn(a, b):
    return jnp.einsum("cki,ckj->cij", a.astype(BF16), b.astype(BF16), preferred_element_type=F32)


def _wkv_streams(r, k, v, a, ld, k_k, k_a, r_k, gn_g, gn_b, z):
    c = WKV_CHUNK
    n = RWKV_HEAD
    ns, tb, _ = r.shape
    nc = tb // c
    r, k, v, a = (t.astype(F32) for t in (r, k, v, a))
    head0 = lax.broadcasted_iota(jnp.int32, (1, 1, LANES), 2) < n

    def head_sum(x):
        s0 = jnp.sum(jnp.where(head0, x, 0.0), axis=-1, keepdims=True)
        s1 = jnp.sum(jnp.where(head0, 0.0, x), axis=-1, keepdims=True)
        return jnp.where(head0, s0, s1)

    def stack(x):
        xb = x.astype(BF16)
        zero = jnp.zeros_like(xb)
        return jnp.concatenate([jnp.where(head0, xb, zero), jnp.where(head0, zero, xb)], axis=1)

    def fold(x):
        return x[:, :c] + x[:, c:]

    kkr = k * k_k
    kk = kkr / jnp.maximum(jnp.sqrt(head_sum(kkr * kkr)), 1e-12)
    km = k * (1.0 + (a - 1.0) * k_a)
    bv = kk * a

    tpos = jnp.bitwise_and(lax.broadcasted_iota(jnp.int32, (ns * tb, 1), 0), c - 1)
    cs = ld.reshape(ns * tb, LANES)
    shift = 1
    while shift < c:
        cs = cs + jnp.where(tpos >= shift, pltpu.roll(cs, shift, axis=0), 0.0)
        shift *= 2

    to3 = lambda x: x.reshape(ns * nc, c, LANES)
    cs3, ld3 = to3(cs), to3(ld)
    cs_last = cs3[:, c - 1:c, :]
    e_neg = jnp.exp(-cs3)
    e_tail = jnp.exp(cs_last - cs3)
    at_s = stack(-to3(kk) * jnp.exp(cs3 - ld3))
    rt = to3(r) * jnp.exp(cs3)
    rt_s = stack(rt)
    bt_s = stack(to3(bv) * e_neg)
    kt_s = stack(to3(km) * e_neg)
    bh_s = stack(to3(bv) * e_tail)
    kh_s = stack(to3(km) * e_tail)
    v_s = stack(to3(v))
    dec = jnp.exp(cs_last)

    c2 = 2 * c
    g = _bmm_nt(jnp.concatenate([at_s, rt_s], axis=1), jnp.concatenate([bt_s, kt_s], axis=1))
    ti = jnp.bitwise_and(lax.broadcasted_iota(jnp.int32, (1, c2, c2), 1), c - 1)
    si = jnp.bitwise_and(lax.broadcasted_iota(jnp.int32, (1, c2, c2), 2), c - 1)
    strict = ti > si
    incl = ti >= si
    lab = jnp.where(strict, g[:, :c2, :c2], 0.0)
    lab_b = lab.astype(BF16)
    lak = jnp.where(strict, g[:, :c2, c2:], 0.0).astype(BF16)
    mrb = jnp.where(incl, g[:, c2:, :c2], 0.0).astype(BF16)
    mrk = jnp.where(incl, g[:, c2:, c2:], 0.0).astype(BF16)

    eye2 = lax.broadcasted_iota(jnp.int32, (1, c2, c2), 1) == lax.broadcasted_iota(jnp.int32, (1, c2, c2), 2)
    t_inv = jnp.where(eye2, 1.0, lab)
    lpow = _bmm(lab_b, lab_b).astype(BF16)
    for _ in range(int(math.log2(c)) - 2):
        both = _bmm(jnp.concatenate([lpow, t_inv.astype(BF16)], axis=1), lpow)
        t_inv = t_inv + both[:, c2:]
        lpow = both[:, :c2].astype(BF16)
    t_inv = t_inv + _bmm(t_inv, lpow)

    lv = _bmm(jnp.concatenate([lak, mrk], axis=1), v_s)
    lakv, mrkv = lv[:, :c2], lv[:, c2:]
    wu = _bmm(t_inv, jnp.concatenate([at_s, lakv.astype(BF16)], axis=2)).astype(BF16)
    qy = _bmm(mrb, wu)
    q = (rt + fold(qy[:, :, :LANES])).astype(BF16)
    yp = fold(qy[:, :, LANES:] + mrkv)
    eye_l = (lax.broadcasted_iota(jnp.int32, (1, LANES, LANES), 1)
             == lax.broadcasted_iota(jnp.int32, (1, LANES, LANES), 2))
    a_t = (jnp.where(eye_l, dec, 0.0) + _bmm_tn(bh_s, wu[:, :, :LANES])).astype(BF16)
    g_t = _bmm_tn(jnp.concatenate([bh_s, kh_s], axis=1), jnp.concatenate([wu[:, :, LANES:], v_s], axis=1))

    per_stream = lambda x: x.reshape(ns, nc, *x.shape[1:])
    q, yp, a_t, g_t = per_stream(q), per_stream(yp), per_stream(a_t), per_stream(g_t)
    ys = []
    for ci in range(nc):
        both = _bmm(jnp.concatenate([q[:, ci], a_t[:, ci]], axis=1), z)
        ys.append(yp[:, ci] + both[:, :c])
        z = both[:, c:] + g_t[:, ci]
    y = jnp.concatenate(ys, axis=1)

    mu = head_sum(y) * (1.0 / n)
    yc = y - mu
    var = head_sum(yc * yc) * (1.0 / n)
    yn = yc * lax.rsqrt(var + GN_EPS) * gn_g + gn_b
    bonus = head_sum(r * km * r_k) * v
    return yn + bonus, z


def _wkv_kernel(r_ref, k_ref, v_ref, a_ref, ld_ref, kk_ref, ka_ref, rk_ref, gg_ref, gb_ref,
                y_ref, z_ref):
    @pl.when(pl.program_id(1) == 0)
    def _():
        z_ref[...] = jnp.zeros_like(z_ref)

    nbatch, npairs = r_ref.shape[0], r_ref.shape[2] // LANES
    streams = [(bi, slice(pi * LANES, (pi + 1) * LANES)) for bi in range(nbatch) for pi in range(npairs)]
    tok = lambda ref: jnp.stack([ref[bi, :, ls] for bi, ls in streams])
    par = lambda ref: jnp.stack([ref[:, ls] for _, ls in streams])
    y, z = _wkv_streams(tok(r_ref), tok(k_ref), tok(v_ref), tok(a_ref), tok(ld_ref), par(kk_ref), par(ka_ref),
                        par(rk_ref), par(gg_ref), par(gb_ref), z_ref[...])
    for si, (bi, ls) in enumerate(streams):
        y_ref[bi, :, ls] = y[si]
    z_ref[...] = z


def _wkv(r, k, v, a, ld, k_k, k_a, r_k, gn_g, gn_b):
    b, lp, d = r.shape
    tb = _pick(lp, (WKV_BLOCK_TOKENS, 128, 64))
    width = WKV_BLOCK_PAIRS * LANES
    tok_spec = pl.BlockSpec((b, tb, width), lambda j, t: (0, t, j))
    par_spec = pl.BlockSpec((1, width), lambda j, t: (0, j))
    return pl.pallas_call(
        _wkv_kernel,
        grid=(d // width, lp // tb),
        in_specs=[tok_spec] * 5 + [par_spec] * 5,
        out_specs=tok_spec,
        out_shape=jax.ShapeDtypeStruct((b, lp, d), F32),
        scratch_shapes=[pltpu.VMEM((b * WKV_BLOCK_PAIRS, LANES, LANES), F32)],
        compiler_params=_params("parallel", "arbitrary"),
        name="wkv7_chunked",
    )(r, k, v, a, ld, k_k, k_a, r_k, gn_g, gn_b)


def _t5_bucket(dist):
    n = jnp.maximum(dist, 0)
    is_small = n < MAX_EXACT
    nf = jnp.maximum(n, 1).astype(F32)
    large = MAX_EXACT + (jnp.log(nf / MAX_EXACT) / math.log(MAX_DISTANCE / MAX_EXACT)
                         * (NUM_BUCKETS - MAX_EXACT)).astype(jnp.int32)
    large = jnp.minimum(large, NUM_BUCKETS - 1)
    return jnp.where(is_small, n, large)


def _bias_tables(rel_bias):
    rb = rel_bias.astype(F32)
    d_band = BLOCK + jnp.arange(BLOCK)[:, None] - jnp.arange(2 * BLOCK)[None, :]
    ok = (d_band >= 0) & (d_band < WINDOW)
    onehot = (_t5_bucket(d_band)[..., None] == jnp.arange(NUM_BUCKETS)).astype(F32)
    looked_up = jnp.einsum("qkn,nh->hqk", onehot, rb, precision=lax.Precision.HIGHEST)
    band = jnp.where(ok[None], looked_up, NEG_INF)
    col = jnp.arange(2 * BLOCK)
    band3 = jnp.stack([jnp.where((col + (n - 1) * BLOCK >= PAD)[None, None, :], band, NEG_INF) for n in range(3)])
    meta = rb[_t5_bucket(jnp.array(WINDOW))]
    mcol = jnp.arange(BLOCK)[None, :] - PAD
    d_meta = lambda n: n * BLOCK + jnp.arange(BLOCK)[:, None] - PAD - mcol
    meta3 = jnp.stack([jnp.where(((mcol >= 0) & (d_meta(n) >= WINDOW))[None], meta[:, None, None], NEG_INF)
                       for n in range(3)])
    return band3, meta3


def kernel(x, meta_tokens, rel_bias, ln_mix_g, ln_mix_b, ln_ffn_g, ln_ffn_b, attn_w_qkv, attn_b_qkv, attn_sinks, attn_w_o, rwkv_mu, rwkv_w0, rwkv_w1, rwkv_w2, rwkv_a0, rwkv_a1, rwkv_a2, rwkv_g1, rwkv_g2, rwkv_k_k, rwkv_k_a, rwkv_r_k, rwkv_w_rkv, rwkv_lnx_g, rwkv_lnx_b, rwkv_w_o, ffn_w_gu, ffn_w_down, moe_router, moe_w_gu, moe_w_down):
    b, seq, d = x.shape
    lp = seq + BLOCK
    m = b * lp
    row = lambda t: t.reshape(1, -1).astype(F32)

    meta = jnp.broadcast_to(meta_tokens.astype(x.dtype)[None], (b, N_META, d))
    h = jnp.concatenate([jnp.zeros((b, PAD, d), x.dtype), meta, x], axis=1).reshape(m, d)

    q_dim = N_HEADS * HEAD_DIM
    e_dim = attn_w_qkv.shape[2]
    col_scale = jnp.where(jnp.arange(e_dim) < q_dim, ATTN_SCALE, 1.0).astype(F32)
    qkv = _qkv_proj(h, attn_w_qkv[0].astype(BF16), row(attn_b_qkv[0]), row(col_scale))
    bias_band, bias_meta = _bias_tables(rel_bias)
    o = _attention(qkv.reshape(b, lp, e_dim), bias_band, bias_meta, attn_sinks[0].astype(F32))
    h = _proj_res_ln(o.reshape(m, q_dim), attn_w_o[0].astype(BF16), h, row(ln_mix_g[0]), row(ln_mix_b[0]))
    h = _ffn(h, ffn_w_gu[0].astype(BF16), ffn_w_down[0].astype(BF16), row(ln_ffn_g[0]), row(ln_ffn_b[0]), lp, b)

    r, k, v, a, ld, g = _rwkv_proj(
        h, rwkv_mu[0], rwkv_w_rkv[0].astype(BF16), row(rwkv_w0[0]), rwkv_w1[0].astype(BF16),
        rwkv_w2[0].astype(BF16), row(rwkv_a0[0]), rwkv_a1[0].astype(BF16), rwkv_a2[0].astype(BF16),
        rwkv_g1[0].astype(BF16), rwkv_g2[0].astype(BF16), lp, b)
    t3 = lambda t: t.reshape(b, lp, d)
    y = _wkv(t3(r), t3(k), t3(v), t3(a), t3(ld), row(rwkv_k_k[0]), row(rwkv_k_a[0]), row(rwkv_r_k[0]),
             row(rwkv_lnx_g[0]), row(rwkv_lnx_b[0]))
    w_router = _router_weights(moe_router[0])
    h, routed = _proj_res_ln_route(y.reshape(m, d), g, rwkv_w_o[0].astype(BF16), h, row(ln_mix_g[1]),
                                   row(ln_mix_b[1]), w_router)
    expert_idx = routed[:, :2].astype(jnp.int32)
    y_slots = _moe_experts(h, expert_idx, moe_w_gu[0].astype(BF16), moe_w_down[0].astype(BF16))
    return _moe_combine(h, y_slots, routed, row(ln_ffn_g[1]), row(ln_ffn_b[1]), b, lp)
```
